```python
import jax, jax.numpy as jnp
from jax import lax
import numpy as np

D_MODEL = 1024
BATCH = 16
SEQ = 256
DEPTH = 2
DEC_BATCH = 4
DEC_SEQ = 2048
PAST_LEN = 512

GRID_W = 64
N_BRANCH = 3
N_DIR = 2
MLSTM_HEADS = 4
MLSTM_WIDTH = D_MODEL
MLSTM_HEAD_DIM = MLSTM_WIDTH // MLSTM_HEADS
MLSTM_CHUNK = 128
POOL_WINDOWS = (2, 4, 8, 16)
POOL_GROUPS = 4
POOL_WIDTH = D_MODEL
POOL_GROUP_DIM = POOL_WIDTH // POOL_GROUPS
GMLP_WIDTH = D_MODEL
GMLP_GROUPS = 4
GMLP_GROUP_DIM = GMLP_WIDTH // GMLP_GROUPS
GMLP_CHUNK = 128
D_FF = 2816
CONV_K = 3
N_MOD = 6
N_GATE_COLS = N_DIR * 2 * MLSTM_HEADS
IN_WIDTH = 4 * MLSTM_WIDTH + N_GATE_COLS + POOL_WIDTH + 2 * GMLP_WIDTH + N_BRANCH * D_MODEL
RMS_EPS = 1e-6
INPUT_GATE_BIAS = -0.5
FORGET_GATE_BIAS = 3.0

kernel_name = 'hybrid_mlstm_pool_gmlp_diffusion_step'


def _split_points():
    sizes = (MLSTM_WIDTH,) * 4 + (N_GATE_COLS, POOL_WIDTH, GMLP_WIDTH, GMLP_WIDTH, N_BRANCH * D_MODEL)
    pts, acc = [], 0
    for s in sizes[:-1]:
        acc += s
        pts.append(acc)
    return pts


def _rmsnorm(x, w):
    xf = x.astype(jnp.float32)
    xf = xf * lax.rsqrt(jnp.mean(xf * xf, axis=-1, keepdims=True) + RMS_EPS)
    return (xf * w.astype(jnp.float32)).astype(x.dtype)


def _to_chunks(a, L):
    b, g, t = a.shape[:3]
    a = a.reshape((b, g, t // L, L) + a.shape[3:])
    a = jnp.moveaxis(a, 2, 0)
    return jnp.swapaxes(a, 3, 4)


def _from_chunks(a):
    a = jnp.swapaxes(a, 3, 4)
    a = jnp.moveaxis(a, 0, 2)
    b, g, nc, L = a.shape[:4]
    return a.reshape((b, g, nc * L) + a.shape[4:])


def _mlstm_bidir(q, k, v, ig, fg, C0, n0, m0):
    f32 = jnp.float32

    def both(a):
        a = a.astype(f32)
        return jnp.stack([a, a[:, ::-1]], axis=1)

    qd = both(q)
    kd = both(k) * (MLSTM_HEAD_DIM ** -0.5)
    vd = both(v)
    igd = jnp.stack([ig[:, :, 0], ig[:, ::-1, 1]], axis=1).astype(f32)
    fgd = jnp.stack([fg[:, :, 0], fg[:, ::-1, 1]], axis=1).astype(f32)
    xs = tuple(_to_chunks(a, MLSTM_CHUNK) for a in (qd, kd, vd, igd, fgd))
    tril = jnp.tril(jnp.ones((MLSTM_CHUNK, MLSTM_CHUNK), dtype=bool))

    def step(carry, inp):
        C, n, m = carry
        qc, kc, vc, ic, fc = inp
        b = jnp.cumsum(jax.nn.log_sigmoid(fc), axis=-1)
        dmat = jnp.where(tril, b[..., :, None] - b[..., None, :] + ic[..., None, :], -jnp.inf)
        inter = b + m[..., None]
        m_t = jnp.maximum(inter, jnp.max(dmat, axis=-1))
        w_intra = jnp.exp(dmat - m_t[..., None])
        w_inter = jnp.exp(inter - m_t)
        s = jnp.einsum('bghtd,bghsd->bghts', qc, kc) * w_intra
        num = (jnp.einsum('bghts,bghsd->bghtd', s, vc)
               + w_inter[..., None] * jnp.einsum('bghvd,bghtd->bghtv', C, qc))
        den = jnp.sum(s, axis=-1) + w_inter * jnp.einsum('bghd,bghtd->bght', n, qc)
        h = num / jnp.maximum(jnp.abs(den), jnp.exp(-m_t))[..., None]
        b_end = b[..., -1]
        g = b_end[..., None] - b + ic
        m_new = jnp.maximum(b_end + m, jnp.max(g, axis=-1))
        w_k = jnp.exp(g - m_new[..., None])
        decay = jnp.exp(b_end + m - m_new)
        C_new = decay[..., None, None] * C + jnp.einsum('bghs,bghsv,bghsd->bghvd', w_k, vc, kc)
        n_new = decay[..., None] * n + jnp.einsum('bghs,bghsd->bghd', w_k, kc)
        return (C_new, n_new, m_new), h

    (C, n, m), h = lax.scan(step, (C0.astype(f32), n0.astype(f32), m0.astype(f32)), xs)
    h = _from_chunks(h)
    return h[:, 0] + h[:, 1, ::-1], (C, n, m)


def _pool_mixer(p, w_grp, scale):
    B, T, _ = p.shape
    pf = p.astype(jnp.float32).reshape(B, T, POOL_GROUPS, POOL_GROUP_DIM)
    cs = jnp.concatenate([jnp.zeros((B, 1, POOL_GROUPS, POOL_GROUP_DIM), jnp.float32),
                          jnp.cumsum(pf, axis=1)], axis=1)
    t = jnp.arange(T)
    outs = []
    for j, w in enumerate(POOL_WINDOWS):
        lo = jnp.clip(t - w // 2, 0, T)
        hi = jnp.clip(t + w // 2, 0, T)
        csj = cs[:, :, j]
        mean = (csj[:, hi] - csj[:, lo]) / (hi - lo).astype(jnp.float32)[None, :, None]
        outs.append(mean - pf[:, :, j])
    y = jnp.stack(outs, axis=2).astype(p.dtype)
    y = jnp.einsum('btgc,gcd->btgd', y, w_grp).reshape(B, T, POOL_WIDTH)
    return y * scale


def _gmlp_mixer(u, vg, norm_w, ws, bs):
    B, T, _ = u.shape
    nc = T // GMLP_CHUNK
    u = jax.nn.gelu(u)
    vg = _rmsnorm(jax.nn.gelu(vg), norm_w)
    vc = vg.reshape(B, nc, GMLP_CHUNK, GMLP_GROUPS, GMLP_GROUP_DIM)
    s = jnp.einsum('gij,bcjgd->bcigd', ws, vc) + bs.T[None, None, :, :, None]
    return u * s.reshape(B, T, GMLP_WIDTH)


def _conv_ffn(h, w_up, conv_w, conv_b, w_down, rows, width):
    B, T, _ = h.shape
    a, g = jnp.split(h @ w_up, 2, axis=-1)
    a = a.reshape(B, rows, width, D_FF)
    a = lax.conv_general_dilated(a, conv_w[:, :, None, :], (1, 1), 'SAME',
                                 dimension_numbers=('NHWC', 'HWIO', 'NHWC'),
                                 feature_group_count=D_FF)
    a = a.reshape(B, T, D_FF) + conv_b
    return (jax.nn.gelu(a) * g) @ w_down


def _trunk_layer(x, cvec, C0, n0, m0, rows, width, p):
    B, T, D = x.shape
    mod = (jax.nn.silu(cvec) @ p['w_mod'] + p['b_mod']).reshape(B, N_MOD, 1, D)
    sh1, sc1, g1, sh2, sc2, g2 = (mod[:, i] for i in range(N_MOD))
    nw = p['norm_w']
    h = _rmsnorm(x, nw[0]) * (1 + sc1) + sh1
    z = h @ p['w_in'] + p['b_in']
    zq, zk, zv, zo, zg, zp, zu, zvg, zm = jnp.split(z, _split_points(), axis=-1)
    gates = zg.reshape(B, T, N_DIR, 2, MLSTM_HEADS) + p['mlstm_gate_b']
    hd = (B, T, MLSTM_HEADS, MLSTM_HEAD_DIM)
    hm, st = _mlstm_bidir(zq.reshape(hd), zk.reshape(hd), zv.reshape(hd),
                          gates[:, :, :, 0], gates[:, :, :, 1], C0, n0, m0)
    ya = jax.nn.sigmoid(zo) * hm.reshape(B, T, MLSTM_WIDTH).astype(x.dtype)
    yb = _pool_mixer(zp, p['pool_w'], p['pool_scale'])
    yc = _gmlp_mixer(zu, zvg, p['gmlp_norm_w'], p['gmlp_ws'], p['gmlp_b'])
    br = jnp.einsum('btni,nid->btnd', jnp.stack([ya, yb, yc], axis=2), p['w_br'])
    mixed = jnp.sum(jax.nn.sigmoid(zm.reshape(B, T, N_BRANCH, D)) * br, axis=2)
    x = x + g1 * _rmsnorm(mixed @ p['w_out'], nw[1])
    h2 = _rmsnorm(x, nw[2]) * (1 + sc2) + sh2
    f = _conv_ffn(h2, p['w_up'], p['conv_w'], p['conv_b'], p['w_down'], rows, width)
    x = x + g2 * _rmsnorm(f, nw[3])
    return x, st


def setup_inputs(seed: int = 0) -> dict:
    key = jax.random.key(seed)
    ks = jax.random.split(key, 24)
    f32 = jnp.float32

    def nrm(k, shape, scale):
        return jax.random.normal(k, shape, f32) * scale

    H, DH = MLSTM_HEADS, MLSTM_HEAD_DIM
    gate_base = jnp.array([INPUT_GATE_BIAS, FORGET_GATE_BIAS], f32).reshape(1, 1, 2, 1)
    return {
        'x_prompt': nrm(ks[0], (BATCH, SEQ, D_MODEL), 1.0),
        'x_sample': nrm(ks[1], (DEC_BATCH, DEC_SEQ, D_MODEL), 1.0),
        'state_C': nrm(ks[2], (DEC_BATCH, DEPTH, N_DIR, H, DH, DH), 0.1),
        'state_n': nrm(ks[3], (DEC_BATCH, DEPTH, N_DIR, H, DH), 0.1),
        'state_m': nrm(ks[4], (DEC_BATCH, DEPTH, N_DIR, H), 0.5),
        'c': nrm(ks[5], (DEC_BATCH, D_MODEL), 1.0),
        'c_ctx': nrm(ks[6], (D_MODEL,), 1.0),
        'w_mod': nrm(ks[7], (DEPTH, D_MODEL, N_MOD * D_MODEL), 0.5 * D_MODEL ** -0.5),
        'b_mod': nrm(ks[8], (DEPTH, N_MOD * D_MODEL), 0.02),
        'norm_w': 1.0 + nrm(ks[9], (DEPTH, 4, D_MODEL), 0.05),
        'w_in': nrm(ks[10], (DEPTH, D_MODEL, IN_WIDTH), D_MODEL ** -0.5),
        'b_in': nrm(ks[11], (DEPTH, IN_WIDTH), 0.02),
        'mlstm_gate_b': gate_base + nrm(ks[12], (DEPTH, N_DIR, 2, H), 0.1),
        'pool_w': nrm(ks[13], (DEPTH, POOL_GROUPS, POOL_GROUP_DIM, POOL_GROUP_DIM), POOL_GROUP_DIM ** -0.5),
        'pool_scale': 1.0 + nrm(ks[14], (DEPTH, POOL_WIDTH), 0.05),
        'gmlp_norm_w': 1.0 + nrm(ks[15], (DEPTH, GMLP_WIDTH), 0.05),
        'gmlp_ws': nrm(ks[16], (DEPTH, GMLP_GROUPS, GMLP_CHUNK, GMLP_CHUNK), GMLP_CHUNK ** -0.5),
        'gmlp_b': 1.0 + nrm(ks[17], (DEPTH, GMLP_GROUPS, GMLP_CHUNK), 0.1),
        'w_br': nrm(ks[18], (DEPTH, N_BRANCH, MLSTM_WIDTH, D_MODEL), MLSTM_WIDTH ** -0.5),
        'w_out': nrm(ks[19], (DEPTH, D_MODEL, D_MODEL), D_MODEL ** -0.5),
        'w_up': nrm(ks[20], (DEPTH, D_MODEL, 2 * D_FF), D_MODEL ** -0.5),
        'conv_w': nrm(ks[21], (DEPTH, CONV_K, CONV_K, D_FF), 1.0 / CONV_K),
        'conv_b': nrm(ks[22], (DEPTH, D_FF), 0.02),
        'w_down': nrm(ks[23], (DEPTH, D_FF, D_MODEL), D_FF ** -0.5),
    }


def reference(x_prompt, x_sample, state_C, state_n, state_m, c, c_ctx, w_mod, b_mod, norm_w,
              w_in, b_in, mlstm_gate_b, pool_w, pool_scale, gmlp_norm_w, gmlp_ws, gmlp_b,
              w_br, w_out, w_up, conv_w, conv_b, w_down):
    B_p, T_p, _ = x_prompt.shape
    T_s = x_sample.shape[1]
    rows = T_s // GRID_W
    H, DH = MLSTM_HEADS, MLSTM_HEAD_DIM
    c_prompt = jnp.broadcast_to(c_ctx, (B_p, D_MODEL))
    C0 = jnp.zeros((B_p, N_DIR, H, DH, DH), jnp.float32)
    n0 = jnp.zeros((B_p, N_DIR, H, DH), jnp.float32)
    m0 = jnp.zeros((B_p, N_DIR, H), jnp.float32)
    xp, xs = x_prompt, x_sample
    Cs, ns, ms = [], [], []
    for l in range(DEPTH):
        p = dict(w_mod=w_mod[l], b_mod=b_mod[l], norm_w=norm_w[l], w_in=w_in[l], b_in=b_in[l],
                 mlstm_gate_b=mlstm_gate_b[l], pool_w=pool_w[l], pool_scale=pool_scale[l],
                 gmlp_norm_w=gmlp_norm_w[l], gmlp_ws=gmlp_ws[l], gmlp_b=gmlp_b[l],
                 w_br=w_br[l], w_out=w_out[l], w_up=w_up[l], conv_w=conv_w[l],
                 conv_b=conv_b[l], w_down=w_down[l])
        xp, (Cl, nl, ml) = _trunk_layer(xp, c_prompt, C0, n0, m0, 1, T_p, p)
        Cs.append(Cl)
        ns.append(nl)
        ms.append(ml)
        xs, _ = _trunk_layer(xs, c, state_C[:, l], state_n[:, l], state_m[:, l], rows, GRID_W, p)
    new_C = jnp.stack(Cs, axis=1)
    new_n = jnp.stack(ns, axis=1)
    new_m = jnp.stack(ms, axis=1)
    return (xp, xs, new_C, new_n, new_m)
```

```python
import functools

import jax
import jax.numpy as jnp
from jax import lax
from jax.experimental import pallas as pl
from jax.experimental.pallas import tpu as pltpu

F32 = jnp.float32
BF16 = jnp.bfloat16

D_MODEL = 1024
N_HEADS = 4
HEAD_DIM = D_MODEL // N_HEADS
CHUNK = 128
POOL_WINDOWS = (2, 4, 8, 16)
N_GROUPS = 4
GROUP_DIM = D_MODEL // N_GROUPS
D_FF = 2816
GRID_W = 64
N_MOD = 6
RMS_EPS = 1e-6
LANES = 128
SUBLANES = 8
VMEM_LIMIT = 56 * 1024 * 1024

COL_Q, COL_K, COL_V, COL_O, COL_P, COL_U, COL_VG, COL_M = 0, 1, 2, 3, 4, 5, 6, 7
MAIN_WIDTH = 10 * D_MODEL


def _rms(x, w):
    return x * lax.rsqrt(jnp.mean(x * x, axis=-1, keepdims=True) + RMS_EPS) * w


def _log_sigmoid(x):
    return jnp.minimum(x, 0.0) - jnp.log1p(jnp.exp(-jnp.abs(x)))


def _params(sem):
    return pltpu.CompilerParams(dimension_semantics=sem, vmem_limit_bytes=VMEM_LIMIT)


def _mod_kernel(c_ref, w_ref, b_ref, o_ref):
    c = c_ref[...]
    s = (c * jax.nn.sigmoid(c)).astype(BF16)
    o_ref[...] = jnp.dot(s, w_ref[...].astype(BF16), preferred_element_type=F32) + b_ref[...]


def _modulation(cvec, w_mod, b_mod):
    depth = w_mod.shape[0]
    n = N_MOD * D_MODEL
    tn = 1536
    out = pl.pallas_call(
        _mod_kernel,
        grid=(depth, n // tn),
        in_specs=[
            pl.BlockSpec((SUBLANES, D_MODEL), lambda l, j: (0, 0)),
            pl.BlockSpec((None, D_MODEL, tn), lambda l, j: (l, 0, j)),
            pl.BlockSpec((None, 1, tn), lambda l, j: (l, 0, j)),
        ],
        out_specs=pl.BlockSpec((None, SUBLANES, tn), lambda l, j: (l, 0, j)),
        out_shape=jax.ShapeDtypeStruct((depth, SUBLANES, n), F32),
        compiler_params=_params(("arbitrary", "arbitrary")),
        name="modulation",
    )(cvec, w_mod, b_mod.reshape(depth, 1, n))
    return out.reshape(depth, SUBLANES, N_MOD, D_MODEL)


def _modnorm(x, mod_ref, nw_ref, k_shift, k_scale):
    h = _rms(x, nw_ref[...])
    return h * (1.0 + mod_ref[0, k_scale:k_scale + 1, :]) + mod_ref[0, k_shift:k_shift + 1, :]


def _inproj_kernel(x_ref, mod_ref, nw_ref, w_ref, b_ref, wg_ref, bga_ref, bgb_ref,
                   wgt_ref, bgta_ref, bgtb_ref, z_ref, gc_ref, gt_ref, h_scr):
    @pl.when(pl.program_id(1) == 0)
    def _():
        hb = _modnorm(x_ref[...], mod_ref, nw_ref, 0, 1).astype(BF16)
        h_scr[...] = hb
        gc_ref[...] = (jnp.dot(hb, wg_ref[...], preferred_element_type=F32)
                       + bga_ref[...] + bgb_ref[...])
        gt_ref[...] = (lax.dot_general(wgt_ref[...], hb, (((1,), (1,)), ((), ())),
                                       preferred_element_type=F32)
                       + bgta_ref[...] + bgtb_ref[...])

    z = jnp.dot(h_scr[...], w_ref[...], preferred_element_type=F32) + b_ref[...]
    z_ref[...] = z.astype(z_ref.dtype)


def _ffn_up_kernel(x_ref, mod_ref, nw_ref, w_ref, o_ref, h_scr):
    @pl.when(pl.program_id(1) == 0)
    def _():
        h_scr[...] = _modnorm(x_ref[...], mod_ref, nw_ref, 3, 4).astype(BF16)

    o_ref[...] = jnp.dot(h_scr[...], w_ref[...], preferred_element_type=F32).astype(o_ref.dtype)


def _cum_sublane(x, op, reverse):
    rows = lax.broadcasted_iota(jnp.int32, x.shape, 0)
    fill = 0.0 if op is jnp.add else -jnp.inf
    k = 1
    while k < CHUNK:
        if reverse:
            shifted = jnp.where(rows < CHUNK - k, pltpu.roll(x, CHUNK - k, 0), fill)
        else:
            shifted = jnp.where(rows >= k, pltpu.roll(x, k, 0), fill)
        x = op(x, shifted)
        k *= 2
    return x


def _cum_lane(x, op, reverse):
    cols = lax.broadcasted_iota(jnp.int32, x.shape, 1)
    fill = 0.0 if op is jnp.add else -jnp.inf
    k = 1
    while k < CHUNK:
        if reverse:
            shifted = jnp.where(cols < CHUNK - k, pltpu.roll(x, CHUNK - k, 1), fill)
        else:
            shifted = jnp.where(cols >= k, pltpu.roll(x, k, 1), fill)
        x = op(x, shifted)
        k *= 2
    return x


def _mlstm_kernel(*refs, seq_len, has_init, emit_state):
    q_ref, k_ref, v_ref, o_ref, gc_ref, gt_ref = refs[:6]
    pos = 6
    if has_init:
        c0_ref, n0_ref, m0_ref = refs[pos:pos + 3]
        pos += 3
    ya_ref = refs[pos]
    pos += 1
    if emit_state:
        cout_ref, nout_ref, mout_ref = refs[pos:pos + 3]
        pos += 3
    ct_scr, n_scr, hacc = refs[pos:pos + 3]

    n_chunks = seq_len // CHUNK
    scale = HEAD_DIM ** -0.5
    row_id = lax.broadcasted_iota(jnp.int32, (CHUNK, CHUNK), 0)
    col_id = lax.broadcasted_iota(jnp.int32, (CHUNK, CHUNK), 1)

    for dr in range(2):
        rev = dr == 1
        if has_init:
            ct_scr[dr] = c0_ref[dr].T
            n_scr[dr] = n0_ref[dr]
            m_init = m0_ref[dr]
        else:
            ct_scr[dr] = jnp.zeros((HEAD_DIM, HEAD_DIM), F32)
            n_scr[dr] = jnp.zeros((1, HEAD_DIM), F32)
            m_init = jnp.zeros((1, 1), F32)
        tri = (col_id >= row_id) if rev else (col_id <= row_id)
        edge = 0 if rev else CHUNK - 1

        def chunk_step(c, m, dr=dr, rev=rev, tri=tri, edge=edge):
            ci = (n_chunks - 1 - c) if rev else c
            r0 = pl.multiple_of(ci * CHUNK, CHUNK)
            q = q_ref[pl.ds(r0, CHUNK), :]
            k = k_ref[pl.ds(r0, CHUNK), :]
            v = v_ref[pl.ds(r0, CHUNK), :]
            g_tok = gc_ref[pl.ds(r0, CHUNK), :]
            b_tok = _cum_sublane(_log_sigmoid(pltpu.roll(g_tok, LANES - 2, 1)), jnp.add, rev)
            a_tok = g_tok - b_tok
            cm_tok = _cum_sublane(a_tok, jnp.maximum, rev)
            a_col = a_tok[:, dr:dr + 1]
            b_col = b_tok[:, dr:dr + 1]
            cm_col = cm_tok[:, dr:dr + 1]
            g_lane = gt_ref[:, pl.ds(r0, CHUNK)]
            b_lane = _cum_lane(_log_sigmoid(pltpu.roll(g_lane, SUBLANES - 2, 0)), jnp.add, rev)
            a_row = (g_lane - b_lane)[dr:dr + 1, :]

            big_m = jnp.maximum(m, cm_col)
            w_intra = jnp.exp(jnp.where(tri, a_row - big_m, -jnp.inf))
            w_inter = jnp.exp(m - big_m)
            floor = jnp.exp(-b_col - big_m)

            kf = k.astype(F32) * scale
            ks = kf.astype(BF16)
            s = lax.dot_general(q, ks, (((1,), (1,)), ((), ())), preferred_element_type=F32) * w_intra
            ct = ct_scr[dr]
            n_row = n_scr[dr]
            qf = q.astype(F32)
            num = (jnp.dot(s.astype(BF16), v, preferred_element_type=F32)
                   + w_inter * jnp.dot(q, ct.astype(BF16), preferred_element_type=F32))
            den = (jnp.sum(s, axis=1, keepdims=True)
                   + w_inter * jnp.sum(qf * n_row, axis=1, keepdims=True))
            h = num / jnp.maximum(jnp.abs(den), floor)
            if rev:
                hacc[pl.ds(r0, CHUNK), :] += h
            else:
                hacc[pl.ds(r0, CHUNK), :] = h

            top = jnp.maximum(m, cm_col[edge:edge + 1, :])
            w_k = jnp.exp(a_col - top)
            decay = jnp.exp(m - top)
            wv = (w_k * v.astype(F32)).astype(BF16)
            ct_scr[dr] = decay * ct + jnp.dot(kf.T.astype(BF16), wv, preferred_element_type=F32)
            n_scr[dr] = decay * n_row + jnp.sum(w_k * kf, axis=0, keepdims=True)
            return b_col[edge:edge + 1, :] + top

        m_fin = lax.fori_loop(0, n_chunks, chunk_step, m_init)
        if emit_state:
            cout_ref[dr] = ct_scr[dr].T
            nout_ref[dr] = n_scr[dr]
            mout_ref[dr] = m_fin

    ya_ref[...] = (jax.nn.sigmoid(o_ref[...].astype(F32)) * hacc[...]).astype(ya_ref.dtype)


def _mlstm(z, gc, gt, tok0, n_batch, seq_len, init=None, layer=0):
    blk0 = tok0 // seq_len
    has_init = init is not None
    emit_state = not has_init

    def zspec(col):
        return pl.BlockSpec((seq_len, HEAD_DIM),
                            lambda b, h: (blk0 + b, col * N_HEADS + h))

    in_specs = [zspec(COL_Q), zspec(COL_K), zspec(COL_V), zspec(COL_O),
                pl.BlockSpec((seq_len, LANES), lambda b, h: (blk0 + b, h)),
                pl.BlockSpec((SUBLANES, seq_len), lambda b, h: (h, blk0 + b))]
    args = [z, z, z, z, gc, gt]
    if has_init:
        sc, sn, sm = init
        nb, depth = sc.shape[:2]
        in_specs += [
            pl.BlockSpec((None, None, 2, None, HEAD_DIM, HEAD_DIM),
                         lambda b, h: (b, layer, 0, h, 0, 0)),
            pl.BlockSpec((None, None, 2, None, 1, HEAD_DIM), lambda b, h: (b, layer, 0, h, 0, 0)),
            pl.BlockSpec((None, None, 2, None, 1, 1), lambda b, h: (b, layer, 0, h, 0, 0)),
        ]
        args += [sc, sn.reshape(nb, depth, 2, N_HEADS, 1, HEAD_DIM),
                 sm.reshape(nb, depth, 2, N_HEADS, 1, 1)]
    out_specs = [pl.BlockSpec((seq_len, HEAD_DIM), lambda b, h: (b, h))]
    out_shape = [jax.ShapeDtypeStruct((n_batch * seq_len, D_MODEL), BF16)]
    if emit_state:
        out_specs += [
            pl.BlockSpec((None, 2, None, HEAD_DIM, HEAD_DIM), lambda b, h: (b, 0, h, 0, 0)),
            pl.BlockSpec((None, 2, None, 1, HEAD_DIM), lambda b, h: (b, 0, h, 0, 0)),
            pl.BlockSpec((None, 2, None, 1, 1), lambda b, h: (b, 0, h, 0, 0)),
        ]
        out_shape += [
            jax.ShapeDtypeStruct((n_batch, 2, N_HEADS, HEAD_DIM, HEAD_DIM), F32),
            jax.ShapeDtypeStruct((n_batch, 2, N_HEADS, 1, HEAD_DIM), F32),
            jax.ShapeDtypeStruct((n_batch, 2, N_HEADS, 1, 1), F32),
        ]
    outs = pl.pallas_call(
        functools.partial(_mlstm_kernel, seq_len=seq_len, has_init=has_init, emit_state=emit_state),
        grid=(n_batch, N_HEADS),
        in_specs=in_specs,
        out_specs=out_specs,
        out_shape=out_shape,
        scratch_shapes=[pltpu.VMEM((2, HEAD_DIM, HEAD_DIM), F32),
                        pltpu.VMEM((2, 1, HEAD_DIM), F32),
                        pltpu.VMEM((seq_len, HEAD_DIM), F32)],
        compiler_params=_params(("arbitrary", "arbitrary")),
        name="mlstm_ctx" if emit_state else "mlstm_lat",
    )(*args)
    if emit_state:
        ya, c_new, n_new, m_new = outs
        return ya, (c_new, n_new.reshape(n_batch, 2, N_HEADS, HEAD_DIM),
                    m_new.reshape(n_batch, 2, N_HEADS))
    return outs[0], None


def _mixers_kernel(pp_ref, pc_ref, pn_ref, u_ref, vg_ref, pw_ref, ps_ref, gnw_ref, ws_ref, gb_ref,
                   yb_ref, yc_ref, *, n_ctx_tok, ctx_len, lat_len):
    tok0 = pl.program_id(0) * CHUNK
    is_ctx = tok0 < n_ctx_tok
    seq_len = jnp.where(is_ctx, ctx_len, lat_len)
    base = jnp.where(is_ctx, 0, n_ctx_tok)
    seq_start = base + ((tok0 - base) // seq_len) * seq_len
    seq_end = seq_start + seq_len

    t_abs = tok0 + lax.broadcasted_iota(jnp.int32, (CHUNK, CHUNK), 0)
    u_rel = lax.broadcasted_iota(jnp.int32, (CHUNK, CHUNK), 1)
    t_col = tok0 + lax.broadcasted_iota(jnp.int32, (CHUNK, 1), 0)
    for g, win in enumerate(POOL_WINDOWS):
        half = win // 2
        sl = slice(g * GROUP_DIM, (g + 1) * GROUP_DIM)
        acc = jnp.zeros((CHUNK, GROUP_DIM), F32)
        for ref, off in ((pp_ref, -CHUNK), (pc_ref, 0), (pn_ref, CHUNK)):
            u_abs = tok0 + off + u_rel
            band = ((u_abs >= t_abs - half) & (u_abs < t_abs + half)
                    & (u_abs >= seq_start) & (u_abs < seq_end))
            acc += jnp.dot(band.astype(F32).astype(BF16), ref[:, sl], preferred_element_type=F32)
        count = (jnp.minimum(t_col + half, seq_end) - jnp.maximum(t_col - half, seq_start)).astype(F32)
        y = (acc / count - pc_ref[:, sl].astype(F32)).astype(BF16)
        yb = jnp.dot(y, pw_ref[g], preferred_element_type=F32) * ps_ref[:, sl]
        yb_ref[:, sl] = yb.astype(yb_ref.dtype)

    u = jax.nn.gelu(u_ref[...].astype(F32))
    vg = _rms(jax.nn.gelu(vg_ref[...].astype(F32)), gnw_ref[...]).astype(BF16)
    for g in range(N_GROUPS):
        sl = slice(g * GROUP_DIM, (g + 1) * GROUP_DIM)
        s = jnp.dot(ws_ref[g], vg[:, sl], preferred_element_type=F32) + gb_ref[:, g:g + 1]
        yc_ref[:, sl] = (u[:, sl] * s).astype(yc_ref.dtype)


def _merge_kernel(ya_ref, yb_ref, yc_ref, m0_ref, m1_ref, m2_ref, x_ref, mod_ref, nw_ref,
                  wbr_ref, wout_ref, o_ref):
    mixed = None
    for n, (y_ref, m_ref) in enumerate(((ya_ref, m0_ref), (yb_ref, m1_ref), (yc_ref, m2_ref))):
        br = jnp.dot(y_ref[...], wbr_ref[n], preferred_element_type=F32)
        term = jax.nn.sigmoid(m_ref[...].astype(F32)) * br
        mixed = term if mixed is None else mixed + term
    o = jnp.dot(mixed.astype(BF16), wout_ref[...], preferred_element_type=F32)
    o_ref[...] = x_ref[...] + mod_ref[0, 2:3, :] * _rms(o, nw_ref[...])


FFN_TILE = 256
FFN_SLAB = 256


def _ffn_down_kernel(ap_ref, ac_ref, an_ref, g_ref, cw_ref, cb_ref, wd_ref, x_ref, mod_ref, nw_ref,
                     o_ref, u_scr, *, n_ctx_tok, ctx_len, lat_len):
    i = pl.program_id(0)
    tok0 = i * FFN_TILE
    is_ctx = tok0 < n_ctx_tok
    tiles_per_lat = lat_len // FFN_TILE
    lat_tile = jnp.maximum(tok0 - n_ctx_tok, 0) // FFN_TILE % tiles_per_lat
    has_up = jnp.logical_and(jnp.logical_not(is_ctx), lat_tile != 0)
    has_dn = jnp.logical_and(jnp.logical_not(is_ctx), lat_tile != tiles_per_lat - 1)
    width = jnp.where(is_ctx, ctx_len, GRID_W)
    ext_rows = FFN_TILE + 2 * GRID_W
    r = lax.broadcasted_iota(jnp.int32, (ext_rows, 1), 0)
    col = (r - GRID_W + width) % width
    ok_left = col != 0
    ok_right = col != width - 1
    vert = jnp.where(is_ctx, 0.0, 1.0)

    for c in range(D_FF // FFN_SLAB):
        sl = slice(c * FFN_SLAB, (c + 1) * FFN_SLAB)
        top = jnp.where(has_up, ap_ref[:, sl].astype(F32), 0.0)
        bot = jnp.where(has_dn, an_ref[:, sl].astype(F32), 0.0)
        ext = jnp.concatenate([top, ac_ref[:, sl].astype(F32), bot], axis=0)
        left = jnp.where(ok_left, pltpu.roll(ext, 1, 0), 0.0)
        right = jnp.where(ok_right, pltpu.roll(ext, ext_rows - 1, 0), 0.0)
        acc = jnp.zeros((FFN_TILE, FFN_SLAB), F32) + cb_ref[:, sl]
        for dy in range(3):
            rows = slice(dy * GRID_W, dy * GRID_W + FFN_TILE)
            for dx, src in enumerate((left, ext, right)):
                w = cw_ref[dy * 3 + dx:dy * 3 + dx + 1, sl]
                if dy != 1:
                    w = w * vert
                acc = acc + w * src[rows, :]
        u_scr[:, sl] = (jax.nn.gelu(acc) * g_ref[:, sl].astype(F32)).astype(BF16)

    f = jnp.dot(u_scr[...], wd_ref[...], preferred_element_type=F32)
    o_ref[...] = x_ref[...] + mod_ref[0, 5:6, :] * _rms(f, nw_ref[...])


def _gate_layout(w_gate):
    def src(h, j):
        d, kind = j % 2, j // 2
        return d * 2 * N_HEADS + kind * N_HEADS + h

    def build(stride):
        idx = [src(p // stride, p % stride) if p % stride < 4 else 0 for p in range(N_HEADS * stride)]
        keep = [1.0 if p % stride < 4 else 0.0 for p in range(N_HEADS * stride)]
        return w_gate[..., jnp.array(idx)] * jnp.array(keep, w_gate.dtype)

    return build(LANES), build(SUBLANES)


def kernel(x_prompt, x_sample, state_C, state_n, state_m, c, c_ctx, w_mod, b_mod, norm_w, w_in, b_in,
           mlstm_gate_b, pool_w, pool_scale, gmlp_norm_w, gmlp_ws, gmlp_b, w_br, w_out, w_up, conv_w,
           conv_b, w_down):
    n_ctx, ctx_len, d = x_prompt.shape
    n_lat, lat_len, _ = x_sample.shape
    depth = w_in.shape[0]
    assert d == D_MODEL and ctx_len == FFN_TILE and lat_len % FFN_TILE == 0
    n_ctx_tok = n_ctx * ctx_len
    n_tok = n_ctx_tok + n_lat * lat_len
    n_gate = 2 * 2 * N_HEADS
    gate0 = 4 * D_MODEL

    x = jnp.concatenate([x_prompt.reshape(n_ctx_tok, d), x_sample.reshape(n_lat * lat_len, d)], axis=0)

    cvec = jnp.zeros((SUBLANES, d), F32).at[0].set(c_ctx).at[1:1 + n_lat].set(c)
    mod = _modulation(cvec, w_mod, b_mod)

    def mod_spec(tm):
        def row(i):
            t0 = i * tm
            return jnp.where(t0 < n_ctx_tok, 0, 1 + jnp.maximum(t0 - n_ctx_tok, 0) // lat_len)
        return row

    tm_proj = 1024
    tn_in = 1024
    row_in = mod_spec(tm_proj)
    row_256 = mod_spec(FFN_TILE)
    vec = lambda a: a.reshape(1, -1)
    c_states, n_states, m_states = [], [], []

    for l in range(depth):
        w_main = jnp.concatenate([w_in[l, :, :gate0], w_in[l, :, gate0 + n_gate:]], axis=1).astype(BF16)
        b_main = jnp.concatenate([b_in[l, :gate0], b_in[l, gate0 + n_gate:]]).reshape(1, MAIN_WIDTH)
        wg_tok, wg_lane = _gate_layout(w_in[l, :, gate0:gate0 + n_gate])
        bga_tok, bga_lane = _gate_layout(b_in[l, gate0:gate0 + n_gate])
        bgb_tok, bgb_lane = _gate_layout(mlstm_gate_b[l].reshape(n_gate))
        mod_l = mod[l]
        nw = norm_w[l]

        z, gc, gt = pl.pallas_call(
            _inproj_kernel,
            grid=(n_tok // tm_proj, MAIN_WIDTH // tn_in),
            in_specs=[
                pl.BlockSpec((tm_proj, d), lambda i, j: (i, 0)),
                pl.BlockSpec((1, N_MOD, d), lambda i, j: (row_in(i), 0, 0)),
                pl.BlockSpec((1, d), lambda i, j: (0, 0)),
                pl.BlockSpec((d, tn_in), lambda i, j: (0, j)),
                pl.BlockSpec((1, tn_in), lambda i, j: (0, j)),
                pl.BlockSpec((d, N_HEADS * LANES), lambda i, j: (0, 0)),
                pl.BlockSpec((1, N_HEADS * LANES), lambda i, j: (0, 0)),
                pl.BlockSpec((1, N_HEADS * LANES), lambda i, j: (0, 0)),
                pl.BlockSpec((N_HEADS * SUBLANES, d), lambda i, j: (0, 0)),
                pl.BlockSpec((N_HEADS * SUBLANES, 1), lambda i, j: (0, 0)),
                pl.BlockSpec((N_HEADS * SUBLANES, 1), lambda i, j: (0, 0)),
            ],
            out_specs=[
                pl.BlockSpec((tm_proj, tn_in), lambda i, j: (i, j)),
                pl.BlockSpec((tm_proj, N_HEADS * LANES), lambda i, j: (i, 0)),
                pl.BlockSpec((N_HEADS * SUBLANES, tm_proj), lambda i, j: (0, i)),
            ],
            out_shape=[
                jax.ShapeDtypeStruct((n_tok, MAIN_WIDTH), BF16),
                jax.ShapeDtypeStruct((n_tok, N_HEADS * LANES), F32),
                jax.ShapeDtypeStruct((N_HEADS * SUBLANES, n_tok), F32),
            ],
            scratch_shapes=[pltpu.VMEM((tm_proj, d), BF16)],
            compiler_params=_params(("arbitrary", "arbitrary")),
            name="inproj",
        )(x, mod_l, vec(nw[0]), w_main, b_main, wg_tok.astype(BF16), vec(bga_tok), vec(bgb_tok),
          wg_lane.T.astype(BF16), bga_lane.reshape(-1, 1), bgb_lane.reshape(-1, 1))

        ya_ctx, (c_l, n_l, m_l) = _mlstm(z, gc, gt, 0, n_ctx, ctx_len)
        ya_lat, _ = _mlstm(z, gc, gt, n_ctx_tok, n_lat, lat_len,
                           init=(state_C, state_n, state_m), layer=l)
        ya = jnp.concatenate([ya_ctx, ya_lat], axis=0)
        c_states.append(c_l)
        n_states.append(n_l)
        m_states.append(m_l)

        n_chunks = n_tok // CHUNK
        zcol = lambda col: pl.BlockSpec((CHUNK, d), lambda i: (i, col))
        full = lambda shape: pl.BlockSpec(shape, lambda i: (0,) * len(shape))
        yb, yc = pl.pallas_call(
            functools.partial(_mixers_kernel, n_ctx_tok=n_ctx_tok, ctx_len=ctx_len, lat_len=lat_len),
            grid=(n_chunks,),
            in_specs=[
                pl.BlockSpec((CHUNK, d), lambda i: (jnp.maximum(i - 1, 0), COL_P)),
                zcol(COL_P),
                pl.BlockSpec((CHUNK, d), lambda i: (jnp.minimum(i + 1, n_chunks - 1), COL_P)),
                zcol(COL_U), zcol(COL_VG),
                full((N_GROUPS, GROUP_DIM, GROUP_DIM)), full((1, d)), full((1, d)),
                full((N_GROUPS, CHUNK, CHUNK)), full((CHUNK, N_GROUPS)),
            ],
            out_specs=[pl.BlockSpec((CHUNK, d), lambda i: (i, 0))] * 2,
            out_shape=[jax.ShapeDtypeStruct((n_tok, d), BF16)] * 2,
            compiler_params=_params(("arbitrary",)),
            name="mixers",
        )(z, z, z, z, z, pool_w[l].astype(BF16), vec(pool_scale[l]), vec(gmlp_norm_w[l]),
          gmlp_ws[l].astype(BF16), gmlp_b[l].T)

        tm = FFN_TILE
        tok = lambda: pl.BlockSpec((tm, d), lambda i: (i, 0))
        x = pl.pallas_call(
            _merge_kernel,
            grid=(n_tok // tm,),
            in_specs=[
                tok(), tok(), tok(),
                pl.BlockSpec((tm, d), lambda i: (i, COL_M)),
                pl.BlockSpec((tm, d), lambda i: (i, COL_M + 1)),
                pl.BlockSpec((tm, d), lambda i: (i, COL_M + 2)),
                tok(),
                pl.BlockSpec((1, N_MOD, d), lambda i: (row_256(i), 0, 0)),
                full((1, d)), full((3, d, d)), full((d, d)),
            ],
            out_specs=tok(),
            out_shape=jax.ShapeDtypeStruct((n_tok, d), F32),
            compiler_params=_params(("arbitrary",)),
            name="merge",
        )(ya, yb, yc, z, z, z, x, mod_l, vec(nw[1]), w_br[l].astype(BF16), w_out[l].astype(BF16))

        up = pl.pallas_call(
            _ffn_up_kernel,
            grid=(n_tok // tm_proj, 2),
            in_specs=[
                pl.BlockSpec((tm_proj, d), lambda i, j: (i, 0)),
                pl.BlockSpec((1, N_MOD, d), lambda i, j: (row_in(i), 0, 0)),
                pl.BlockSpec((1, d), lambda i, j: (0, 0)),
                pl.BlockSpec((d, D_FF), lambda i, j: (0, j)),
            ],
            out_specs=pl.BlockSpec((tm_proj, D_FF), lambda i, j: (i, j)),
            out_shape=jax.ShapeDtypeStruct((n_tok, 2 * D_FF), BF16),
            scratch_shapes=[pltpu.VMEM((tm_proj, d), BF16)],
            compiler_params=_params(("arbitrary", "arbitrary")),
            name="ffn_up",
        )(x, mod_l, vec(nw[2]), w_up[l].astype(BF16))

        rows_per_tile = tm // GRID_W
        n_rows = n_tok // GRID_W
        x = pl.pallas_call(
            functools.partial(_ffn_down_kernel, n_ctx_tok=n_ctx_tok, ctx_len=ctx_len, lat_len=lat_len),
            grid=(n_tok // tm,),
            in_specs=[
                pl.BlockSpec((GRID_W, D_FF), lambda i: (jnp.maximum(i * rows_per_tile - 1, 0), 0)),
                pl.BlockSpec((tm, D_FF), lambda i: (i, 0)),
                pl.BlockSpec((GRID_W, D_FF),
                             lambda i: (jnp.minimum((i + 1) * rows_per_tile, n_rows - 1), 0)),
                pl.BlockSpec((tm, D_FF), lambda i: (i, 1)),
                full((9, D_FF)), full((1, D_FF)), full((D_FF, d)),
                tok(),
                pl.BlockSpec((1, N_MOD, d), lambda i: (row_256(i), 0, 0)),
                full((1, d)),
            ],
            out_specs=tok(),
            out_shape=jax.ShapeDtypeStruct((n_tok, d), F32),
            scratch_shapes=[pltpu.VMEM((tm, D_FF), BF16)],
            compiler_params=_params(("arbitrary",)),
            name="ffn_down",
        )(up, up, up, up, conv_w[l].reshape(9, D_FF), vec(conv_b[l]), w_down[l].astype(BF16),
          x, mod_l, vec(nw[3]))

    y_prompt = x[:n_ctx_tok].reshape(n_ctx, ctx_len, d)
    y_sample = x[n_ctx_tok:].reshape(n_lat, lat_len, d)
    return (y_prompt, y_sample, jnp.stack(c_states, axis=1), jnp.stack(n_states, axis=1),
            jnp.stack(m_states, axis=1))
```

```python
import functools

import jax
import jax.numpy as jnp
from jax import lax
from jax.experimental import pallas as pl
from jax.experimental.pallas import tpu as pltpu

F32 = jnp.float32
BF16 = jnp.bfloat16

D_MODEL = 1024
N_HEADS = 4
HEAD_DIM = D_MODEL // N_HEADS
CHUNK = 128
POOL_WINDOWS = (2, 4, 8, 16)
N_GROUPS = 4
GROUP_DIM = D_MODEL // N_GROUPS
D_FF = 2816
GRID_W = 64
N_MOD = 6
RMS_EPS = 1e-6
LANES = 128
SUBLANES = 8
VMEM_LIMIT = 56 * 1024 * 1024

COL_Q, COL_K, COL_V, COL_O, COL_P, COL_U, COL_VG, COL_M = 0, 1, 2, 3, 4, 5, 6, 7
MAIN_WIDTH = 10 * D_MODEL

NT_DIMS = (((1,), (1,)), ((), ()))


def _rms(x, w):
    return x * lax.rsqrt(jnp.mean(x * x, axis=-1, keepdims=True) + RMS_EPS) * w


def _log_sigmoid(x):
    return jnp.minimum(x, 0.0) - jnp.log1p(jnp.exp(-jnp.abs(x)))


def _params(sem):
    return pltpu.CompilerParams(dimension_semantics=sem, vmem_limit_bytes=VMEM_LIMIT)


def _mod_kernel(c_ref, w_ref, b_ref, o_ref):
    c = c_ref[...]
    s = (c * jax.nn.sigmoid(c)).astype(BF16)
    o_ref[...] = jnp.dot(s, w_ref[...].astype(BF16), preferred_element_type=F32) + b_ref[...]


def _modulation(cvec, w_mod, b_mod):
    depth = w_mod.shape[0]
    n = N_MOD * D_MODEL
    tn = 1536
    out = pl.pallas_call(
        _mod_kernel,
        grid=(depth, n // tn),
        in_specs=[
            pl.BlockSpec((SUBLANES, D_MODEL), lambda l, j: (0, 0)),
            pl.BlockSpec((None, D_MODEL, tn), lambda l, j: (l, 0, j)),
            pl.BlockSpec((None, 1, tn), lambda l, j: (l, 0, j)),
        ],
        out_specs=pl.BlockSpec((None, SUBLANES, tn), lambda l, j: (l, 0, j)),
        out_shape=jax.ShapeDtypeStruct((depth, SUBLANES, n), F32),
        compiler_params=_params(("arbitrary", "arbitrary")),
        name="modulation",
    )(cvec, w_mod, b_mod.reshape(depth, 1, n))
    return out.reshape(depth, SUBLANES, N_MOD, D_MODEL)


def _modnorm(x, mod_ref, nw_ref, k_shift, k_scale):
    h = _rms(x, nw_ref[...])
    return h * (1.0 + mod_ref[0, k_scale:k_scale + 1, :]) + mod_ref[0, k_shift:k_shift + 1, :]


def _inproj_kernel(x_ref, mod_ref, nw_ref, w_ref, b_ref, wgt_ref, bgta_ref, bgtb_ref,
                   z_ref, gt_ref, h_scr):
    @pl.when(pl.program_id(1) == 0)
    def _():
        hb = _modnorm(x_ref[...], mod_ref, nw_ref, 0, 1).astype(BF16)
        h_scr[...] = hb
        gt_ref[...] = (lax.dot_general(wgt_ref[...], hb, NT_DIMS, preferred_element_type=F32)
                       + bgta_ref[...] + bgtb_ref[...])

    z = jnp.dot(h_scr[...], w_ref[...], preferred_element_type=F32) + b_ref[...]
    z_ref[...] = z.astype(z_ref.dtype)


def _ffn_up_kernel(x_ref, mod_ref, nw_ref, w_ref, o_ref, h_scr):
    @pl.when(pl.program_id(1) == 0)
    def _():
        h_scr[...] = _modnorm(x_ref[...], mod_ref, nw_ref, 3, 4).astype(BF16)

    o_ref[...] = jnp.dot(h_scr[...], w_ref[...], preferred_element_type=F32).astype(o_ref.dtype)


def _chunk_scans(x, op, fill, lane):
    width = x.shape[1]
    fwd = bwd = x
    k = 1
    while k < CHUNK:
        fwd = op(fwd, jnp.where(lane >= k, pltpu.roll(fwd, k, 1), fill))
        bwd = op(bwd, jnp.where(lane < CHUNK - k, pltpu.roll(bwd, width - k, 1), fill))
        k *= 2
    return fwd, bwd


def _mlstm_kernel(*refs, seq_len, has_init, emit_state):
    q_ref, k_ref, v_ref, o_ref, gt_ref = refs[:5]
    pos = 5
    if has_init:
        c0_ref, n0_ref, m0_ref = refs[pos:pos + 3]
        pos += 3
    ya_ref = refs[pos]
    pos += 1
    if emit_state:
        cout_ref, nout_ref, mout_ref = refs[pos:pos + 3]
        pos += 3
    rows_scr, wk16_scr, dec_scr, vt_scr, vtw_scr, cb_scr, n1_scr, c_scr = refs[pos:pos + 8]

    n_chunks = seq_len // CHUNK
    scale = HEAD_DIM ** -0.5
    row81 = lax.broadcasted_iota(jnp.int32, (SUBLANES, 1), 0)
    is_fwd1 = row81 == 0
    row_n = lax.broadcasted_iota(jnp.int32, (SUBLANES, HEAD_DIM), 0)

    g_all = gt_ref[...]
    fwd_all = lax.broadcasted_iota(jnp.int32, g_all.shape, 0) == 0
    lane_all = lax.broadcasted_iota(jnp.int32, g_all.shape, 1) % CHUNK
    lf_all = _log_sigmoid(pltpu.roll(g_all, SUBLANES - 2, 0))
    pre, suf = _chunk_scans(lf_all, jnp.add, 0.0, lane_all)
    b_all = jnp.where(fwd_all, pre, suf)
    a_all = g_all - b_all
    pre, suf = _chunk_scans(a_all, jnp.maximum, -jnp.inf, lane_all)
    cm_all = jnp.where(fwd_all, pre, suf)
    chunk = lambda x, c: x[:, c * CHUNK:(c + 1) * CHUNK]
    a_c = [chunk(a_all, c) for c in range(n_chunks)]
    b_c = [chunk(b_all, c) for c in range(n_chunks)]
    cm_c = [chunk(cm_all, c) for c in range(n_chunks)]
    amax_c = [jnp.max(a, axis=1, keepdims=True) for a in a_c]
    bend_c = [jnp.sum(chunk(lf_all, c), axis=1, keepdims=True) for c in range(n_chunks)]

    if has_init:
        m_state = jnp.where(is_fwd1, m0_ref[0], jnp.where(row81 == 1, m0_ref[1], 0.0))
        n_init = jnp.where(row_n == 0, n0_ref[0], n0_ref[1])
        c_scr[...] = c0_ref[...]
    else:
        m_state = jnp.zeros((SUBLANES, 1), F32)
        n_init = jnp.zeros((SUBLANES, HEAD_DIM), F32)
        c_scr[...] = jnp.zeros(c_scr.shape, F32)

    m_before, top = [], []
    for j in range(n_chunks):
        jr = n_chunks - 1 - j
        amax = jnp.where(is_fwd1, amax_c[j], amax_c[jr])
        bend = jnp.where(is_fwd1, bend_c[j], bend_c[jr])
        m_before.append(m_state)
        top.append(jnp.maximum(m_state, amax))
        m_state = bend + top[-1]

    for c in range(n_chunks):
        cr = n_chunks - 1 - c
        m_c = jnp.where(is_fwd1, m_before[c], m_before[cr])
        top_c = jnp.where(is_fwd1, top[c], top[cr])
        big_m = jnp.maximum(m_c, cm_c[c])
        w_k = jnp.exp(a_c[c] - top_c)
        rows_scr[c, 0] = a_c[c]
        rows_scr[c, 1] = big_m
        rows_scr[c, 2] = jnp.exp(m_c - big_m)
        rows_scr[c, 3] = jnp.exp(-b_c[c] - big_m)
        rows_scr[c, 4] = w_k
        wk16_scr[c] = jnp.concatenate([w_k, jnp.zeros_like(w_k)], axis=0).astype(BF16)
        dec_scr[c] = jnp.broadcast_to(jnp.exp(m_c - top_c), (SUBLANES, HEAD_DIM))

    def load_k(r0):
        return (k_ref[pl.ds(r0, CHUNK), :].astype(F32) * scale).astype(BF16)

    def pass_a(j, nst):
        c = n_chunks - 1 - j
        r0 = pl.multiple_of(c * CHUNK, CHUNK)
        ks = load_k(r0)
        vt = v_ref[pl.ds(r0, CHUNK), :].astype(F32).T
        w_k = rows_scr[c, 4]
        vt_scr[c] = vt.astype(BF16)
        vtw_scr[c] = (vt * w_k[0:1, :]).astype(BF16)
        c_in = c_scr[1]
        cb_scr[c] = c_in.astype(BF16)
        n1_scr[c] = nst
        dec = dec_scr[c]
        c_scr[1] = dec[1:2, :] * c_in + jnp.dot((vt * w_k[1:2, :]).astype(BF16), ks,
                                                preferred_element_type=F32)
        n_upd = jnp.dot(wk16_scr[c], ks, preferred_element_type=F32)[:SUBLANES]
        return dec * nst + n_upd

    n_rev = lax.fori_loop(0, n_chunks, pass_a, n_init, unroll=2)

    row_id = lax.broadcasted_iota(jnp.int32, (CHUNK, CHUNK), 0)
    col_id = lax.broadcasted_iota(jnp.int32, (CHUNK, CHUNK), 1)
    tri = (row_id <= col_id, row_id >= col_id)

    def pass_b(c, nst):
        r0 = pl.multiple_of(c * CHUNK, CHUNK)
        q = q_ref[pl.ds(r0, CHUNK), :]
        ks = load_k(r0)
        vt = vt_scr[c]
        a_r, bigm_r, winter_r, floor_r = (rows_scr[c, i] for i in range(4))
        st_all = lax.dot_general(ks, q, NT_DIMS, preferred_element_type=F32)
        n_rows = jnp.where(row_n == 0, nst, n1_scr[c])
        n_rows = jnp.concatenate([n_rows, jnp.zeros_like(n_rows)], axis=0).astype(BF16)
        qn = lax.dot_general(n_rows, q, NT_DIMS, preferred_element_type=F32)
        c0 = c_scr[0]
        ht = None
        for d in range(2):
            row = slice(d, d + 1)
            a_st = jnp.broadcast_to(a_r[row, :], (CHUNK, CHUNK)).T
            st = st_all * jnp.exp(jnp.where(tri[d], a_st - bigm_r[row, :], -jnp.inf))
            den = jnp.sum(st, axis=0, keepdims=True) + winter_r[row, :] * qn[row, :]
            inv = 1.0 / jnp.maximum(jnp.abs(den), floor_r[row, :])
            cb = c0.astype(BF16) if d == 0 else cb_scr[c]
            num_t = (jnp.dot(vt, st.astype(BF16), preferred_element_type=F32)
                     + winter_r[row, :] * lax.dot_general(cb, q, NT_DIMS,
                                                          preferred_element_type=F32))
            ht = num_t * inv if ht is None else ht + num_t * inv
        gate = jax.nn.sigmoid(o_ref[pl.ds(r0, CHUNK), :].astype(F32))
        ya_ref[pl.ds(r0, CHUNK), :] = (gate * ht.T).astype(ya_ref.dtype)

        dec = dec_scr[c]
        c_scr[0] = dec[0:1, :] * c0 + jnp.dot(vtw_scr[c], ks, preferred_element_type=F32)
        n_upd = jnp.dot(wk16_scr[c], ks, preferred_element_type=F32)[:SUBLANES]
        return dec * nst + n_upd

    n_fwd = lax.fori_loop(0, n_chunks, pass_b, n_init, unroll=2)

    if emit_state:
        cout_ref[...] = c_scr[...]
        nout_ref[0] = n_fwd[0:1, :]
        nout_ref[1] = n_rev[1:2, :]
        mout_ref[0] = m_state[0:1, :]
        mout_ref[1] = m_state[1:2, :]


def _mlstm(z, gt, tok0, n_batch, seq_len, init=None, layer=0):
    blk0 = tok0 // seq_len
    n_chunks = seq_len // CHUNK
    has_init = init is not None
    emit_state = not has_init

    def zspec(col):
        return pl.BlockSpec((seq_len, HEAD_DIM),
                            lambda b, h: (blk0 + b, col * N_HEADS + h))

    in_specs = [zspec(COL_Q), zspec(COL_K), zspec(COL_V), zspec(COL_O),
                pl.BlockSpec((SUBLANES, seq_len), lambda b, h: (h, blk0 + b))]
    args = [z, z, z, z, gt]
    if has_init:
        sc, sn, sm = init
        nb, depth = sc.shape[:2]
        in_specs += [
            pl.BlockSpec((None, None, 2, None, HEAD_DIM, HEAD_DIM),
                         lambda b, h: (b, layer, 0, h, 0, 0)),
            pl.BlockSpec((None, None, 2, None, 1, HEAD_DIM), lambda b, h: (b, layer, 0, h, 0, 0)),
            pl.BlockSpec((None, None, 2, None, 1, 1), lambda b, h: (b, layer, 0, h, 0, 0)),
        ]
        args += [sc, sn.reshape(nb, depth, 2, N_HEADS, 1, HEAD_DIM),
                 sm.reshape(nb, depth, 2, N_HEADS, 1, 1)]
    out_specs = [pl.BlockSpec((seq_len, HEAD_DIM), lambda b, h: (b, h))]
    out_shape = [jax.ShapeDtypeStruct((n_batch * seq_len, D_MODEL), BF16)]
    if emit_state:
        out_specs += [
            pl.BlockSpec((None, 2, None, HEAD_DIM, HEAD_DIM), lambda b, h: (b, 0, h, 0, 0)),
            pl.BlockSpec((None, 2, None, 1, HEAD_DIM), lambda b, h: (b, 0, h, 0, 0)),
            pl.BlockSpec((None, 2, None, 1, 1), lambda b, h: (b, 0, h, 0, 0)),
        ]
        out_shape += [
            jax.ShapeDtypeStruct((n_batch, 2, N_HEADS, HEAD_DIM, HEAD_DIM), F32),
            jax.ShapeDtypeStruct((n_batch, 2, N_HEADS, 1, HEAD_DIM), F32),
            jax.ShapeDtypeStruct((n_batch, 2, N_HEADS, 1, 1), F32),
        ]
    outs = pl.pallas_call(
        functools.partial(_mlstm_kernel, seq_len=seq_len, has_init=has_init, emit_state=emit_state),
        grid=(n_batch, N_HEADS),
        in_specs=in_specs,
        out_specs=out_specs,
        out_shape=out_shape,
        scratch_shapes=[
            pltpu.VMEM((n_chunks, 5, SUBLANES, CHUNK), F32),
            pltpu.VMEM((n_chunks, 2 * SUBLANES, CHUNK), BF16),
            pltpu.VMEM((n_chunks, SUBLANES, HEAD_DIM), F32),
            pltpu.VMEM((n_chunks, HEAD_DIM, CHUNK), BF16),
            pltpu.VMEM((n_chunks, HEAD_DIM, CHUNK), BF16),
            pltpu.VMEM((n_chunks, HEAD_DIM, HEAD_DIM), BF16),
            pltpu.VMEM((n_chunks, SUBLANES, HEAD_DIM), F32),
            pltpu.VMEM((2, HEAD_DIM, HEAD_DIM), F32),
        ],
        compiler_params=_params(("arbitrary", "arbitrary")),
        name="mlstm_ctx" if emit_state else "mlstm_lat",
    )(*args)
    if emit_state:
        ya, c_new, n_new, m_new = outs
        return ya, (c_new, n_new.reshape(n_batch, 2, N_HEADS, HEAD_DIM),
                    m_new.reshape(n_batch, 2, N_HEADS))
    return outs[0], None


def _mixers_kernel(pp_ref, pc_ref, pn_ref, u_ref, vg_ref, pw_ref, ps_ref, gnw_ref, ws_ref, gb_ref,
                   yb_ref, yc_ref, *, n_ctx_tok, ctx_len, lat_len):
    tok0 = pl.program_id(0) * CHUNK
    is_ctx = tok0 < n_ctx_tok
    seq_len = jnp.where(is_ctx, ctx_len, lat_len)
    base = jnp.where(is_ctx, 0, n_ctx_tok)
    seq_start = base + ((tok0 - base) // seq_len) * seq_len
    seq_end = seq_start + seq_len

    t_abs = tok0 + lax.broadcasted_iota(jnp.int32, (CHUNK, CHUNK), 0)
    u_rel = lax.broadcasted_iota(jnp.int32, (CHUNK, CHUNK), 1)
    t_col = tok0 + lax.broadcasted_iota(jnp.int32, (CHUNK, 1), 0)
    for g, win in enumerate(POOL_WINDOWS):
        half = win // 2
        sl = slice(g * GROUP_DIM, (g + 1) * GROUP_DIM)
        acc = jnp.zeros((CHUNK, GROUP_DIM), F32)
        for ref, off in ((pp_ref, -CHUNK), (pc_ref, 0), (pn_ref, CHUNK)):
            u_abs = tok0 + off + u_rel
            band = ((u_abs >= t_abs - half) & (u_abs < t_abs + half)
                    & (u_abs >= seq_start) & (u_abs < seq_end))
            acc += jnp.dot(band.astype(F32).astype(BF16), ref[:, sl], preferred_element_type=F32)
        count = (jnp.minimum(t_col + half, seq_end) - jnp.maximum(t_col - half, seq_start)).astype(F32)
        y = (acc / count - pc_ref[:, sl].astype(F32)).astype(BF16)
        yb = jnp.dot(y, pw_ref[g], preferred_element_type=F32) * ps_ref[:, sl]
        yb_ref[:, sl] = yb.astype(yb_ref.dtype)

    u = jax.nn.gelu(u_ref[...].astype(F32))
    vg = _rms(jax.nn.gelu(vg_ref[...].astype(F32)), gnw_ref[...]).astype(BF16)
    for g in range(N_GROUPS):
        sl = slice(g * GROUP_DIM, (g + 1) * GROUP_DIM)
        s = jnp.dot(ws_ref[g], vg[:, sl], preferred_element_type=F32) + gb_ref[:, g:g + 1]
        yc_ref[:, sl] = (u[:, sl] * s).astype(yc_ref.dtype)


def _merge_kernel(ya_ref, yb_ref, yc_ref, m0_ref, m1_ref, m2_ref, x_ref, mod_ref, nw_ref,
                  wbr_ref, wout_ref, o_ref):
    mixed = None
    for n, (y_ref, m_ref) in enumerate(((ya_ref, m0_ref), (yb_ref, m1_ref), (yc_ref, m2_ref))):
        br = jnp.dot(y_ref[...], wbr_ref[n], preferred_element_type=F32)
        term = jax.nn.sigmoid(m_ref[...].astype(F32)) * br
        mixed = term if mixed is None else mixed + term
    o = jnp.dot(mixed.astype(BF16), wout_ref[...], preferred_element_type=F32)
    o_ref[...] = x_ref[...] + mod_ref[0, 2:3, :] * _rms(o, nw_ref[...])


FFN_TILE = 256
FFN_SLAB = 256


def _ffn_down_kernel(ap_ref, ac_ref, an_ref, g_ref, cw_ref, cb_ref, wd_ref, x_ref, mod_ref, nw_ref,
                     o_ref, u_scr, *, n_ctx_tok, ctx_len, lat_len):
    i = pl.program_id(0)
    tok0 = i * FFN_TILE
    is_ctx = tok0 < n_ctx_tok
    tiles_per_lat = lat_len // FFN_TILE
    lat_tile = jnp.maximum(tok0 - n_ctx_tok, 0) // FFN_TILE % tiles_per_lat
    has_up = jnp.logical_and(jnp.logical_not(is_ctx), lat_tile != 0)
    has_dn = jnp.logical_and(jnp.logical_not(is_ctx), lat_tile != tiles_per_lat - 1)
    width = jnp.where(is_ctx, ctx_len, GRID_W)
    ext_rows = FFN_TILE + 2 * GRID_W
    r = lax.broadcasted_iota(jnp.int32, (ext_rows, 1), 0)
    col = (r - GRID_W + width) % width
    ok_left = col != 0
    ok_right = col != width - 1
    vert = jnp.where(is_ctx, 0.0, 1.0)

    for c in range(D_FF // FFN_SLAB):
        sl = slice(c * FFN_SLAB, (c + 1) * FFN_SLAB)
        top = jnp.where(has_up, ap_ref[:, sl].astype(F32), 0.0)
        bot = jnp.where(has_dn, an_ref[:, sl].astype(F32), 0.0)
        ext = jnp.concatenate([top, ac_ref[:, sl].astype(F32), bot], axis=0)
        left = jnp.where(ok_left, pltpu.roll(ext, 1, 0), 0.0)
        right = jnp.where(ok_right, pltpu.roll(ext, ext_rows - 1, 0), 0.0)
        acc = jnp.zeros((FFN_TILE, FFN_SLAB), F32) + cb_ref[:, sl]
        for dy in range(3):
            rows = slice(dy * GRID_W, dy * GRID_W + FFN_TILE)
            for dx, src in enumerate((left, ext, right)):
                w = cw_ref[dy * 3 + dx:dy * 3 + dx + 1, sl]
                if dy != 1:
                    w = w * vert
                acc = acc + w * src[rows, :]
        u_scr[:, sl] = (jax.nn.gelu(acc) * g_ref[:, sl].astype(F32)).astype(BF16)

    f = jnp.dot(u_scr[...], wd_ref[...], preferred_element_type=F32)
    o_ref[...] = x_ref[...] + mod_ref[0, 5:6, :] * _rms(f, nw_ref[...])


def _gate_rows(w_gate):
    idx, keep = [], []
    for p in range(N_HEADS * SUBLANES):
        h, j = divmod(p, SUBLANES)
        d, kind = j % 2, j // 2
        idx.append(d * 2 * N_HEADS + kind * N_HEADS + h if j < 4 else 0)
        keep.append(1.0 if j < 4 else 0.0)
    return w_gate[..., jnp.array(idx)] * jnp.array(keep, w_gate.dtype)


def kernel(x_prompt, x_sample, state_C, state_n, state_m, c, c_ctx, w_mod, b_mod, norm_w, w_in, b_in,
           mlstm_gate_b, pool_w, pool_scale, gmlp_norm_w, gmlp_ws, gmlp_b, w_br, w_out, w_up, conv_w,
           conv_b, w_down):
    n_ctx, ctx_len, d = x_prompt.shape
    n_lat, lat_len, _ = x_sample.shape
    depth = w_in.shape[0]
    assert d == D_MODEL and ctx_len == FFN_TILE and lat_len % FFN_TILE == 0
    n_ctx_tok = n_ctx * ctx_len
    n_tok = n_ctx_tok + n_lat * lat_len
    n_gate = 2 * 2 * N_HEADS
    gate0 = 4 * D_MODEL

    x = jnp.concatenate([x_prompt.reshape(n_ctx_tok, d), x_sample.reshape(n_lat * lat_len, d)], axis=0)

    cvec = jnp.zeros((SUBLANES, d), F32).at[0].set(c_ctx).at[1:1 + n_lat].set(c)
    mod = _modulation(cvec, w_mod, b_mod)

    def mod_spec(tm):
        def row(i):
            t0 = i * tm
            return jnp.where(t0 < n_ctx_tok, 0, 1 + jnp.maximum(t0 - n_ctx_tok, 0) // lat_len)
        return row

    tm_proj = 1024
    tn_in = 1024
    row_in = mod_spec(tm_proj)
    row_256 = mod_spec(FFN_TILE)
    vec = lambda a: a.reshape(1, -1)
    c_states, n_states, m_states = [], [], []

    for l in range(depth):
        w_main = jnp.concatenate([w_in[l, :, :gate0], w_in[l, :, gate0 + n_gate:]], axis=1).astype(BF16)
        b_main = jnp.concatenate([b_in[l, :gate0], b_in[l, gate0 + n_gate:]]).reshape(1, MAIN_WIDTH)
        wg_rows = _gate_rows(w_in[l, :, gate0:gate0 + n_gate]).T.astype(BF16)
        bga_rows = _gate_rows(b_in[l, gate0:gate0 + n_gate]).reshape(-1, 1)
        bgb_rows = _gate_rows(mlstm_gate_b[l].reshape(n_gate)).reshape(-1, 1)
        mod_l = mod[l]
        nw = norm_w[l]
        n_grow = N_HEADS * SUBLANES

        z, gt = pl.pallas_call(
            _inproj_kernel,
            grid=(n_tok // tm_proj, MAIN_WIDTH // tn_in),
            in_specs=[
                pl.BlockSpec((tm_proj, d), lambda i, j: (i, 0)),
                pl.BlockSpec((1, N_MOD, d), lambda i, j: (row_in(i), 0, 0)),
                pl.BlockSpec((1, d), lambda i, j: (0, 0)),
                pl.BlockSpec((d, tn_in), lambda i, j: (0, j)),
                pl.BlockSpec((1, tn_in), lambda i, j: (0, j)),
                pl.BlockSpec((n_grow, d), lambda i, j: (0, 0)),
                pl.BlockSpec((n_grow, 1), lambda i, j: (0, 0)),
                pl.BlockSpec((n_grow, 1), lambda i, j: (0, 0)),
            ],
            out_specs=[
                pl.BlockSpec((tm_proj, tn_in), lambda i, j: (i, j)),
                pl.BlockSpec((n_grow, tm_proj), lambda i, j: (0, i)),
            ],
            out_shape=[
                jax.ShapeDtypeStruct((n_tok, MAIN_WIDTH), BF16),
                jax.ShapeDtypeStruct((n_grow, n_tok), F32),
            ],
            scratch_shapes=[pltpu.VMEM((tm_proj, d), BF16)],
            compiler_params=_params(("arbitrary", "arbitrary")),
            name="inproj",
        )(x, mod_l, vec(nw[0]), w_main, b_main, wg_rows, bga_rows, bgb_rows)

        ya_ctx, (c_l, n_l, m_l) = _mlstm(z, gt, 0, n_ctx, ctx_len)
        ya_lat, _ = _mlstm(z, gt, n_ctx_tok, n_lat, lat_len,
                           init=(state_C, state_n, state_m), layer=l)
        ya = jnp.concatenate([ya_ctx, ya_lat], axis=0)
        c_states.append(c_l)
        n_states.append(n_l)
        m_states.append(m_l)

        n_chunks = n_tok // CHUNK
        zcol = lambda col: pl.BlockSpec((CHUNK, d), lambda i: (i, col))
        full = lambda shape: pl.BlockSpec(shape, lambda i: (0,) * len(shape))
        yb, yc = pl.pallas_call(
            functools.partial(_mixers_kernel, n_ctx_tok=n_ctx_tok, ctx_len=ctx_len, lat_len=lat_len),
            grid=(n_chunks,),
            in_specs=[
                pl.BlockSpec((CHUNK, d), lambda i: (jnp.maximum(i - 1, 0), COL_P)),
                zcol(COL_P),
                pl.BlockSpec((CHUNK, d), lambda i: (jnp.minimum(i + 1, n_chunks - 1), COL_P)),
                zcol(COL_U), zcol(COL_VG),
                full((N_GROUPS, GROUP_DIM, GROUP_DIM)), full((1, d)), full((1, d)),
                full((N_GROUPS, CHUNK, CHUNK)), full((CHUNK, N_GROUPS)),
            ],
            out_specs=[pl.BlockSpec((CHUNK, d), lambda i: (i, 0))] * 2,
            out_shape=[jax.ShapeDtypeStruct((n_tok, d), BF16)] * 2,
            compiler_params=_params(("arbitrary",)),
            name="mixers",
        )(z, z, z, z, z, pool_w[l].astype(BF16), vec(pool_scale[l]), vec(gmlp_norm_w[l]),
          gmlp_ws[l].astype(BF16), gmlp_b[l].T)

        tm = FFN_TILE
        tok = lambda: pl.BlockSpec((tm, d), lambda i: (i, 0))
        x = pl.pallas_call(
            _merge_kernel,
            grid=(n_tok // tm,),
            in_specs=[
                tok(), tok(), tok(),
                pl.BlockSpec((tm, d), lambda i: (i, COL_M)),
                pl.BlockSpec((tm, d), lambda i: (i, COL_M + 1)),
                pl.BlockSpec((tm, d), lambda i: (i, COL_M + 2)),
                tok(),
                pl.BlockSpec((1, N_MOD, d), lambda i: (row_256(i), 0, 0)),
                full((1, d)), full((3, d, d)), full((d, d)),
            ],
            out_specs=tok(),
            out_shape=jax.ShapeDtypeStruct((n_tok, d), F32),
            compiler_params=_params(("arbitrary",)),
            name="merge",
        )(ya, yb, yc, z, z, z, x, mod_l, vec(nw[1]), w_br[l].astype(BF16), w_out[l].astype(BF16))

        up = pl.pallas_call(
            _ffn_up_kernel,
            grid=(n_tok // tm_proj, 2),
            in_specs=[
                pl.BlockSpec((tm_proj, d), lambda i, j: (i, 0)),
                pl.BlockSpec((1, N_MOD, d), lambda i, j: (row_in(i), 0, 0)),
                pl.BlockSpec((1, d), lambda i, j: (0, 0)),
                pl.BlockSpec((d, D_FF), lambda i, j: (0, j)),
            ],
            out_specs=pl.BlockSpec((tm_proj, D_FF), lambda i, j: (i, j)),
            out_shape=jax.ShapeDtypeStruct((n_tok, 2 * D_FF), BF16),
            scratch_shapes=[pltpu.VMEM((tm_proj, d), BF16)],
            compiler_params=_params(("arbitrary", "arbitrary")),
            name="ffn_up",
        )(x, mod_l, vec(nw[2]), w_up[l].astype(BF16))

        rows_per_tile = tm // GRID_W
        n_rows = n_tok // GRID_W
        x = pl.pallas_call(
            functools.partial(_ffn_down_kernel, n_ctx_tok=n_ctx_tok, ctx_len=ctx_len, lat_len=lat_len),
            grid=(n_tok // tm,),
            in_specs=[
                pl.BlockSpec((GRID_W, D_FF), lambda i: (jnp.maximum(i * rows_per_tile - 1, 0), 0)),
                pl.BlockSpec((tm, D_FF), lambda i: (i, 0)),
                pl.BlockSpec((GRID_W, D_FF),
                             lambda i: (jnp.minimum((i + 1) * rows_per_tile, n_rows - 1), 0)),
                pl.BlockSpec((tm, D_FF), lambda i: (i, 1)),
                full((9, D_FF)), full((1, D_FF)), full((D_FF, d)),
                tok(),
                pl.BlockSpec((1, N_MOD, d), lambda i: (row_256(i), 0, 0)),
                full((1, d)),
            ],
            out_specs=tok(),
            out_shape=jax.ShapeDtypeStruct((n_tok, d), F32),
            scratch_shapes=[pltpu.VMEM((tm, D_FF), BF16)],
            compiler_params=_params(("arbitrary",)),
            name="ffn_down",
        )(up, up, up, up, conv_w[l].reshape(9, D_FF), vec(conv_b[l]), w_down[l].astype(BF16),
          x, mod_l, vec(nw[3]))

    y_prompt = x[:n_ctx_tok].reshape(n_ctx, ctx_len, d)
    y_sample = x[n_ctx_tok:].reshape(n_lat, lat_len, d)
    return (y_prompt, y_sample, jnp.stack(c_states, axis=1), jnp.stack(n_states, axis=1),
            jnp.stack(m_states, axis=1))
```

```python
import functools

import jax
import jax.numpy as jnp
from jax import lax
from jax.experimental import pallas as pl
from jax.experimental.pallas import tpu as pltpu

F32 = jnp.float32
BF16 = jnp.bfloat16

D_MODEL = 1024
N_HEADS = 4
HEAD_DIM = D_MODEL // N_HEADS
CHUNK = 128
POOL_WINDOWS = (2, 4, 8, 16)
N_GROUPS = 4
GROUP_DIM = D_MODEL // N_GROUPS
D_FF = 2816
GRID_W = 64
N_MOD = 6
RMS_EPS = 1e-6
LANES = 128
SUBLANES = 8
VMEM_LIMIT = 56 * 1024 * 1024

COL_Q, COL_K, COL_V, COL_O, COL_P, COL_U, COL_VG, COL_M = 0, 1, 2, 3, 4, 5, 6, 7
MAIN_WIDTH = 10 * D_MODEL

NT_DIMS = (((1,), (1,)), ((), ()))


def _rms(x, w):
    return x * lax.rsqrt(jnp.mean(x * x, axis=-1, keepdims=True) + RMS_EPS) * w


def _log_sigmoid(x):
    return jnp.minimum(x, 0.0) - jnp.log1p(jnp.exp(-jnp.abs(x)))


def _params(sem):
    return pltpu.CompilerParams(dimension_semantics=sem, vmem_limit_bytes=VMEM_LIMIT)


def _mod_kernel(c_ref, w_ref, b_ref, o_ref):
    c = c_ref[...]
    s = (c * jax.nn.sigmoid(c)).astype(BF16)
    o_ref[...] = jnp.dot(s, w_ref[...].astype(BF16), preferred_element_type=F32) + b_ref[...]


def _modulation(cvec, w_mod, b_mod):
    depth = w_mod.shape[0]
    n = N_MOD * D_MODEL
    tn = 1536
    out = pl.pallas_call(
        _mod_kernel,
        grid=(depth, n // tn),
        in_specs=[
            pl.BlockSpec((SUBLANES, D_MODEL), lambda l, j: (0, 0)),
            pl.BlockSpec((None, D_MODEL, tn), lambda l, j: (l, 0, j)),
            pl.BlockSpec((None, 1, tn), lambda l, j: (l, 0, j)),
        ],
        out_specs=pl.BlockSpec((None, SUBLANES, tn), lambda l, j: (l, 0, j)),
        out_shape=jax.ShapeDtypeStruct((depth, SUBLANES, n), F32),
        compiler_params=_params(("arbitrary", "arbitrary")),
        name="modulation",
    )(cvec, w_mod, b_mod.reshape(depth, 1, n))
    return out.reshape(depth, SUBLANES, N_MOD, D_MODEL)


def _modnorm(x, mod_ref, nw_ref, k_shift, k_scale):
    h = _rms(x, nw_ref[...])
    return h * (1.0 + mod_ref[0, k_scale:k_scale + 1, :]) + mod_ref[0, k_shift:k_shift + 1, :]


def _pick(refs, first):
    if len(refs) == 1:
        return refs[0][...]
    return jnp.where(first, refs[0][...], refs[1][...])


def _inproj_kernel(*refs, n_x, n_first, n_wa):
    x_refs = refs[:n_x]
    (mod_ref, nw_ref, wa_ref, ba_ref, wb_ref, bb_ref, wgt_ref, bgta_ref, bgtb_ref,
     z_ref, gt_ref, h_scr) = refs[n_x:]
    j = pl.program_id(1)

    @pl.when(j == 0)
    def _():
        x = _pick(x_refs, pl.program_id(0) < n_first)
        hb = _modnorm(x, mod_ref, nw_ref, 0, 1).astype(BF16)
        h_scr[...] = hb
        gt_ref[...] = (lax.dot_general(wgt_ref[...], hb, NT_DIMS, preferred_element_type=F32)
                       + bgta_ref[...] + bgtb_ref[...])

    @pl.when(j < n_wa)
    def _():
        z = jnp.dot(h_scr[...], wa_ref[...], preferred_element_type=F32) + ba_ref[...]
        z_ref[...] = z.astype(z_ref.dtype)

    @pl.when(j >= n_wa)
    def _():
        z = jnp.dot(h_scr[...], wb_ref[...], preferred_element_type=F32) + bb_ref[...]
        z_ref[...] = z.astype(z_ref.dtype)


def _gates_kernel(g_ref, a_ref, b_ref, cm_ref):
    g = g_ref[...]
    fwd = lax.broadcasted_iota(jnp.int32, g.shape, 0) % SUBLANES == 0
    lane = lax.broadcasted_iota(jnp.int32, g.shape, 1) % CHUNK
    lf = _log_sigmoid(pltpu.roll(g, g.shape[0] - 2, 0))
    pre, suf = _chunk_scans(lf, jnp.add, 0.0, lane)
    b = jnp.where(fwd, pre, suf)
    a = g - b
    pre, suf = _chunk_scans(a, jnp.maximum, -jnp.inf, lane)
    a_ref[...] = a
    b_ref[...] = b
    cm_ref[...] = jnp.where(fwd, pre, suf)


def _chunk_scans(x, op, fill, lane):
    width = x.shape[1]
    fwd = bwd = x
    k = 1
    while k < CHUNK:
        fwd = op(fwd, jnp.where(lane >= k, pltpu.roll(fwd, k, 1), fill))
        bwd = op(bwd, jnp.where(lane < CHUNK - k, pltpu.roll(bwd, width - k, 1), fill))
        k *= 2
    return fwd, bwd


def _mlstm_kernel(*refs, seq_len, has_init, emit_state, n_prev):
    q_ref, k_ref, v_ref, o_ref, a_ref, b_ref, cm_ref = refs[:7]
    pos = 7
    if has_init:
        c0_ref, n0_ref, m0_ref = refs[pos:pos + 3]
        pos += 3
    if n_prev:
        cprev_ref, nprev_ref, mprev_ref = refs[pos:pos + 3]
        pos += 3
    ya_ref = refs[pos]
    pos += 1
    if emit_state:
        cout_ref, nout_ref, mout_ref = refs[pos:pos + 3]
        pos += 3
    rows_scr, wk16_scr, dec_scr, vt_scr, vtw_scr, cb_scr, n1_scr, c_scr = refs[pos:pos + 8]

    n_chunks = seq_len // CHUNK
    scale = HEAD_DIM ** -0.5
    row81 = lax.broadcasted_iota(jnp.int32, (SUBLANES, 1), 0)
    is_fwd1 = row81 == 0
    row_n = lax.broadcasted_iota(jnp.int32, (SUBLANES, HEAD_DIM), 0)

    chunk = lambda ref, c: ref[:, c * CHUNK:(c + 1) * CHUNK]
    a_c = [chunk(a_ref, c) for c in range(n_chunks)]
    b_c = [chunk(b_ref, c) for c in range(n_chunks)]
    cm_c = [chunk(cm_ref, c) for c in range(n_chunks)]
    row8 = lax.broadcasted_iota(jnp.int32, (SUBLANES, CHUNK), 0)
    lane8 = lax.broadcasted_iota(jnp.int32, (SUBLANES, CHUNK), 1)
    last = lane8 == jnp.where(row8 == 0, CHUNK - 1, 0)
    amax_c = [jnp.max(a, axis=1, keepdims=True) for a in a_c]
    bend_c = [jnp.sum(jnp.where(last, b, 0.0), axis=1, keepdims=True) for b in b_c]

    if has_init:
        m_state = jnp.where(is_fwd1, m0_ref[0], jnp.where(row81 == 1, m0_ref[1], 0.0))
        n_init = jnp.where(row_n == 0, n0_ref[0], n0_ref[1])
        c_scr[...] = c0_ref[...]
    else:
        m_state = jnp.zeros((SUBLANES, 1), F32)
        n_init = jnp.zeros((SUBLANES, HEAD_DIM), F32)
        c_scr[...] = jnp.zeros(c_scr.shape, F32)

    m_before, top = [], []
    for j in range(n_chunks):
        jr = n_chunks - 1 - j
        amax = jnp.where(is_fwd1, amax_c[j], amax_c[jr])
        bend = jnp.where(is_fwd1, bend_c[j], bend_c[jr])
        m_before.append(m_state)
        top.append(jnp.maximum(m_state, amax))
        m_state = bend + top[-1]

    for c in range(n_chunks):
        cr = n_chunks - 1 - c
        m_c = jnp.where(is_fwd1, m_before[c], m_before[cr])
        top_c = jnp.where(is_fwd1, top[c], top[cr])
        big_m = jnp.maximum(m_c, cm_c[c])
        w_k = jnp.exp(a_c[c] - top_c)
        rows_scr[c, 0] = a_c[c]
        rows_scr[c, 1] = big_m
        rows_scr[c, 2] = jnp.exp(m_c - big_m)
        rows_scr[c, 3] = jnp.exp(-b_c[c] - big_m)
        rows_scr[c, 4] = w_k
        wk16_scr[c] = jnp.concatenate([w_k, jnp.zeros_like(w_k)], axis=0).astype(BF16)
        dec_scr[c] = jnp.broadcast_to(jnp.exp(m_c - top_c), (SUBLANES, HEAD_DIM))

    def load_k(r0):
        return (k_ref[pl.ds(r0, CHUNK), :].astype(F32) * scale).astype(BF16)

    def pass_a(j, nst):
        c = n_chunks - 1 - j
        r0 = pl.multiple_of(c * CHUNK, CHUNK)
        ks = load_k(r0)
        vt = v_ref[pl.ds(r0, CHUNK), :].astype(F32).T
        w_k = rows_scr[c, 4]
        vt_scr[c] = vt.astype(BF16)
        vtw_scr[c] = (vt * w_k[0:1, :]).astype(BF16)
        c_in = c_scr[1]
        cb_scr[c] = c_in.astype(BF16)
        n1_scr[c] = nst
        dec = dec_scr[c]
        c_scr[1] = dec[1:2, :] * c_in + jnp.dot((vt * w_k[1:2, :]).astype(BF16), ks,
                                                preferred_element_type=F32)
        n_upd = jnp.dot(wk16_scr[c], ks, preferred_element_type=F32)[:SUBLANES]
        return dec * nst + n_upd

    n_rev = lax.fori_loop(0, n_chunks, pass_a, n_init, unroll=2)

    row_id = lax.broadcasted_iota(jnp.int32, (CHUNK, CHUNK), 0)
    col_id = lax.broadcasted_iota(jnp.int32, (CHUNK, CHUNK), 1)
    tri = (row_id <= col_id, row_id >= col_id)

    def pass_b(c, nst):
        r0 = pl.multiple_of(c * CHUNK, CHUNK)
        q = q_ref[pl.ds(r0, CHUNK), :]
        ks = load_k(r0)
        vt = vt_scr[c]
        a_r, bigm_r, winter_r, floor_r = (rows_scr[c, i] for i in range(4))
        st_all = lax.dot_general(ks, q, NT_DIMS, preferred_element_type=F32)
        n_rows = jnp.where(row_n == 0, nst, n1_scr[c])
        n_rows = jnp.concatenate([n_rows, jnp.zeros_like(n_rows)], axis=0).astype(BF16)
        qn = lax.dot_general(n_rows, q, NT_DIMS, preferred_element_type=F32)
        c0 = c_scr[0]
        ht = None
        for d in range(2):
            row = slice(d, d + 1)
            a_st = jnp.broadcast_to(a_r[row, :], (CHUNK, CHUNK)).T
            st = st_all * jnp.exp(jnp.where(tri[d], a_st - bigm_r[row, :], -jnp.inf))
            den = jnp.sum(st, axis=0, keepdims=True) + winter_r[row, :] * qn[row, :]
            inv = 1.0 / jnp.maximum(jnp.abs(den), floor_r[row, :])
            cb = c0.astype(BF16) if d == 0 else cb_scr[c]
            num_t = (jnp.dot(vt, st.astype(BF16), preferred_element_type=F32)
                     + winter_r[row, :] * lax.dot_general(cb, q, NT_DIMS,
                                                          preferred_element_type=F32))
            ht = num_t * inv if ht is None else ht + num_t * inv
        gate = jax.nn.sigmoid(o_ref[pl.ds(r0, CHUNK), :].astype(F32))
        ya_ref[pl.ds(r0, CHUNK), :] = (gate * ht.T).astype(ya_ref.dtype)

        dec = dec_scr[c]
        c_scr[0] = dec[0:1, :] * c0 + jnp.dot(vtw_scr[c], ks, preferred_element_type=F32)
        n_upd = jnp.dot(wk16_scr[c], ks, preferred_element_type=F32)[:SUBLANES]
        return dec * nst + n_upd

    n_fwd = lax.fori_loop(0, n_chunks, pass_b, n_init, unroll=2)

    if emit_state:
        if n_prev:
            cout_ref[:n_prev] = cprev_ref[...]
            nout_ref[:n_prev] = nprev_ref[...]
            mout_ref[:n_prev] = mprev_ref[...]
        cout_ref[n_prev] = c_scr[...]
        nout_ref[n_prev, 0] = n_fwd[0:1, :]
        nout_ref[n_prev, 1] = n_rev[1:2, :]
        mout_ref[n_prev, 0] = m_state[0:1, :]
        mout_ref[n_prev, 1] = m_state[1:2, :]


def _mlstm(z, scans, tok0, n_batch, seq_len, init=None, layer=0, prev=None):
    blk0 = tok0 // seq_len
    n_chunks = seq_len // CHUNK
    has_init = init is not None
    emit_state = not has_init
    n_prev = 0 if prev is None else prev[0].shape[1]

    def zspec(col):
        return pl.BlockSpec((seq_len, HEAD_DIM),
                            lambda b, h: (blk0 + b, col * N_HEADS + h))

    scan_spec = pl.BlockSpec((SUBLANES, seq_len), lambda b, h: (h, blk0 + b))
    in_specs = [zspec(COL_Q), zspec(COL_K), zspec(COL_V), zspec(COL_O),
                scan_spec, scan_spec, scan_spec]
    args = [z, z, z, z, *scans]

    def state_specs(n_layers):
        return [
            pl.BlockSpec((None, n_layers, 2, None, HEAD_DIM, HEAD_DIM),
                         lambda b, h: (b, 0, 0, h, 0, 0)),
            pl.BlockSpec((None, n_layers, 2, None, 1, HEAD_DIM), lambda b, h: (b, 0, 0, h, 0, 0)),
            pl.BlockSpec((None, n_layers, 2, None, 1, 1), lambda b, h: (b, 0, 0, h, 0, 0)),
        ]

    if has_init:
        sc, sn, sm = init
        nb, depth = sc.shape[:2]
        in_specs += [
            pl.BlockSpec((None, None, 2, None, HEAD_DIM, HEAD_DIM),
                         lambda b, h: (b, layer, 0, h, 0, 0)),
            pl.BlockSpec((None, None, 2, None, 1, HEAD_DIM), lambda b, h: (b, layer, 0, h, 0, 0)),
            pl.BlockSpec((None, None, 2, None, 1, 1), lambda b, h: (b, layer, 0, h, 0, 0)),
        ]
        args += [sc, sn.reshape(nb, depth, 2, N_HEADS, 1, HEAD_DIM),
                 sm.reshape(nb, depth, 2, N_HEADS, 1, 1)]
    if n_prev:
        in_specs += state_specs(n_prev)
        args += list(prev)
    out_specs = [pl.BlockSpec((seq_len, HEAD_DIM), lambda b, h: (b, h))]
    out_shape = [jax.ShapeDtypeStruct((n_batch * seq_len, D_MODEL), BF16)]
    if emit_state:
        n_out = n_prev + 1
        out_specs += state_specs(n_out)
        out_shape += [
            jax.ShapeDtypeStruct((n_batch, n_out, 2, N_HEADS, HEAD_DIM, HEAD_DIM), F32),
            jax.ShapeDtypeStruct((n_batch, n_out, 2, N_HEADS, 1, HEAD_DIM), F32),
            jax.ShapeDtypeStruct((n_batch, n_out, 2, N_HEADS, 1, 1), F32),
        ]
    outs = pl.pallas_call(
        functools.partial(_mlstm_kernel, seq_len=seq_len, has_init=has_init, emit_state=emit_state,
                          n_prev=n_prev),
        grid=(n_batch, N_HEADS),
        in_specs=in_specs,
        out_specs=out_specs,
        out_shape=out_shape,
        scratch_shapes=[
            pltpu.VMEM((n_chunks, 5, SUBLANES, CHUNK), F32),
            pltpu.VMEM((n_chunks, 2 * SUBLANES, CHUNK), BF16),
            pltpu.VMEM((n_chunks, SUBLANES, HEAD_DIM), F32),
            pltpu.VMEM((n_chunks, HEAD_DIM, CHUNK), BF16),
            pltpu.VMEM((n_chunks, HEAD_DIM, CHUNK), BF16),
            pltpu.VMEM((n_chunks, HEAD_DIM, HEAD_DIM), BF16),
            pltpu.VMEM((n_chunks, SUBLANES, HEAD_DIM), F32),
            pltpu.VMEM((2, HEAD_DIM, HEAD_DIM), F32),
        ],
        compiler_params=_params(("arbitrary", "arbitrary")),
        name="mlstm_ctx" if emit_state else "mlstm_lat",
    )(*args)
    if emit_state:
        return outs[0], tuple(outs[1:])
    return outs[0], None


def _mixers_kernel(pp_ref, pc_ref, pn_ref, u_ref, vg_ref, pw_ref, ps_ref, gnw_ref, ws_ref, gb_ref,
                   yb_ref, yc_ref, *, n_ctx_tok, ctx_len, lat_len):
    tok0 = pl.program_id(0) * CHUNK
    is_ctx = tok0 < n_ctx_tok
    seq_len = jnp.where(is_ctx, ctx_len, lat_len)
    base = jnp.where(is_ctx, 0, n_ctx_tok)
    seq_start = base + ((tok0 - base) // seq_len) * seq_len
    seq_end = seq_start + seq_len

    t_abs = tok0 + lax.broadcasted_iota(jnp.int32, (CHUNK, CHUNK), 0)
    u_rel = lax.broadcasted_iota(jnp.int32, (CHUNK, CHUNK), 1)
    t_col = tok0 + lax.broadcasted_iota(jnp.int32, (CHUNK, 1), 0)
    for g, win in enumerate(POOL_WINDOWS):
        half = win // 2
        sl = slice(g * GROUP_DIM, (g + 1) * GROUP_DIM)
        acc = jnp.zeros((CHUNK, GROUP_DIM), F32)
        for ref, off in ((pp_ref, -CHUNK), (pc_ref, 0), (pn_ref, CHUNK)):
            u_abs = tok0 + off + u_rel
            band = ((u_abs >= t_abs - half) & (u_abs < t_abs + half)
                    & (u_abs >= seq_start) & (u_abs < seq_end))
            acc += jnp.dot(band.astype(F32).astype(BF16), ref[:, sl], preferred_element_type=F32)
        count = (jnp.minimum(t_col + half, seq_end) - jnp.maximum(t_col - half, seq_start)).astype(F32)
        y = (acc / count - pc_ref[:, sl].astype(F32)).astype(BF16)
        yb = jnp.dot(y, pw_ref[g], preferred_element_type=F32) * ps_ref[:, sl]
        yb_ref[:, sl] = yb.astype(yb_ref.dtype)

    u = jax.nn.gelu(u_ref[...].astype(F32))
    vg = _rms(jax.nn.gelu(vg_ref[...].astype(F32)), gnw_ref[...]).astype(BF16)
    for g in range(N_GROUPS):
        sl = slice(g * GROUP_DIM, (g + 1) * GROUP_DIM)
        s = jnp.dot(ws_ref[g], vg[:, sl], preferred_element_type=F32) + gb_ref[:, g:g + 1]
        yc_ref[:, sl] = (u[:, sl] * s).astype(yc_ref.dtype)


def _merge_kernel(*refs, n_x, n_first):
    x_refs = refs[:n_x]
    (ya_ctx_ref, ya_lat_ref, yb_ref, yc_ref, m0_ref, m1_ref, m2_ref, mod_ref, nw_ref,
     wbr_ref, wout_ref, o_ref) = refs[n_x:]
    first = pl.program_id(0) < n_first
    ys = (_pick((ya_ctx_ref, ya_lat_ref), first), yb_ref[...], yc_ref[...])
    mixed = None
    for n, (y, m_ref) in enumerate(zip(ys, (m0_ref, m1_ref, m2_ref))):
        br = jnp.dot(y, wbr_ref[n], preferred_element_type=F32)
        term = jax.nn.sigmoid(m_ref[...].astype(F32)) * br
        mixed = term if mixed is None else mixed + term
    o = jnp.dot(mixed.astype(BF16), wout_ref[...], preferred_element_type=F32)
    o_ref[...] = _pick(x_refs, first) + mod_ref[0, 2:3, :] * _rms(o, nw_ref[...])


TOK_TILE = 256
FFN_TILE = 512
FFN_SLAB = 256


def _ffn_kernel(xp_ref, xc_ref, xn_ref, mod_ref, nw2_ref, nw3_ref, wup_ref, cw_ref, cb_ref, wd_ref,
                *rest, n_ctx_tok, ctx_len, lat_len):
    *out_refs, f_scr = rest
    i = pl.program_id(0)
    tok0 = i * FFN_TILE
    is_ctx = tok0 < n_ctx_tok
    tiles_per_lat = lat_len // FFN_TILE
    lat_tile = jnp.maximum(tok0 - n_ctx_tok, 0) // FFN_TILE % tiles_per_lat
    has_up = jnp.logical_and(jnp.logical_not(is_ctx), lat_tile != 0)
    has_dn = jnp.logical_and(jnp.logical_not(is_ctx), lat_tile != tiles_per_lat - 1)
    ext_rows = FFN_TILE + 2 * GRID_W
    r = lax.broadcasted_iota(jnp.int32, (ext_rows, FFN_SLAB), 0)
    col_ctx = (r - GRID_W + ctx_len) % ctx_len
    col_lat = r % GRID_W
    keep_left = jnp.where(jnp.where(is_ctx, col_ctx, col_lat) != 0, 1.0, 0.0)
    keep_right = jnp.where(jnp.where(is_ctx, col_ctx - (ctx_len - 1), col_lat - (GRID_W - 1)) != 0,
                           1.0, 0.0)
    r1 = lax.broadcasted_iota(jnp.int32, (ext_rows, 1), 0)
    in_grid = jnp.logical_and(jnp.logical_or(r1 >= GRID_W, has_up),
                              jnp.logical_or(r1 < GRID_W + FFN_TILE, has_dn))
    tap_on = jnp.where(lax.broadcasted_iota(jnp.int32, (9, 1), 0) // 3 == 1, 1.0,
                       jnp.where(is_ctx, 0.0, 1.0))
    cw = cw_ref[...] * tap_on

    x_ext = jnp.concatenate([xp_ref[...], xc_ref[...], xn_ref[...]], axis=0)
    h_ext = jnp.where(in_grid, _modnorm(x_ext, mod_ref, nw2_ref, 3, 4), 0.0).astype(BF16)
    h_cur = h_ext[GRID_W:GRID_W + FFN_TILE]

    def up(c):
        sl = slice(c * FFN_SLAB, (c + 1) * FFN_SLAB)
        sl_gate = slice(D_FF + c * FFN_SLAB, D_FF + (c + 1) * FFN_SLAB)
        return (jnp.dot(h_ext, wup_ref[:, sl], preferred_element_type=F32),
                jnp.dot(h_cur, wup_ref[:, sl_gate], preferred_element_type=F32))

    n_slabs = D_FF // FFN_SLAB
    nxt = up(0)
    for c in range(n_slabs):
        sl = slice(c * FFN_SLAB, (c + 1) * FFN_SLAB)
        ext, gate = nxt
        if c + 1 < n_slabs:
            nxt = up(c + 1)
        left = pltpu.roll(ext, 1, 0) * keep_left
        right = pltpu.roll(ext, ext_rows - 1, 0) * keep_right
        acc = jnp.zeros((FFN_TILE, FFN_SLAB), F32) + cb_ref[:, sl]
        for dy in range(3):
            rows = slice(dy * GRID_W, dy * GRID_W + FFN_TILE)
            for dx, src in enumerate((left, ext, right)):
                acc = acc + cw[dy * 3 + dx:dy * 3 + dx + 1, sl] * src[rows, :]
        u = (jax.nn.gelu(acc) * gate).astype(BF16)
        part = jnp.dot(u, wd_ref[sl, :], preferred_element_type=F32)
        if c == 0:
            f_scr[...] = part
        else:
            f_scr[...] += part

    y = xc_ref[...] + mod_ref[0, 5:6, :] * _rms(f_scr[...], nw3_ref[...])
    if len(out_refs) == 1:
        out_refs[0][...] = y
    else:
        @pl.when(is_ctx)
        def _():
            out_refs[0][...] = y

        @pl.when(jnp.logical_not(is_ctx))
        def _():
            out_refs[1][...] = y


def _gate_rows(w_gate):
    idx, keep = [], []
    for p in range(N_HEADS * SUBLANES):
        h, j = divmod(p, SUBLANES)
        d, kind = j % 2, j // 2
        idx.append(d * 2 * N_HEADS + kind * N_HEADS + h if j < 4 else 0)
        keep.append(1.0 if j < 4 else 0.0)
    return w_gate[..., jnp.array(idx)] * jnp.array(keep, w_gate.dtype)


def kernel(x_prompt, x_sample, state_C, state_n, state_m, c, c_ctx, w_mod, b_mod, norm_w, w_in, b_in,
           mlstm_gate_b, pool_w, pool_scale, gmlp_norm_w, gmlp_ws, gmlp_b, w_br, w_out, w_up, conv_w,
           conv_b, w_down):
    n_ctx, ctx_len, d = x_prompt.shape
    n_lat, lat_len, _ = x_sample.shape
    depth = w_in.shape[0]
    n_ctx_tok = n_ctx * ctx_len
    n_lat_tok = n_lat * lat_len
    n_tok = n_ctx_tok + n_lat_tok
    assert d == D_MODEL and FFN_TILE % ctx_len == 0 and lat_len % FFN_TILE == 0
    assert n_ctx_tok % FFN_TILE == 0 and ctx_len % CHUNK == 0
    n_gate = 2 * 2 * N_HEADS
    gate0 = 4 * D_MODEL

    x_parts = (x_prompt.reshape(n_ctx_tok, d), x_sample.reshape(n_lat_tok, d))

    def tok_specs(n_parts, tm):
        if n_parts == 1:
            return [pl.BlockSpec((tm, d), lambda i, *_: (i, 0))]
        n_a = n_ctx_tok // tm
        return [pl.BlockSpec((tm, d), lambda i, *_: (jnp.minimum(i, n_a - 1), 0)),
                pl.BlockSpec((tm, d), lambda i, *_: (jnp.maximum(i - n_a, 0), 0))]

    cvec = jnp.zeros((SUBLANES, d), F32).at[0].set(c_ctx).at[1:1 + n_lat].set(c)
    mod = _modulation(cvec, w_mod, b_mod)

    def mod_spec(tm):
        def row(i):
            t0 = i * tm
            return jnp.where(t0 < n_ctx_tok, 0, 1 + jnp.maximum(t0 - n_ctx_tok, 0) // lat_len)
        return row

    tm_proj = 1024
    tn_in = 1024
    row_in = mod_spec(tm_proj)
    row_tok = mod_spec(TOK_TILE)
    row_ffn = mod_spec(FFN_TILE)
    vec = lambda a: a.reshape(1, -1)
    states = None
    n_grow = N_HEADS * SUBLANES
    n_wa = gate0 // tn_in
    n_wb = (MAIN_WIDTH - gate0) // tn_in
    scan_lanes = 2048

    for l in range(depth):
        last_layer = l == depth - 1
        w_a = w_in[l, :, :gate0].astype(BF16)
        w_b = w_in[l, :, gate0 + n_gate:].astype(BF16)
        wg_rows = _gate_rows(w_in[l, :, gate0:gate0 + n_gate]).T.astype(BF16)
        bga_rows = _gate_rows(b_in[l, gate0:gate0 + n_gate]).reshape(-1, 1)
        bgb_rows = _gate_rows(mlstm_gate_b[l].reshape(n_gate)).reshape(-1, 1)
        mod_l = mod[l]
        nw = norm_w[l]

        z, gt = pl.pallas_call(
            functools.partial(_inproj_kernel, n_x=len(x_parts), n_first=n_ctx_tok // tm_proj, n_wa=n_wa),
            grid=(n_tok // tm_proj, n_wa + n_wb),
            in_specs=tok_specs(len(x_parts), tm_proj) + [
                pl.BlockSpec((1, N_MOD, d), lambda i, j: (row_in(i), 0, 0)),
                pl.BlockSpec((1, d), lambda i, j: (0, 0)),
                pl.BlockSpec((d, tn_in), lambda i, j: (0, jnp.minimum(j, n_wa - 1))),
                pl.BlockSpec((1, tn_in), lambda i, j: (0, jnp.minimum(j, n_wa - 1))),
                pl.BlockSpec((d, tn_in), lambda i, j: (0, jnp.maximum(j - n_wa, 0))),
                pl.BlockSpec((1, tn_in), lambda i, j: (0, jnp.maximum(j - n_wa, 0))),
                pl.BlockSpec((n_grow, d), lambda i, j: (0, 0)),
                pl.BlockSpec((n_grow, 1), lambda i, j: (0, 0)),
                pl.BlockSpec((n_grow, 1), lambda i, j: (0, 0)),
            ],
            out_specs=[
                pl.BlockSpec((tm_proj, tn_in), lambda i, j: (i, j)),
                pl.BlockSpec((n_grow, tm_proj), lambda i, j: (0, i)),
            ],
            out_shape=[
                jax.ShapeDtypeStruct((n_tok, MAIN_WIDTH), BF16),
                jax.ShapeDtypeStruct((n_grow, n_tok), F32),
            ],
            scratch_shapes=[pltpu.VMEM((tm_proj, d), BF16)],
            compiler_params=_params(("arbitrary", "arbitrary")),
            name="inproj",
        )(*x_parts, mod_l, vec(nw[0]), w_a, vec(b_in[l, :gate0]), w_b, vec(b_in[l, gate0 + n_gate:]),
          wg_rows, bga_rows, bgb_rows)

        scan_spec = pl.BlockSpec((n_grow, scan_lanes), lambda i: (0, i))
        scans = pl.pallas_call(
            _gates_kernel,
            grid=(n_tok // scan_lanes,),
            in_specs=[scan_spec],
            out_specs=[scan_spec] * 3,
            out_shape=[jax.ShapeDtypeStruct((n_grow, n_tok), F32)] * 3,
            compiler_params=_params(("arbitrary",)),
            name="gates",
        )(gt)

        ya_ctx, states = _mlstm(z, scans, 0, n_ctx, ctx_len, prev=states)
        ya_lat, _ = _mlstm(z, scans, n_ctx_tok, n_lat, lat_len,
                           init=(state_C, state_n, state_m), layer=l)

        n_chunks = n_tok // CHUNK
        zcol = lambda col: pl.BlockSpec((CHUNK, d), lambda i: (i, col))
        full = lambda shape: pl.BlockSpec(shape, lambda i: (0,) * len(shape))
        yb, yc = pl.pallas_call(
            functools.partial(_mixers_kernel, n_ctx_tok=n_ctx_tok, ctx_len=ctx_len, lat_len=lat_len),
            grid=(n_chunks,),
            in_specs=[
                pl.BlockSpec((CHUNK, d), lambda i: (jnp.maximum(i - 1, 0), COL_P)),
                zcol(COL_P),
                pl.BlockSpec((CHUNK, d), lambda i: (jnp.minimum(i + 1, n_chunks - 1), COL_P)),
                zcol(COL_U), zcol(COL_VG),
                full((N_GROUPS, GROUP_DIM, GROUP_DIM)), full((1, d)), full((1, d)),
                full((N_GROUPS, CHUNK, CHUNK)), full((CHUNK, N_GROUPS)),
            ],
            out_specs=[pl.BlockSpec((CHUNK, d), lambda i: (i, 0))] * 2,
            out_shape=[jax.ShapeDtypeStruct((n_tok, d), BF16)] * 2,
            compiler_params=_params(("arbitrary",)),
            name="mixers",
        )(z, z, z, z, z, pool_w[l].astype(BF16), vec(pool_scale[l]), vec(gmlp_norm_w[l]),
          gmlp_ws[l].astype(BF16), gmlp_b[l].T)

        tm = TOK_TILE
        tok = lambda: pl.BlockSpec((tm, d), lambda i: (i, 0))
        x = pl.pallas_call(
            functools.partial(_merge_kernel, n_x=len(x_parts), n_first=n_ctx_tok // tm),
            grid=(n_tok // tm,),
            in_specs=tok_specs(len(x_parts), tm) + tok_specs(2, tm) + [
                tok(), tok(),
                pl.BlockSpec((tm, d), lambda i: (i, COL_M)),
                pl.BlockSpec((tm, d), lambda i: (i, COL_M + 1)),
                pl.BlockSpec((tm, d), lambda i: (i, COL_M + 2)),
                pl.BlockSpec((1, N_MOD, d), lambda i: (row_tok(i), 0, 0)),
                full((1, d)), full((3, d, d)), full((d, d)),
            ],
            out_specs=tok(),
            out_shape=jax.ShapeDtypeStruct((n_tok, d), F32),
            compiler_params=_params(("arbitrary",)),
            name="merge",
        )(*x_parts, ya_ctx, ya_lat, yb, yc, z, z, z, mod_l, vec(nw[1]),
          w_br[l].astype(BF16), w_out[l].astype(BF16))
        x_parts = (x,)

        tf = FFN_TILE
        rows_per_tile = tf // GRID_W
        n_rows = n_tok // GRID_W
        n_ctx_tiles = n_ctx_tok // tf
        if last_layer:
            out_specs = [pl.BlockSpec((tf, d), lambda i: (jnp.minimum(i, n_ctx_tiles - 1), 0)),
                         pl.BlockSpec((tf, d), lambda i: (jnp.maximum(i - n_ctx_tiles, 0), 0))]
            out_shape = [jax.ShapeDtypeStruct((n_ctx_tok, d), F32),
                         jax.ShapeDtypeStruct((n_lat_tok, d), F32)]
        else:
            out_specs = [pl.BlockSpec((tf, d), lambda i: (i, 0))]
            out_shape = [jax.ShapeDtypeStruct((n_tok, d), F32)]
        x_parts = tuple(pl.pallas_call(
            functools.partial(_ffn_kernel, n_ctx_tok=n_ctx_tok, ctx_len=ctx_len, lat_len=lat_len),
            grid=(n_tok // tf,),
            in_specs=[
                pl.BlockSpec((GRID_W, d), lambda i: (jnp.maximum(i * rows_per_tile - 1, 0), 0)),
                pl.BlockSpec((tf, d), lambda i: (i, 0)),
                pl.BlockSpec((GRID_W, d),
                             lambda i: (jnp.minimum((i + 1) * rows_per_tile, n_rows - 1), 0)),
                pl.BlockSpec((1, N_MOD, d), lambda i: (row_ffn(i), 0, 0)),
                full((1, d)), full((1, d)), full((d, 2 * D_FF)),
                full((9, D_FF)), full((1, D_FF)), full((D_FF, d)),
            ],
            out_specs=out_specs,
            out_shape=out_shape,
            scratch_shapes=[pltpu.VMEM((tf, d), F32)],
            compiler_params=_params(("arbitrary",)),
            name="ffn",
        )(x, x, x, mod_l, vec(nw[2]), vec(nw[3]), w_up[l].astype(BF16),
          conv_w[l].reshape(9, D_FF), vec(conv_b[l]), w_down[l].astype(BF16)))

    new_c, new_n, new_m = states
    return (x_parts[0].reshape(n_ctx, ctx_len, d), x_parts[1].reshape(n_lat, lat_len, d), new_c,
            new_n.reshape(n_ctx, depth, 2, N_HEADS, HEAD_DIM), new_m.reshape(n_ctx, depth, 2, N_HEADS))
```

```python
import functools

import jax
import jax.numpy as jnp
from jax import lax
from jax.experimental import pallas as pl
from jax.experimental.pallas import tpu as pltpu

F32 = jnp.float32
BF16 = jnp.bfloat16

D_MODEL = 1024
N_HEADS = 4
HEAD_DIM = D_MODEL // N_HEADS
CHUNK = 128
POOL_WINDOWS = (2, 4, 8, 16)
N_GROUPS = 4
GROUP_DIM = D_MODEL // N_GROUPS
D_FF = 2816
GRID_W = 64
N_MOD = 6
RMS_EPS = 1e-6
LANES = 128
SUBLANES = 8
VMEM_LIMIT = 56 * 1024 * 1024

COL_Q, COL_K, COL_V, COL_O, COL_P, COL_U, COL_VG, COL_M = 0, 1, 2, 3, 4, 5, 6, 7
MAIN_WIDTH = 10 * D_MODEL

NT_DIMS = (((1,), (1,)), ((), ()))


def _rms(x, w):
    return x * lax.rsqrt(jnp.mean(x * x, axis=-1, keepdims=True) + RMS_EPS) * w


def _log_sigmoid(x):
    return jnp.minimum(x, 0.0) - jnp.log1p(jnp.exp(-jnp.abs(x)))


def _params(sem):
    return pltpu.CompilerParams(dimension_semantics=sem, vmem_limit_bytes=VMEM_LIMIT)


def _mod_kernel(c_ref, w_ref, b_ref, o_ref):
    c = c_ref[...]
    s = (c * jax.nn.sigmoid(c)).astype(BF16)
    o_ref[...] = jnp.dot(s, w_ref[...].astype(BF16), preferred_element_type=F32) + b_ref[...]


def _modulation(cvec, w_mod, b_mod):
    depth = w_mod.shape[0]
    n = N_MOD * D_MODEL
    tn = 1536
    out = pl.pallas_call(
        _mod_kernel,
        grid=(depth, n // tn),
        in_specs=[
            pl.BlockSpec((SUBLANES, D_MODEL), lambda l, j: (0, 0)),
            pl.BlockSpec((None, D_MODEL, tn), lambda l, j: (l, 0, j)),
            pl.BlockSpec((None, 1, tn), lambda l, j: (l, 0, j)),
        ],
        out_specs=pl.BlockSpec((None, SUBLANES, tn), lambda l, j: (l, 0, j)),
        out_shape=jax.ShapeDtypeStruct((depth, SUBLANES, n), F32),
        compiler_params=_params(("arbitrary", "arbitrary")),
        name="modulation",
    )(cvec, w_mod, b_mod.reshape(depth, 1, n))
    return out.reshape(depth, SUBLANES, N_MOD, D_MODEL)


def _modnorm(x, mod_ref, nw_ref, k_shift, k_scale):
    h = _rms(x, nw_ref[...])
    return h * (1.0 + mod_ref[0, k_scale:k_scale + 1, :]) + mod_ref[0, k_shift:k_shift + 1, :]


def _pick(refs, first):
    if len(refs) == 1:
        return refs[0][...]
    return jnp.where(first, refs[0][...], refs[1][...])


def _inproj_kernel(*refs, n_x, n_first, n_wa):
    x_refs = refs[:n_x]
    (mod_ref, nw_ref, wa_ref, ba_ref, wb_ref, bb_ref, wgt_ref, bgta_ref, bgtb_ref,
     z_ref, gt_ref, h_scr) = refs[n_x:]
    j = pl.program_id(1)

    @pl.when(j == 0)
    def _():
        x = _pick(x_refs, pl.program_id(0) < n_first)
        hb = _modnorm(x, mod_ref, nw_ref, 0, 1).astype(BF16)
        h_scr[...] = hb
        gt_ref[...] = (lax.dot_general(wgt_ref[...], hb, NT_DIMS, preferred_element_type=F32)
                       + bgta_ref[...] + bgtb_ref[...])

    @pl.when(j < n_wa)
    def _():
        z = jnp.dot(h_scr[...], wa_ref[...], preferred_element_type=F32) + ba_ref[...]
        z_ref[...] = z.astype(z_ref.dtype)

    @pl.when(j >= n_wa)
    def _():
        z = jnp.dot(h_scr[...], wb_ref[...], preferred_element_type=F32) + bb_ref[...]
        z_ref[...] = z.astype(z_ref.dtype)


def _gates_kernel(g_ref, a_ref, b_ref, cm_ref):
    g = g_ref[...]
    fwd = lax.broadcasted_iota(jnp.int32, g.shape, 0) % SUBLANES == 0
    lane = lax.broadcasted_iota(jnp.int32, g.shape, 1) % CHUNK
    lf = _log_sigmoid(pltpu.roll(g, g.shape[0] - 2, 0))
    pre, suf = _chunk_scans(lf, jnp.add, 0.0, lane)
    b = jnp.where(fwd, pre, suf)
    a = g - b
    pre, suf = _chunk_scans(a, jnp.maximum, -jnp.inf, lane)
    a_ref[...] = a
    b_ref[...] = b
    cm_ref[...] = jnp.where(fwd, pre, suf)


def _chunk_scans(x, op, fill, lane):
    width = x.shape[1]
    fwd = bwd = x
    k = 1
    while k < CHUNK:
        fwd = op(fwd, jnp.where(lane >= k, pltpu.roll(fwd, k, 1), fill))
        bwd = op(bwd, jnp.where(lane < CHUNK - k, pltpu.roll(bwd, width - k, 1), fill))
        k *= 2
    return fwd, bwd


def _mlstm_kernel(*refs, seq_len, has_init, emit_state, n_prev):
    q_ref, k_ref, v_ref, o_ref, a_ref, b_ref, cm_ref = refs[:7]
    pos = 7
    if has_init:
        c0_ref, n0_ref, m0_ref = refs[pos:pos + 3]
        pos += 3
    if n_prev:
        cprev_ref, nprev_ref, mprev_ref = refs[pos:pos + 3]
        pos += 3
    ya_ref = refs[pos]
    pos += 1
    if emit_state:
        cout_ref, nout_ref, mout_ref = refs[pos:pos + 3]
        pos += 3
    rows_scr, wk16_scr, dec_scr, vt_scr, vtw_scr, cb_scr, n1_scr, c_scr = refs[pos:pos + 8]

    n_chunks = seq_len // CHUNK
    scale = HEAD_DIM ** -0.5
    row81 = lax.broadcasted_iota(jnp.int32, (SUBLANES, 1), 0)
    is_fwd1 = row81 == 0
    row_n = lax.broadcasted_iota(jnp.int32, (SUBLANES, HEAD_DIM), 0)

    chunk = lambda ref, c: ref[:, c * CHUNK:(c + 1) * CHUNK]
    a_c = [chunk(a_ref, c) for c in range(n_chunks)]
    b_c = [chunk(b_ref, c) for c in range(n_chunks)]
    cm_c = [chunk(cm_ref, c) for c in range(n_chunks)]
    row8 = lax.broadcasted_iota(jnp.int32, (SUBLANES, CHUNK), 0)
    lane8 = lax.broadcasted_iota(jnp.int32, (SUBLANES, CHUNK), 1)
    last = lane8 == jnp.where(row8 == 0, CHUNK - 1, 0)
    amax_c = [jnp.max(a, axis=1, keepdims=True) for a in a_c]
    bend_c = [jnp.sum(jnp.where(last, b, 0.0), axis=1, keepdims=True) for b in b_c]

    if has_init:
        m_state = jnp.where(is_fwd1, m0_ref[0], jnp.where(row81 == 1, m0_ref[1], 0.0))
        n_init = jnp.where(row_n == 0, n0_ref[0], n0_ref[1])
        c_scr[...] = c0_ref[...]
    else:
        m_state = jnp.zeros((SUBLANES, 1), F32)
        n_init = jnp.zeros((SUBLANES, HEAD_DIM), F32)
        c_scr[...] = jnp.zeros(c_scr.shape, F32)

    m_before, top = [], []
    for j in range(n_chunks):
        jr = n_chunks - 1 - j
        amax = jnp.where(is_fwd1, amax_c[j], amax_c[jr])
        bend = jnp.where(is_fwd1, bend_c[j], bend_c[jr])
        m_before.append(m_state)
        top.append(jnp.maximum(m_state, amax))
        m_state = bend + top[-1]

    for c in range(n_chunks):
        cr = n_chunks - 1 - c
        m_c = jnp.where(is_fwd1, m_before[c], m_before[cr])
        top_c = jnp.where(is_fwd1, top[c], top[cr])
        big_m = jnp.maximum(m_c, cm_c[c])
        w_k = jnp.exp(a_c[c] - top_c)
        rows_scr[c, 0] = a_c[c]
        rows_scr[c, 1] = big_m
        rows_scr[c, 2] = jnp.exp(m_c - big_m)
        rows_scr[c, 3] = jnp.exp(-b_c[c] - big_m)
        rows_scr[c, 4] = w_k
        wk16_scr[c] = jnp.concatenate([w_k, jnp.zeros_like(w_k)], axis=0).astype(BF16)
        dec_scr[c] = jnp.broadcast_to(jnp.exp(m_c - top_c), (SUBLANES, HEAD_DIM))

    def load_k(r0):
        return (k_ref[pl.ds(r0, CHUNK), :].astype(F32) * scale).astype(BF16)

    def pass_a(j, nst):
        c = n_chunks - 1 - j
        r0 = pl.multiple_of(c * CHUNK, CHUNK)
        ks = load_k(r0)
        vt = v_ref[pl.ds(r0, CHUNK), :].astype(F32).T
        w_k = rows_scr[c, 4]
        vt_scr[c] = vt.astype(BF16)
        vtw_scr[c] = (vt * w_k[0:1, :]).astype(BF16)
        c_in = c_scr[1]
        cb_scr[c] = c_in.astype(BF16)
        n1_scr[c] = nst
        dec = dec_scr[c]
        c_scr[1] = dec[1:2, :] * c_in + jnp.dot((vt * w_k[1:2, :]).astype(BF16), ks,
                                                preferred_element_type=F32)
        n_upd = jnp.dot(wk16_scr[c], ks, preferred_element_type=F32)[:SUBLANES]
        return dec * nst + n_upd

    n_rev = lax.fori_loop(0, n_chunks, pass_a, n_init, unroll=2)

    row_id = lax.broadcasted_iota(jnp.int32, (CHUNK, CHUNK), 0)
    col_id = lax.broadcasted_iota(jnp.int32, (CHUNK, CHUNK), 1)
    tri = (row_id <= col_id, row_id >= col_id)

    def pass_b(c, nst):
        r0 = pl.multiple_of(c * CHUNK, CHUNK)
        q = q_ref[pl.ds(r0, CHUNK), :]
        ks = load_k(r0)
        vt = vt_scr[c]
        a_r, bigm_r, winter_r, floor_r = (rows_scr[c, i] for i in range(4))
        st_all = lax.dot_general(ks, q, NT_DIMS, preferred_element_type=F32)
        n_rows = jnp.where(row_n == 0, nst, n1_scr[c])
        n_rows = jnp.concatenate([n_rows, jnp.zeros_like(n_rows)], axis=0).astype(BF16)
        qn = lax.dot_general(n_rows, q, NT_DIMS, preferred_element_type=F32)
        c0 = c_scr[0]
        ht = None
        for d in range(2):
            row = slice(d, d + 1)
            a_st = jnp.broadcast_to(a_r[row, :], (CHUNK, CHUNK)).T
            st = st_all * jnp.exp(jnp.where(tri[d], a_st - bigm_r[row, :], -jnp.inf))
            den = jnp.sum(st, axis=0, keepdims=True) + winter_r[row, :] * qn[row, :]
            inv = 1.0 / jnp.maximum(jnp.abs(den), floor_r[row, :])
            cb = c0.astype(BF16) if d == 0 else cb_scr[c]
            num_t = (jnp.dot(vt, st.astype(BF16), preferred_element_type=F32)
                     + winter_r[row, :] * lax.dot_general(cb, q, NT_DIMS,
                                                          preferred_element_type=F32))
            ht = num_t * inv if ht is None else ht + num_t * inv
        gate = jax.nn.sigmoid(o_ref[pl.ds(r0, CHUNK), :].astype(F32))
        ya_ref[pl.ds(r0, CHUNK), :] = (gate * ht.T).astype(ya_ref.dtype)

        dec = dec_scr[c]
        c_scr[0] = dec[0:1, :] * c0 + jnp.dot(vtw_scr[c], ks, preferred_element_type=F32)
        n_upd = jnp.dot(wk16_scr[c], ks, preferred_element_type=F32)[:SUBLANES]
        return dec * nst + n_upd

    n_fwd = lax.fori_loop(0, n_chunks, pass_b, n_init, unroll=2)

    if emit_state:
        if n_prev:
            cout_ref[:n_prev] = cprev_ref[...]
            nout_ref[:n_prev] = nprev_ref[...]
            mout_ref[:n_prev] = mprev_ref[...]
        cout_ref[n_prev] = c_scr[...]
        nout_ref[n_prev, 0] = n_fwd[0:1, :]
        nout_ref[n_prev, 1] = n_rev[1:2, :]
        mout_ref[n_prev, 0] = m_state[0:1, :]
        mout_ref[n_prev, 1] = m_state[1:2, :]


def _mlstm(z, scans, tok0, n_batch, seq_len, init=None, layer=0, prev=None):
    blk0 = tok0 // seq_len
    n_chunks = seq_len // CHUNK
    has_init = init is not None
    emit_state = not has_init
    n_prev = 0 if prev is None else prev[0].shape[1]

    def zspec(col):
        return pl.BlockSpec((seq_len, HEAD_DIM),
                            lambda b, h: (blk0 + b, col * N_HEADS + h))

    scan_spec = pl.BlockSpec((SUBLANES, seq_len), lambda b, h: (h, blk0 + b))
    in_specs = [zspec(COL_Q), zspec(COL_K), zspec(COL_V), zspec(COL_O),
                scan_spec, scan_spec, scan_spec]
    args = [z, z, z, z, *scans]

    def state_specs(n_layers):
        return [
            pl.BlockSpec((None, n_layers, 2, None, HEAD_DIM, HEAD_DIM),
                         lambda b, h: (b, 0, 0, h, 0, 0)),
            pl.BlockSpec((None, n_layers, 2, None, 1, HEAD_DIM), lambda b, h: (b, 0, 0, h, 0, 0)),
            pl.BlockSpec((None, n_layers, 2, None, 1, 1), lambda b, h: (b, 0, 0, h, 0, 0)),
        ]

    if has_init:
        sc, sn, sm = init
        nb, depth = sc.shape[:2]
        in_specs += [
            pl.BlockSpec((None, None, 2, None, HEAD_DIM, HEAD_DIM),
                         lambda b, h: (b, layer, 0, h, 0, 0)),
            pl.BlockSpec((None, None, 2, None, 1, HEAD_DIM), lambda b, h: (b, layer, 0, h, 0, 0)),
            pl.BlockSpec((None, None, 2, None, 1, 1), lambda b, h: (b, layer, 0, h, 0, 0)),
        ]
        args += [sc, sn.reshape(nb, depth, 2, N_HEADS, 1, HEAD_DIM),
                 sm.reshape(nb, depth, 2, N_HEADS, 1, 1)]
    if n_prev:
        in_specs += state_specs(n_prev)
        args += list(prev)
    out_specs = [pl.BlockSpec((seq_len, HEAD_DIM), lambda b, h: (b, h))]
    out_shape = [jax.ShapeDtypeStruct((n_batch * seq_len, D_MODEL), BF16)]
    if emit_state:
        n_out = n_prev + 1
        out_specs += state_specs(n_out)
        out_shape += [
            jax.ShapeDtypeStruct((n_batch, n_out, 2, N_HEADS, HEAD_DIM, HEAD_DIM), F32),
            jax.ShapeDtypeStruct((n_batch, n_out, 2, N_HEADS, 1, HEAD_DIM), F32),
            jax.ShapeDtypeStruct((n_batch, n_out, 2, N_HEADS, 1, 1), F32),
        ]
    outs = pl.pallas_call(
        functools.partial(_mlstm_kernel, seq_len=seq_len, has_init=has_init, emit_state=emit_state,
                          n_prev=n_prev),
        grid=(n_batch, N_HEADS),
        in_specs=in_specs,
        out_specs=out_specs,
        out_shape=out_shape,
        scratch_shapes=[
            pltpu.VMEM((n_chunks, 5, SUBLANES, CHUNK), F32),
            pltpu.VMEM((n_chunks, 2 * SUBLANES, CHUNK), BF16),
            pltpu.VMEM((n_chunks, SUBLANES, HEAD_DIM), F32),
            pltpu.VMEM((n_chunks, HEAD_DIM, CHUNK), BF16),
            pltpu.VMEM((n_chunks, HEAD_DIM, CHUNK), BF16),
            pltpu.VMEM((n_chunks, HEAD_DIM, HEAD_DIM), BF16),
            pltpu.VMEM((n_chunks, SUBLANES, HEAD_DIM), F32),
            pltpu.VMEM((2, HEAD_DIM, HEAD_DIM), F32),
        ],
        compiler_params=_params(("arbitrary", "arbitrary")),
        name="mlstm_ctx" if emit_state else "mlstm_lat",
    )(*args)
    if emit_state:
        return outs[0], tuple(outs[1:])
    return outs[0], None


TOK_TILE = 256
POOL_HALO = 64


def _mix_merge_kernel(*refs, n_x, n_first, n_ctx_tok, ctx_len, lat_len):
    x_refs = refs[:n_x]
    (ya_ctx_ref, ya_lat_ref, pp_ref, pc_ref, pn_ref, u_ref, vg_ref, m0_ref, m1_ref, m2_ref,
     mod_ref, nw_ref, pw_ref, ps_ref, gnw_ref, ws_ref, gb_ref, wbr_ref, wout_ref, o_ref) = refs[n_x:]
    tok0 = pl.program_id(0) * TOK_TILE
    first = pl.program_id(0) < n_first
    is_ctx = tok0 < n_ctx_tok
    seq_len = jnp.where(is_ctx, ctx_len, lat_len)
    base = jnp.where(is_ctx, 0, n_ctx_tok)
    seq_start = base + ((tok0 - base) // seq_len) * seq_len
    seq_end = seq_start + seq_len

    k_rows = TOK_TILE + 2 * POOL_HALO
    t_abs = tok0 + lax.broadcasted_iota(jnp.int32, (TOK_TILE, k_rows), 0)
    u_abs = tok0 - POOL_HALO + lax.broadcasted_iota(jnp.int32, (TOK_TILE, k_rows), 1)
    in_seq = (u_abs >= seq_start) & (u_abs < seq_end)
    t_col = tok0 + lax.broadcasted_iota(jnp.int32, (TOK_TILE, 1), 0)
    p_ext = jnp.concatenate([pp_ref[...], pc_ref[...], pn_ref[...]], axis=0)
    yb = []
    for g, win in enumerate(POOL_WINDOWS):
        half = win // 2
        sl = slice(g * GROUP_DIM, (g + 1) * GROUP_DIM)
        band = (u_abs >= t_abs - half) & (u_abs < t_abs + half) & in_seq
        acc = jnp.dot(jnp.where(band, 1.0, 0.0).astype(BF16), p_ext[:, sl],
                      preferred_element_type=F32)
        count = (jnp.minimum(t_col + half, seq_end) - jnp.maximum(t_col - half, seq_start)).astype(F32)
        y = (acc / count - pc_ref[:, sl].astype(F32)).astype(BF16)
        yb.append(jnp.dot(y, pw_ref[g], preferred_element_type=F32) * ps_ref[:, sl])
    yb = jnp.concatenate(yb, axis=1).astype(BF16)

    u = jax.nn.gelu(u_ref[...].astype(F32))
    vg = _rms(jax.nn.gelu(vg_ref[...].astype(F32)), gnw_ref[...]).astype(BF16)
    yc = []
    for ch in range(TOK_TILE // CHUNK):
        rows = slice(ch * CHUNK, (ch + 1) * CHUNK)
        parts = []
        for g in range(N_GROUPS):
            sl = slice(g * GROUP_DIM, (g + 1) * GROUP_DIM)
            s = jnp.dot(ws_ref[g], vg[rows, sl], preferred_element_type=F32) + gb_ref[:, g:g + 1]
            parts.append(u[rows, sl] * s)
        yc.append(jnp.concatenate(parts, axis=1))
    yc = jnp.concatenate(yc, axis=0).astype(BF16)

    ys = (_pick((ya_ctx_ref, ya_lat_ref), first), yb, yc)
    mixed = None
    for n, (y, m_ref) in enumerate(zip(ys, (m0_ref, m1_ref, m2_ref))):
        br = jnp.dot(y, wbr_ref[n], preferred_element_type=F32)
        term = jax.nn.sigmoid(m_ref[...].astype(F32)) * br
        mixed = term if mixed is None else mixed + term
    o = jnp.dot(mixed.astype(BF16), wout_ref[...], preferred_element_type=F32)
    o_ref[...] = _pick(x_refs, first) + mod_ref[0, 2:3, :] * _rms(o, nw_ref[...])


FFN_TILE = 512
FFN_SLAB = 256


def _ffn_kernel(xp_ref, xc_ref, xn_ref, mod_ref, nw2_ref, nw3_ref, wup_ref, cw_ref, cb_ref, wd_ref,
                *rest, n_ctx_tok, ctx_len, lat_len):
    *out_refs, f_scr = rest
    i = pl.program_id(0)
    tok0 = i * FFN_TILE
    is_ctx = tok0 < n_ctx_tok
    tiles_per_lat = lat_len // FFN_TILE
    lat_tile = jnp.maximum(tok0 - n_ctx_tok, 0) // FFN_TILE % tiles_per_lat
    has_up = jnp.logical_and(jnp.logical_not(is_ctx), lat_tile != 0)
    has_dn = jnp.logical_and(jnp.logical_not(is_ctx), lat_tile != tiles_per_lat - 1)
    ext_rows = FFN_TILE + 2 * GRID_W
    r = lax.broadcasted_iota(jnp.int32, (ext_rows, FFN_SLAB), 0)
    col_ctx = (r - GRID_W + ctx_len) % ctx_len
    col_lat = r % GRID_W
    keep_left = jnp.where(jnp.where(is_ctx, col_ctx, col_lat) != 0, 1.0, 0.0)
    keep_right = jnp.where(jnp.where(is_ctx, col_ctx - (ctx_len - 1), col_lat - (GRID_W - 1)) != 0,
                           1.0, 0.0)
    r1 = lax.broadcasted_iota(jnp.int32, (ext_rows, 1), 0)
    in_grid = jnp.logical_and(jnp.logical_or(r1 >= GRID_W, has_up),
                              jnp.logical_or(r1 < GRID_W + FFN_TILE, has_dn))
    tap_on = jnp.where(lax.broadcasted_iota(jnp.int32, (9, 1), 0) // 3 == 1, 1.0,
                       jnp.where(is_ctx, 0.0, 1.0))
    cw = cw_ref[...] * tap_on

    x_ext = jnp.concatenate([xp_ref[...], xc_ref[...], xn_ref[...]], axis=0)
    h_ext = jnp.where(in_grid, _modnorm(x_ext, mod_ref, nw2_ref, 3, 4), 0.0).astype(BF16)
    h_cur = h_ext[GRID_W:GRID_W + FFN_TILE]

    def up(c):
        sl = slice(c * FFN_SLAB, (c + 1) * FFN_SLAB)
        sl_gate = slice(D_FF + c * FFN_SLAB, D_FF + (c + 1) * FFN_SLAB)
        return (jnp.dot(h_ext, wup_ref[:, sl], preferred_element_type=F32),
                jnp.dot(h_cur, wup_ref[:, sl_gate], preferred_element_type=F32))

    n_slabs = D_FF // FFN_SLAB
    nxt = up(0)
    for c in range(n_slabs):
        sl = slice(c * FFN_SLAB, (c + 1) * FFN_SLAB)
        ext, gate = nxt
        if c + 1 < n_slabs:
            nxt = up(c + 1)
        left = pltpu.roll(ext, 1, 0) * keep_left
        right = pltpu.roll(ext, ext_rows - 1, 0) * keep_right
        acc = jnp.zeros((FFN_TILE, FFN_SLAB), F32) + cb_ref[:, sl]
        for dy in range(3):
            rows = slice(dy * GRID_W, dy * GRID_W + FFN_TILE)
            for dx, src in enumerate((left, ext, right)):
                acc = acc + cw[dy * 3 + dx:dy * 3 + dx + 1, sl] * src[rows, :]
        u = (jax.nn.gelu(acc) * gate).astype(BF16)
        part = jnp.dot(u, wd_ref[sl, :], preferred_element_type=F32)
        if c == 0:
            f_scr[...] = part
        else:
            f_scr[...] += part

    y = xc_ref[...] + mod_ref[0, 5:6, :] * _rms(f_scr[...], nw3_ref[...])
    if len(out_refs) == 1:
        out_refs[0][...] = y
    else:
        @pl.when(is_ctx)
        def _():
            out_refs[0][...] = y

        @pl.when(jnp.logical_not(is_ctx))
        def _():
            out_refs[1][...] = y


def _gate_rows(w_gate):
    idx, keep = [], []
    for p in range(N_HEADS * SUBLANES):
        h, j = divmod(p, SUBLANES)
        d, kind = j % 2, j // 2
        idx.append(d * 2 * N_HEADS + kind * N_HEADS + h if j < 4 else 0)
        keep.append(1.0 if j < 4 else 0.0)
    return w_gate[..., jnp.array(idx)] * jnp.array(keep, w_gate.dtype)


def kernel(x_prompt, x_sample, state_C, state_n, state_m, c, c_ctx, w_mod, b_mod, norm_w, w_in, b_in,
           mlstm_gate_b, pool_w, pool_scale, gmlp_norm_w, gmlp_ws, gmlp_b, w_br, w_out, w_up, conv_w,
           conv_b, w_down):
    n_ctx, ctx_len, d = x_prompt.shape
    n_lat, lat_len, _ = x_sample.shape
    depth = w_in.shape[0]
    n_ctx_tok = n_ctx * ctx_len
    n_lat_tok = n_lat * lat_len
    n_tok = n_ctx_tok + n_lat_tok
    assert d == D_MODEL and FFN_TILE % ctx_len == 0 and lat_len % FFN_TILE == 0
    assert n_ctx_tok % FFN_TILE == 0 and ctx_len % CHUNK == 0
    n_gate = 2 * 2 * N_HEADS
    gate0 = 4 * D_MODEL

    x_parts = (x_prompt.reshape(n_ctx_tok, d), x_sample.reshape(n_lat_tok, d))

    def tok_specs(n_parts, tm):
        if n_parts == 1:
            return [pl.BlockSpec((tm, d), lambda i, *_: (i, 0))]
        n_a = n_ctx_tok // tm
        return [pl.BlockSpec((tm, d), lambda i, *_: (jnp.minimum(i, n_a - 1), 0)),
                pl.BlockSpec((tm, d), lambda i, *_: (jnp.maximum(i - n_a, 0), 0))]

    cvec = jnp.zeros((SUBLANES, d), F32).at[0].set(c_ctx).at[1:1 + n_lat].set(c)
    mod = _modulation(cvec, w_mod, b_mod)

    def mod_spec(tm):
        def row(i):
            t0 = i * tm
            return jnp.where(t0 < n_ctx_tok, 0, 1 + jnp.maximum(t0 - n_ctx_tok, 0) // lat_len)
        return row

    tm_proj = 1024
    tn_in = 2048
    w_in16 = w_in.astype(BF16)
    row_in = mod_spec(tm_proj)
    row_tok = mod_spec(TOK_TILE)
    row_ffn = mod_spec(FFN_TILE)
    vec = lambda a: a.reshape(1, -1)
    states = None
    n_grow = N_HEADS * SUBLANES
    n_wa = gate0 // tn_in
    n_wb = (MAIN_WIDTH - gate0) // tn_in
    scan_lanes = 2048

    for l in range(depth):
        last_layer = l == depth - 1
        w_b = w_in16[l, :, gate0 + n_gate:]
        wg_rows = _gate_rows(w_in[l, :, gate0:gate0 + n_gate]).T.astype(BF16)
        bga_rows = _gate_rows(b_in[l, gate0:gate0 + n_gate]).reshape(-1, 1)
        bgb_rows = _gate_rows(mlstm_gate_b[l].reshape(n_gate)).reshape(-1, 1)
        mod_l = mod[l]
        nw = norm_w[l]

        z, gt = pl.pallas_call(
            functools.partial(_inproj_kernel, n_x=len(x_parts), n_first=n_ctx_tok // tm_proj, n_wa=n_wa),
            grid=(n_tok // tm_proj, n_wa + n_wb),
            in_specs=tok_specs(len(x_parts), tm_proj) + [
                pl.BlockSpec((1, N_MOD, d), lambda i, j: (row_in(i), 0, 0)),
                pl.BlockSpec((1, d), lambda i, j: (0, 0)),
                pl.BlockSpec((None, d, tn_in), lambda i, j, l=l: (l, 0, jnp.minimum(j, n_wa - 1))),
                pl.BlockSpec((1, tn_in), lambda i, j: (0, jnp.minimum(j, n_wa - 1))),
                pl.BlockSpec((d, tn_in), lambda i, j: (0, jnp.maximum(j - n_wa, 0))),
                pl.BlockSpec((1, tn_in), lambda i, j: (0, jnp.maximum(j - n_wa, 0))),
                pl.BlockSpec((n_grow, d), lambda i, j: (0, 0)),
                pl.BlockSpec((n_grow, 1), lambda i, j: (0, 0)),
                pl.BlockSpec((n_grow, 1), lambda i, j: (0, 0)),
            ],
            out_specs=[
                pl.BlockSpec((tm_proj, tn_in), lambda i, j: (i, j)),
                pl.BlockSpec((n_grow, tm_proj), lambda i, j: (0, i)),
            ],
            out_shape=[
                jax.ShapeDtypeStruct((n_tok, MAIN_WIDTH), BF16),
                jax.ShapeDtypeStruct((n_grow, n_tok), F32),
            ],
            scratch_shapes=[pltpu.VMEM((tm_proj, d), BF16)],
            compiler_params=_params(("arbitrary", "arbitrary")),
            name="inproj",
        )(*x_parts, mod_l, vec(nw[0]), w_in16, vec(b_in[l, :gate0]), w_b, vec(b_in[l, gate0 + n_gate:]),
          wg_rows, bga_rows, bgb_rows)

        scan_spec = pl.BlockSpec((n_grow, scan_lanes), lambda i: (0, i))
        scans = pl.pallas_call(
            _gates_kernel,
            grid=(n_tok // scan_lanes,),
            in_specs=[scan_spec],
            out_specs=[scan_spec] * 3,
            out_shape=[jax.ShapeDtypeStruct((n_grow, n_tok), F32)] * 3,
            compiler_params=_params(("arbitrary",)),
            name="gates",
        )(gt)

        ya_ctx, states = _mlstm(z, scans, 0, n_ctx, ctx_len, prev=states)
        ya_lat, _ = _mlstm(z, scans, n_ctx_tok, n_lat, lat_len,
                           init=(state_C, state_n, state_m), layer=l)

        tm = TOK_TILE
        halo_per_tile = tm // POOL_HALO
        n_halo = n_tok // POOL_HALO
        zcol = lambda col: pl.BlockSpec((tm, d), lambda i: (i, col))
        full = lambda shape: pl.BlockSpec(shape, lambda i: (0,) * len(shape))
        x = pl.pallas_call(
            functools.partial(_mix_merge_kernel, n_x=len(x_parts), n_first=n_ctx_tok // tm,
                              n_ctx_tok=n_ctx_tok, ctx_len=ctx_len, lat_len=lat_len),
            grid=(n_tok // tm,),
            in_specs=tok_specs(len(x_parts), tm) + tok_specs(2, tm) + [
                pl.BlockSpec((POOL_HALO, d),
                             lambda i: (jnp.maximum(i * halo_per_tile - 1, 0), COL_P)),
                zcol(COL_P),
                pl.BlockSpec((POOL_HALO, d),
                             lambda i: (jnp.minimum((i + 1) * halo_per_tile, n_halo - 1), COL_P)),
                zcol(COL_U), zcol(COL_VG), zcol(COL_M), zcol(COL_M + 1), zcol(COL_M + 2),
                pl.BlockSpec((1, N_MOD, d), lambda i: (row_tok(i), 0, 0)),
                full((1, d)),
                full((N_GROUPS, GROUP_DIM, GROUP_DIM)), full((1, d)), full((1, d)),
                full((N_GROUPS, CHUNK, CHUNK)), full((CHUNK, N_GROUPS)),
                full((3, d, d)), full((d, d)),
            ],
            out_specs=pl.BlockSpec((tm, d), lambda i: (i, 0)),
            out_shape=jax.ShapeDtypeStruct((n_tok, d), F32),
            compiler_params=_params(("arbitrary",)),
            name="mix_merge",
        )(*x_parts, ya_ctx, ya_lat, z, z, z, z, z, z, z, z, mod_l, vec(nw[1]),
          pool_w[l].astype(BF16), vec(pool_scale[l]), vec(gmlp_norm_w[l]),
          gmlp_ws[l].astype(BF16), gmlp_b[l].T, w_br[l].astype(BF16), w_out[l].astype(BF16))
        x_parts = (x,)

        tf = FFN_TILE
        rows_per_tile = tf // GRID_W
        n_rows = n_tok // GRID_W
        n_ctx_tiles = n_ctx_tok // tf
        if last_layer:
            out_specs = [pl.BlockSpec((tf, d), lambda i: (jnp.minimum(i, n_ctx_tiles - 1), 0)),
                         pl.BlockSpec((tf, d), lambda i: (jnp.maximum(i - n_ctx_tiles, 0), 0))]
            out_shape = [jax.ShapeDtypeStruct((n_ctx_tok, d), F32),
                         jax.ShapeDtypeStruct((n_lat_tok, d), F32)]
        else:
            out_specs = [pl.BlockSpec((tf, d), lambda i: (i, 0))]
            out_shape = [jax.ShapeDtypeStruct((n_tok, d), F32)]
        x_parts = tuple(pl.pallas_call(
            functools.partial(_ffn_kernel, n_ctx_tok=n_ctx_tok, ctx_len=ctx_len, lat_len=lat_len),
            grid=(n_tok // tf,),
            in_specs=[
                pl.BlockSpec((GRID_W, d), lambda i: (jnp.maximum(i * rows_per_tile - 1, 0), 0)),
                pl.BlockSpec((tf, d), lambda i: (i, 0)),
                pl.BlockSpec((GRID_W, d),
                             lambda i: (jnp.minimum((i + 1) * rows_per_tile, n_rows - 1), 0)),
                pl.BlockSpec((1, N_MOD, d), lambda i: (row_ffn(i), 0, 0)),
                full((1, d)), full((1, d)), full((d, 2 * D_FF)),
                full((9, D_FF)), full((1, D_FF)), full((D_FF, d)),
            ],
            out_specs=out_specs,
            out_shape=out_shape,
            scratch_shapes=[pltpu.VMEM((tf, d), F32)],
            compiler_params=_params(("arbitrary",)),
            name="ffn",
        )(x, x, x, mod_l, vec(nw[2]), vec(nw[3]), w_up[l].astype(BF16),
          conv_w[l].reshape(9, D_FF), vec(conv_b[l]), w_down[l].astype(BF16)))

    new_c, new_n, new_m = states
    return (x_parts[0].reshape(n_ctx, ctx_len, d), x_parts[1].reshape(n_lat, lat_len, d), new_c,
            new_n.reshape(n_ctx, depth, 2, N_HEADS, HEAD_DIM), new_m.reshape(n_ctx, depth, 2, N_HEADS))
```

```python
import functools

import jax
import jax.numpy as jnp
from jax import lax
from jax.experimental import pallas as pl
from jax.experimental.pallas import tpu as pltpu

F32 = jnp.float32
BF16 = jnp.bfloat16

D_MODEL = 1024
N_HEADS = 4
HEAD_DIM = D_MODEL // N_HEADS
CHUNK = 128
POOL_WINDOWS = (2, 4, 8, 16)
N_GROUPS = 4
GROUP_DIM = D_MODEL // N_GROUPS
D_FF = 2816
GRID_W = 64
N_MOD = 6
RMS_EPS = 1e-6
LANES = 128
SUBLANES = 8
VMEM_LIMIT = 56 * 1024 * 1024

COL_Q, COL_K, COL_V, COL_O, COL_P, COL_U, COL_VG, COL_M = 0, 1, 2, 3, 4, 5, 6, 7
MAIN_WIDTH = 10 * D_MODEL

NT_DIMS = (((1,), (1,)), ((), ()))


def _rms(x, w):
    return x * lax.rsqrt(jnp.mean(x * x, axis=-1, keepdims=True) + RMS_EPS) * w


def _log_sigmoid(x):
    return jnp.minimum(x, 0.0) - jnp.log1p(jnp.exp(-jnp.abs(x)))


def _params(sem):
    return pltpu.CompilerParams(dimension_semantics=sem, vmem_limit_bytes=VMEM_LIMIT)


def _mod_kernel(c_ref, w_ref, b_ref, o_ref):
    c = c_ref[...]
    s = (c * jax.nn.sigmoid(c)).astype(BF16)
    o_ref[...] = jnp.dot(s, w_ref[...].astype(BF16), preferred_element_type=F32) + b_ref[...]


def _modulation(cvec, w_mod, b_mod):
    depth = w_mod.shape[0]
    n = N_MOD * D_MODEL
    tn = 1536
    out = pl.pallas_call(
        _mod_kernel,
        grid=(depth, n // tn),
        in_specs=[
            pl.BlockSpec((SUBLANES, D_MODEL), lambda l, j: (0, 0)),
            pl.BlockSpec((None, D_MODEL, tn), lambda l, j: (l, 0, j)),
            pl.BlockSpec((None, 1, tn), lambda l, j: (l, 0, j)),
        ],
        out_specs=pl.BlockSpec((None, SUBLANES, tn), lambda l, j: (l, 0, j)),
        out_shape=jax.ShapeDtypeStruct((depth, SUBLANES, n), F32),
        compiler_params=_params(("arbitrary", "arbitrary")),
        name="modulation",
    )(cvec, w_mod, b_mod.reshape(depth, 1, n))
    return out.reshape(depth, SUBLANES, N_MOD, D_MODEL)


def _modnorm(x, mod_ref, nw_ref, k_shift, k_scale):
    h = _rms(x, nw_ref[...])
    return h * (1.0 + mod_ref[0, k_scale:k_scale + 1, :]) + mod_ref[0, k_shift:k_shift + 1, :]


def _pick(refs, first):
    if len(refs) == 1:
        return refs[0][...]
    return jnp.where(first, refs[0][...], refs[1][...])


def _inproj_kernel(*refs, n_x, n_first, n_wa):
    x_refs = refs[:n_x]
    (mod_ref, nw_ref, wa_ref, ba_ref, wb_ref, bb_ref, wgt_ref, bgta_ref, bgtb_ref,
     z_ref, gt_ref, h_scr) = refs[n_x:]
    j = pl.program_id(1)

    @pl.when(j == 0)
    def _():
        x = _pick(x_refs, pl.program_id(0) < n_first)
        hb = _modnorm(x, mod_ref, nw_ref, 0, 1).astype(BF16)
        h_scr[...] = hb
        gt_ref[...] = (lax.dot_general(wgt_ref[...], hb, NT_DIMS, preferred_element_type=F32)
                       + bgta_ref[...] + bgtb_ref[...])

    @pl.when(j < n_wa)
    def _():
        z = jnp.dot(h_scr[...], wa_ref[...], preferred_element_type=F32) + ba_ref[...]
        z_ref[...] = z.astype(z_ref.dtype)

    @pl.when(j >= n_wa)
    def _():
        z = jnp.dot(h_scr[...], wb_ref[...], preferred_element_type=F32) + bb_ref[...]
        z_ref[...] = z.astype(z_ref.dtype)


def _gates_kernel(g_ref, a_ref, b_ref, cm_ref):
    g = g_ref[...]
    fwd = lax.broadcasted_iota(jnp.int32, g.shape, 0) % SUBLANES == 0
    lane = lax.broadcasted_iota(jnp.int32, g.shape, 1) % CHUNK
    lf = _log_sigmoid(pltpu.roll(g, g.shape[0] - 2, 0))
    pre, suf = _chunk_scans(lf, jnp.add, 0.0, lane)
    b = jnp.where(fwd, pre, suf)
    a = g - b
    pre, suf = _chunk_scans(a, jnp.maximum, -jnp.inf, lane)
    a_ref[...] = a
    b_ref[...] = b
    cm_ref[...] = jnp.where(fwd, pre, suf)


def _chunk_scans(x, op, fill, lane):
    width = x.shape[1]
    fwd = bwd = x
    k = 1
    while k < CHUNK:
        fwd = op(fwd, jnp.where(lane >= k, pltpu.roll(fwd, k, 1), fill))
        bwd = op(bwd, jnp.where(lane < CHUNK - k, pltpu.roll(bwd, width - k, 1), fill))
        k *= 2
    return fwd, bwd


def _mlstm_kernel(*refs, seq_len, has_init, emit_state, n_prev):
    q_ref, k_ref, v_ref, o_ref, a_ref, b_ref, cm_ref = refs[:7]
    pos = 7
    if has_init:
        c0_ref, n0_ref, m0_ref = refs[pos:pos + 3]
        pos += 3
    if n_prev:
        cprev_ref, nprev_ref, mprev_ref = refs[pos:pos + 3]
        pos += 3
    ya_ref = refs[pos]
    pos += 1
    if emit_state:
        cout_ref, nout_ref, mout_ref = refs[pos:pos + 3]
        pos += 3
    rows_scr, wk16_scr, dec_scr, vt_scr, vtw_scr, cb_scr, n1_scr, c_scr = refs[pos:pos + 8]

    n_chunks = seq_len // CHUNK
    unroll = min(4, n_chunks)
    scale = HEAD_DIM ** -0.5
    row81 =lax.broadcasted_iota(jnp.int32, (SUBLANES, 1), 0)
    is_fwd1 = row81 == 0
    row_n = lax.broadcasted_iota(jnp.int32, (SUBLANES, HEAD_DIM), 0)

    chunk = lambda ref, c: ref[:, c * CHUNK:(c + 1) * CHUNK]
    a_c = [chunk(a_ref, c) for c in range(n_chunks)]
    b_c = [chunk(b_ref, c) for c in range(n_chunks)]
    cm_c = [chunk(cm_ref, c) for c in range(n_chunks)]
    row8 = lax.broadcasted_iota(jnp.int32, (SUBLANES, CHUNK), 0)
    lane8 = lax.broadcasted_iota(jnp.int32, (SUBLANES, CHUNK), 1)
    last = lane8 == jnp.where(row8 == 0, CHUNK - 1, 0)
    amax_c = [jnp.max(a, axis=1, keepdims=True) for a in a_c]
    bend_c = [jnp.sum(jnp.where(last, b, 0.0), axis=1, keepdims=True) for b in b_c]

    if has_init:
        m_state = jnp.where(is_fwd1, m0_ref[0], jnp.where(row81 == 1, m0_ref[1], 0.0))
        n_init = jnp.where(row_n == 0, n0_ref[0], n0_ref[1])
        c_scr[...] = c0_ref[...]
    else:
        m_state = jnp.zeros((SUBLANES, 1), F32)
        n_init = jnp.zeros((SUBLANES, HEAD_DIM), F32)
        c_scr[...] = jnp.zeros(c_scr.shape, F32)

    m_before, top = [], []
    for j in range(n_chunks):
        jr = n_chunks - 1 - j
        amax = jnp.where(is_fwd1, amax_c[j], amax_c[jr])
        bend = jnp.where(is_fwd1, bend_c[j], bend_c[jr])
        m_before.append(m_state)
        top.append(jnp.maximum(m_state, amax))
        m_state = bend + top[-1]

    for c in range(n_chunks):
        cr = n_chunks - 1 - c
        m_c = jnp.where(is_fwd1, m_before[c], m_before[cr])
        top_c = jnp.where(is_fwd1, top[c], top[cr])
        big_m = jnp.maximum(m_c, cm_c[c])
        w_k = jnp.exp(a_c[c] - top_c)
        rows_scr[c, 0] = a_c[c]
        rows_scr[c, 1] = big_m
        rows_scr[c, 2] = jnp.exp(m_c - big_m)
        rows_scr[c, 3] = jnp.exp(-b_c[c] - big_m)
        rows_scr[c, 4] = w_k
        wk16_scr[c] = jnp.concatenate([w_k, jnp.zeros_like(w_k)], axis=0).astype(BF16)
        dec_scr[c] = jnp.broadcast_to(jnp.exp(m_c - top_c), (SUBLANES, HEAD_DIM))

    def load_k(r0):
        return (k_ref[pl.ds(r0, CHUNK), :].astype(F32) * scale).astype(BF16)

    def pass_a(j, nst):
        c = n_chunks - 1 - j
        r0 = pl.multiple_of(c * CHUNK, CHUNK)
        ks = load_k(r0)
        vt = v_ref[pl.ds(r0, CHUNK), :].astype(F32).T
        w_k = rows_scr[c, 4]
        vt_scr[c] = vt.astype(BF16)
        vtw_scr[c] = (vt * w_k[0:1, :]).astype(BF16)
        c_in = c_scr[1]
        cb_scr[c] = c_in.astype(BF16)
        n1_scr[c] = nst
        dec = dec_scr[c]
        c_scr[1] = dec[1:2, :] * c_in + jnp.dot((vt * w_k[1:2, :]).astype(BF16), ks,
                                                preferred_element_type=F32)
        n_upd = jnp.dot(wk16_scr[c], ks, preferred_element_type=F32)[:SUBLANES]
        return dec * nst + n_upd

    n_rev = lax.fori_loop(0, n_chunks, pass_a, n_init, unroll=unroll)

    row_id = lax.broadcasted_iota(jnp.int32, (CHUNK, CHUNK), 0)
    col_id = lax.broadcasted_iota(jnp.int32, (CHUNK, CHUNK), 1)
    tri = (row_id <= col_id, row_id >= col_id)

    def pass_b(c, nst):
        r0 = pl.multiple_of(c * CHUNK, CHUNK)
        q = q_ref[pl.ds(r0, CHUNK), :]
        ks = load_k(r0)
        vt = vt_scr[c]
        a_r, bigm_r, winter_r, floor_r = (rows_scr[c, i] for i in range(4))
        st_all = lax.dot_general(ks, q, NT_DIMS, preferred_element_type=F32)
        n_rows = jnp.where(row_n == 0, nst, n1_scr[c])
        n_rows = jnp.concatenate([n_rows, jnp.zeros_like(n_rows)], axis=0).astype(BF16)
        qn = lax.dot_general(n_rows, q, NT_DIMS, preferred_element_type=F32)
        c0 = c_scr[0]
        sts, invs = [], []
        for d in range(2):
            row = slice(d, d + 1)
            a_st = jnp.broadcast_to(a_r[row, :], (CHUNK, CHUNK)).T
            st = st_all * jnp.exp(jnp.where(tri[d], a_st - bigm_r[row, :], -jnp.inf))
            den = jnp.sum(st, axis=0, keepdims=True) + winter_r[row, :] * qn[row, :]
            invs.append(1.0 / jnp.maximum(jnp.abs(den), floor_r[row, :]))
            sts.append(st.astype(BF16))
        intra = jnp.dot(vt, jnp.concatenate(sts, axis=1), preferred_element_type=F32)
        ht = None
        for d in range(2):
            row = slice(d, d + 1)
            cb = c0.astype(BF16) if d == 0 else cb_scr[c]
            num_t = (intra[:, d * CHUNK:(d + 1) * CHUNK]
                     + winter_r[row, :] * lax.dot_general(cb, q, NT_DIMS,
                                                          preferred_element_type=F32))
            ht = num_t * invs[d] if ht is None else ht + num_t * invs[d]
        gate = jax.nn.sigmoid(o_ref[pl.ds(r0, CHUNK), :].astype(F32))
        ya_ref[pl.ds(r0, CHUNK), :] = (gate * ht.T).astype(ya_ref.dtype)

        dec = dec_scr[c]
        c_scr[0] = dec[0:1, :] * c0 + jnp.dot(vtw_scr[c], ks, preferred_element_type=F32)
        n_upd = jnp.dot(wk16_scr[c], ks, preferred_element_type=F32)[:SUBLANES]
        return dec * nst + n_upd

    n_fwd = lax.fori_loop(0, n_chunks, pass_b, n_init, unroll=unroll)

    if emit_state:
        if n_prev:
            cout_ref[:n_prev] = cprev_ref[...]
            nout_ref[:n_prev] = nprev_ref[...]
            mout_ref[:n_prev] = mprev_ref[...]
        cout_ref[n_prev] = c_scr[...]
        nout_ref[n_prev, 0] = n_fwd[0:1, :]
        nout_ref[n_prev, 1] = n_rev[1:2, :]
        mout_ref[n_prev, 0] = m_state[0:1, :]
        mout_ref[n_prev, 1] = m_state[1:2, :]


def _mlstm(z, scans, tok0, n_batch, seq_len, init=None, layer=0, prev=None):
    blk0 = tok0 // seq_len
    n_chunks = seq_len // CHUNK
    has_init = init is not None
    emit_state = not has_init
    n_prev = 0 if prev is None else prev[0].shape[1]

    def zspec(col):
        return pl.BlockSpec((seq_len, HEAD_DIM),
                            lambda b, h: (blk0 + b, col * N_HEADS + h))

    scan_spec = pl.BlockSpec((SUBLANES, seq_len), lambda b, h: (h, blk0 + b))
    in_specs = [zspec(COL_Q), zspec(COL_K), zspec(COL_V), zspec(COL_O),
                scan_spec, scan_spec, scan_spec]
    args = [z, z, z, z, *scans]

    def state_specs(n_layers):
        return [
            pl.BlockSpec((None, n_layers, 2, None, HEAD_DIM, HEAD_DIM),
                         lambda b, h: (b, 0, 0, h, 0, 0)),
            pl.BlockSpec((None, n_layers, 2, None, 1, HEAD_DIM), lambda b, h: (b, 0, 0, h, 0, 0)),
            pl.BlockSpec((None, n_layers, 2, None, 1, 1), lambda b, h: (b, 0, 0, h, 0, 0)),
        ]

    if has_init:
        sc, sn, sm = init
        nb, depth = sc.shape[:2]
        in_specs += [
            pl.BlockSpec((None, None, 2, None, HEAD_DIM, HEAD_DIM),
                         lambda b, h: (b, layer, 0, h, 0, 0)),
            pl.BlockSpec((None, None, 2, None, 1, HEAD_DIM), lambda b, h: (b, layer, 0, h, 0, 0)),
            pl.BlockSpec((None, None, 2, None, 1, 1), lambda b, h: (b, layer, 0, h, 0, 0)),
        ]
        args += [sc, sn.reshape(nb, depth, 2, N_HEADS, 1, HEAD_DIM),
                 sm.reshape(nb, depth, 2, N_HEADS, 1, 1)]
    if n_prev:
        in_specs += state_specs(n_prev)
        args += list(prev)
    out_specs = [pl.BlockSpec((seq_len, HEAD_DIM), lambda b, h: (b, h))]
    out_shape = [jax.ShapeDtypeStruct((n_batch * seq_len, D_MODEL), BF16)]
    if emit_state:
        n_out = n_prev + 1
        out_specs += state_specs(n_out)
        out_shape += [
            jax.ShapeDtypeStruct((n_batch, n_out, 2, N_HEADS, HEAD_DIM, HEAD_DIM), F32),
            jax.ShapeDtypeStruct((n_batch, n_out, 2, N_HEADS, 1, HEAD_DIM), F32),
            jax.ShapeDtypeStruct((n_batch, n_out, 2, N_HEADS, 1, 1), F32),
        ]
    outs = pl.pallas_call(
        functools.partial(_mlstm_kernel, seq_len=seq_len, has_init=has_init, emit_state=emit_state,
                          n_prev=n_prev),
        grid=(n_batch, N_HEADS),
        in_specs=in_specs,
        out_specs=out_specs,
        out_shape=out_shape,
        scratch_shapes=[
            pltpu.VMEM((n_chunks, 5, SUBLANES, CHUNK), F32),
            pltpu.VMEM((n_chunks, 2 * SUBLANES, CHUNK), BF16),
            pltpu.VMEM((n_chunks, SUBLANES, HEAD_DIM), F32),
            pltpu.VMEM((n_chunks, HEAD_DIM, CHUNK), BF16),
            pltpu.VMEM((n_chunks, HEAD_DIM, CHUNK), BF16),
            pltpu.VMEM((n_chunks, HEAD_DIM, HEAD_DIM), BF16),
            pltpu.VMEM((n_chunks, SUBLANES, HEAD_DIM), F32),
            pltpu.VMEM((2, HEAD_DIM, HEAD_DIM), F32),
        ],
        compiler_params=_params(("arbitrary", "arbitrary")),
        name="mlstm_ctx" if emit_state else "mlstm_lat",
    )(*args)
    if emit_state:
        return outs[0], tuple(outs[1:])
    return outs[0], None


TOK_TILE = 256
POOL_HALO = 64


def _mix_merge_kernel(*refs, n_x, n_first, n_ctx_tok, ctx_len, lat_len):
    x_refs = refs[:n_x]
    (ya_ctx_ref, ya_lat_ref, pp_ref, pc_ref, pn_ref, u_ref, vg_ref, m0_ref, m1_ref, m2_ref,
     mod_ref, nw_ref, pw_ref, ps_ref, gnw_ref, ws_ref, gb_ref, wbr_ref, wout_ref, o_ref) = refs[n_x:]
    tok0 = pl.program_id(0) * TOK_TILE
    first = pl.program_id(0) < n_first
    is_ctx = tok0 < n_ctx_tok
    seq_len = jnp.where(is_ctx, ctx_len, lat_len)
    base = jnp.where(is_ctx, 0, n_ctx_tok)
    seq_start = base + ((tok0 - base) // seq_len) * seq_len
    seq_end = seq_start + seq_len

    k_rows = TOK_TILE + 2 * POOL_HALO
    t_abs = tok0 + lax.broadcasted_iota(jnp.int32, (TOK_TILE, k_rows), 0)
    u_abs = tok0 - POOL_HALO + lax.broadcasted_iota(jnp.int32, (TOK_TILE, k_rows), 1)
    in_seq = (u_abs >= seq_start) & (u_abs < seq_end)
    t_col = tok0 + lax.broadcasted_iota(jnp.int32, (TOK_TILE, 1), 0)
    p_ext = jnp.concatenate([pp_ref[...], pc_ref[...], pn_ref[...]], axis=0)
    yb = []
    for g, win in enumerate(POOL_WINDOWS):
        half = win // 2
        sl = slice(g * GROUP_DIM, (g + 1) * GROUP_DIM)
        band = (u_abs >= t_abs - half) & (u_abs < t_abs + half) & in_seq
        acc = jnp.dot(jnp.where(band, 1.0, 0.0).astype(BF16), p_ext[:, sl],
                      preferred_element_type=F32)
        count = (jnp.minimum(t_col + half, seq_end) - jnp.maximum(t_col - half, seq_start)).astype(F32)
        y = (acc / count - pc_ref[:, sl].astype(F32)).astype(BF16)
        yb.append(jnp.dot(y, pw_ref[g], preferred_element_type=F32) * ps_ref[:, sl])
    yb = jnp.concatenate(yb, axis=1).astype(BF16)

    u = jax.nn.gelu(u_ref[...].astype(F32))
    vg = _rms(jax.nn.gelu(vg_ref[...].astype(F32)), gnw_ref[...]).astype(BF16)
    yc = []
    for ch in range(TOK_TILE // CHUNK):
        rows = slice(ch * CHUNK, (ch + 1) * CHUNK)
        parts = []
        for g in range(N_GROUPS):
            sl = slice(g * GROUP_DIM, (g + 1) * GROUP_DIM)
            s = jnp.dot(ws_ref[g], vg[rows, sl], preferred_element_type=F32) + gb_ref[:, g:g + 1]
            parts.append(u[rows, sl] * s)
        yc.append(jnp.concatenate(parts, axis=1))
    yc = jnp.concatenate(yc, axis=0).astype(BF16)

    ys = (_pick((ya_ctx_ref, ya_lat_ref), first), yb, yc)
    mixed = None
    for n, (y, m_ref) in enumerate(zip(ys, (m0_ref, m1_ref, m2_ref))):
        br = jnp.dot(y, wbr_ref[n], preferred_element_type=F32)
        term = jax.nn.sigmoid(m_ref[...].astype(F32)) * br
        mixed = term if mixed is None else mixed + term
    o = jnp.dot(mixed.astype(BF16), wout_ref[...], preferred_element_type=F32)
    o_ref[...] = _pick(x_refs, first) + mod_ref[0, 2:3, :] * _rms(o, nw_ref[...])


FFN_TILE = 512
FFN_SLAB = 256


def _ffn_kernel(xp_ref, xc_ref, xn_ref, mod_ref, nw2_ref, nw3_ref, wup_ref, cw_ref, cb_ref, wd_ref,
                *rest, n_ctx_tok, ctx_len, lat_len):
    out_refs = rest
    i = pl.program_id(0)
    tok0 = i * FFN_TILE
    is_ctx = tok0 < n_ctx_tok
    tiles_per_lat = lat_len // FFN_TILE
    lat_tile = jnp.maximum(tok0 - n_ctx_tok, 0) // FFN_TILE % tiles_per_lat
    has_up = jnp.logical_and(jnp.logical_not(is_ctx), lat_tile != 0)
    has_dn = jnp.logical_and(jnp.logical_not(is_ctx), lat_tile != tiles_per_lat - 1)
    ext_rows = FFN_TILE + 2 * GRID_W
    r = lax.broadcasted_iota(jnp.int32, (FFN_TILE, FFN_SLAB), 0)
    col_ctx = r % ctx_len
    col_lat = r % GRID_W
    keep_left = jnp.where(jnp.where(is_ctx, col_ctx, col_lat) != 0, 1.0, 0.0)
    keep_right = jnp.where(jnp.where(is_ctx, col_ctx - (ctx_len - 1), col_lat - (GRID_W - 1)) != 0,
                           1.0, 0.0)
    r1 = lax.broadcasted_iota(jnp.int32, (ext_rows, 1), 0)
    in_grid = jnp.logical_and(jnp.logical_or(r1 >= GRID_W, has_up),
                              jnp.logical_or(r1 < GRID_W + FFN_TILE, has_dn))
    tap_on = jnp.where(lax.broadcasted_iota(jnp.int32, (9, 1), 0) // 3 == 1, 1.0,
                       jnp.where(is_ctx, 0.0, 1.0))
    cw = cw_ref[...] * tap_on

    x_ext = jnp.concatenate([xp_ref[...], xc_ref[...], xn_ref[...]], axis=0)
    h_ext = jnp.where(in_grid, _modnorm(x_ext, mod_ref, nw2_ref, 3, 4), 0.0).astype(BF16)
    h_cur = h_ext[GRID_W:GRID_W + FFN_TILE]

    ext_all = jnp.dot(h_ext, wup_ref[:, :D_FF], preferred_element_type=F32)
    gate_all = jnp.dot(h_cur, wup_ref[:, D_FF:], preferred_element_type=F32)

    u = []
    for c in range(D_FF // FFN_SLAB):
        sl = slice(c * FFN_SLAB, (c + 1) * FFN_SLAB)
        ext = ext_all[:, sl]
        q = []
        for dx in range(3):
            qs = None
            for dy in range(3):
                term = cw[dy * 3 + dx:dy * 3 + dx + 1, sl] * ext[dy * GRID_W:dy * GRID_W + FFN_TILE, :]
                qs = term if qs is None else qs + term
            q.append(qs)
        acc = (q[1] + cb_ref[:, sl] + pltpu.roll(q[0], 1, 0) * keep_left
               + pltpu.roll(q[2], FFN_TILE - 1, 0) * keep_right)
        u.append((jax.nn.gelu(acc) * gate_all[:, sl]).astype(BF16))
    f = jnp.dot(jnp.concatenate(u, axis=1), wd_ref[...], preferred_element_type=F32)

    y = xc_ref[...] + mod_ref[0, 5:6, :] * _rms(f, nw3_ref[...])
    if len(out_refs) == 1:
        out_refs[0][...] = y
    else:
        @pl.when(is_ctx)
        def _():
            out_refs[0][...] = y

        @pl.when(jnp.logical_not(is_ctx))
        def _():
            out_refs[1][...] = y


def _gate_rows(w_gate):
    idx, keep = [], []
    for p in range(N_HEADS * SUBLANES):
        h, j = divmod(p, SUBLANES)
        d, kind = j % 2, j // 2
        idx.append(d * 2 * N_HEADS + kind * N_HEADS + h if j < 4 else 0)
        keep.append(1.0 if j < 4 else 0.0)
    return w_gate[..., jnp.array(idx)] * jnp.array(keep, w_gate.dtype)


def kernel(x_prompt, x_sample, state_C, state_n, state_m, c, c_ctx, w_mod, b_mod, norm_w, w_in, b_in,
           mlstm_gate_b, pool_w, pool_scale, gmlp_norm_w, gmlp_ws, gmlp_b, w_br, w_out, w_up, conv_w,
           conv_b, w_down):
    n_ctx, ctx_len, d = x_prompt.shape
    n_lat, lat_len, _ = x_sample.shape
    depth = w_in.shape[0]
    n_ctx_tok = n_ctx * ctx_len
    n_lat_tok = n_lat * lat_len
    n_tok = n_ctx_tok + n_lat_tok
    assert d == D_MODEL and FFN_TILE % ctx_len == 0 and lat_len % FFN_TILE == 0
    assert n_ctx_tok % FFN_TILE == 0 and ctx_len % CHUNK == 0
    n_gate = 2 * 2 * N_HEADS
    gate0 = 4 * D_MODEL

    x_parts = (x_prompt.reshape(n_ctx_tok, d), x_sample.reshape(n_lat_tok, d))

    def tok_specs(n_parts, tm):
        if n_parts == 1:
            return [pl.BlockSpec((tm, d), lambda i, *_: (i, 0))]
        n_a = n_ctx_tok // tm
        return [pl.BlockSpec((tm, d), lambda i, *_: (jnp.minimum(i, n_a - 1), 0)),
                pl.BlockSpec((tm, d), lambda i, *_: (jnp.maximum(i - n_a, 0), 0))]

    cvec = jnp.zeros((SUBLANES, d), F32).at[0].set(c_ctx).at[1:1 + n_lat].set(c)
    mod = _modulation(cvec, w_mod, b_mod)

    def mod_spec(tm):
        def row(i):
            t0 = i * tm
            return jnp.where(t0 < n_ctx_tok, 0, 1 + jnp.maximum(t0 - n_ctx_tok, 0) // lat_len)
        return row

    tm_proj = 1024
    tn_in = 2048
    w_in16 = w_in.astype(BF16)
    row_in = mod_spec(tm_proj)
    row_tok = mod_spec(TOK_TILE)
    row_ffn = mod_spec(FFN_TILE)
    vec = lambda a: a.reshape(1, -1)
    states = None
    n_grow = N_HEADS * SUBLANES
    n_wa = gate0 // tn_in
    n_wb = (MAIN_WIDTH - gate0) // tn_in
    scan_lanes = 2048

    for l in range(depth):
        last_layer = l == depth - 1
        w_b = w_in16[l, :, gate0 + n_gate:]
        wg_rows = _gate_rows(w_in[l, :, gate0:gate0 + n_gate]).T.astype(BF16)
        bga_rows = _gate_rows(b_in[l, gate0:gate0 + n_gate]).reshape(-1, 1)
        bgb_rows = _gate_rows(mlstm_gate_b[l].reshape(n_gate)).reshape(-1, 1)
        mod_l = mod[l]
        nw = norm_w[l]

        z, gt = pl.pallas_call(
            functools.partial(_inproj_kernel, n_x=len(x_parts), n_first=n_ctx_tok // tm_proj, n_wa=n_wa),
            grid=(n_tok // tm_proj, n_wa + n_wb),
            in_specs=tok_specs(len(x_parts), tm_proj) + [
                pl.BlockSpec((1, N_MOD, d), lambda i, j: (row_in(i), 0, 0)),
                pl.BlockSpec((1, d), lambda i, j: (0, 0)),
                pl.BlockSpec((None, d, tn_in), lambda i, j, l=l: (l, 0, jnp.minimum(j, n_wa - 1))),
                pl.BlockSpec((1, tn_in), lambda i, j: (0, jnp.minimum(j, n_wa - 1))),
                pl.BlockSpec((d, tn_in), lambda i, j: (0, jnp.maximum(j - n_wa, 0))),
                pl.BlockSpec((1, tn_in), lambda i, j: (0, jnp.maximum(j - n_wa, 0))),
                pl.BlockSpec((n_grow, d), lambda i, j: (0, 0)),
                pl.BlockSpec((n_grow, 1), lambda i, j: (0, 0)),
                pl.BlockSpec((n_grow, 1), lambda i, j: (0, 0)),
            ],
            out_specs=[
                pl.BlockSpec((tm_proj, tn_in), lambda i, j: (i, j)),
                pl.BlockSpec((n_grow, tm_proj), lambda i, j: (0, i)),
            ],
            out_shape=[
                jax.ShapeDtypeStruct((n_tok, MAIN_WIDTH), BF16),
                jax.ShapeDtypeStruct((n_grow, n_tok), F32),
            ],
            scratch_shapes=[pltpu.VMEM((tm_proj, d), BF16)],
            compiler_params=_params(("arbitrary", "arbitrary")),
            name="inproj",
        )(*x_parts, mod_l, vec(nw[0]), w_in16, vec(b_in[l, :gate0]), w_b, vec(b_in[l, gate0 + n_gate:]),
          wg_rows, bga_rows, bgb_rows)

        scan_spec = pl.BlockSpec((n_grow, scan_lanes), lambda i: (0, i))
        scans = pl.pallas_call(
            _gates_kernel,
            grid=(n_tok // scan_lanes,),
            in_specs=[scan_spec],
            out_specs=[scan_spec] * 3,
            out_shape=[jax.ShapeDtypeStruct((n_grow, n_tok), F32)] * 3,
            compiler_params=_params(("arbitrary",)),
            name="gates",
        )(gt)

        ya_ctx, states = _mlstm(z, scans, 0, n_ctx, ctx_len, prev=states)
        ya_lat, _ = _mlstm(z, scans, n_ctx_tok, n_lat, lat_len,
                           init=(state_C, state_n, state_m), layer=l)

        tm = TOK_TILE
        halo_per_tile = tm // POOL_HALO
        n_halo = n_tok // POOL_HALO
        zcol = lambda col: pl.BlockSpec((tm, d), lambda i: (i, col))
        full = lambda shape: pl.BlockSpec(shape, lambda i: (0,) * len(shape))
        x = pl.pallas_call(
            functools.partial(_mix_merge_kernel, n_x=len(x_parts), n_first=n_ctx_tok // tm,
                              n_ctx_tok=n_ctx_tok, ctx_len=ctx_len, lat_len=lat_len),
            grid=(n_tok // tm,),
            in_specs=tok_specs(len(x_parts), tm) + tok_specs(2, tm) + [
                pl.BlockSpec((POOL_HALO, d),
                             lambda i: (jnp.maximum(i * halo_per_tile - 1, 0), COL_P)),
                zcol(COL_P),
                pl.BlockSpec((POOL_HALO, d),
                             lambda i: (jnp.minimum((i + 1) * halo_per_tile, n_halo - 1), COL_P)),
                zcol(COL_U), zcol(COL_VG), zcol(COL_M), zcol(COL_M + 1), zcol(COL_M + 2),
                pl.BlockSpec((1, N_MOD, d), lambda i: (row_tok(i), 0, 0)),
                full((1, d)),
                full((N_GROUPS, GROUP_DIM, GROUP_DIM)), full((1, d)), full((1, d)),
                full((N_GROUPS, CHUNK, CHUNK)), full((CHUNK, N_GROUPS)),
                full((3, d, d)), full((d, d)),
            ],
            out_specs=pl.BlockSpec((tm, d), lambda i: (i, 0)),
            out_shape=jax.ShapeDtypeStruct((n_tok, d), F32),
            compiler_params=_params(("arbitrary",)),
            name="mix_merge",
        )(*x_parts, ya_ctx, ya_lat, z, z, z, z, z, z, z, z, mod_l, vec(nw[1]),
          pool_w[l].astype(BF16), vec(pool_scale[l]), vec(gmlp_norm_w[l]),
          gmlp_ws[l].astype(BF16), gmlp_b[l].T, w_br[l].astype(BF16), w_out[l].astype(BF16))
        x_parts = (x,)

        tf = FFN_TILE
        rows_per_tile = tf // GRID_W
        n_rows = n_tok // GRID_W
        n_ctx_tiles = n_ctx_tok // tf
        if last_layer:
            out_specs = [pl.BlockSpec((tf, d), lambda i: (jnp.minimum(i, n_ctx_tiles - 1), 0)),
                         pl.BlockSpec((tf, d), lambda i: (jnp.maximum(i - n_ctx_tiles, 0), 0))]
            out_shape = [jax.ShapeDtypeStruct((n_ctx_tok, d), F32),
                         jax.ShapeDtypeStruct((n_lat_tok, d), F32)]
        else:
            out_specs = [pl.BlockSpec((tf, d), lambda i: (i, 0))]
            out_shape = [jax.ShapeDtypeStruct((n_tok, d), F32)]
        x_parts = tuple(pl.pallas_call(
            functools.partial(_ffn_kernel, n_ctx_tok=n_ctx_tok, ctx_len=ctx_len, lat_len=lat_len),
            grid=(n_tok // tf,),
            in_specs=[
                pl.BlockSpec((GRID_W, d), lambda i: (jnp.maximum(i * rows_per_tile - 1, 0), 0)),
                pl.BlockSpec((tf, d), lambda i: (i, 0)),
                pl.BlockSpec((GRID_W, d),
                             lambda i: (jnp.minimum((i + 1) * rows_per_tile, n_rows - 1), 0)),
                pl.BlockSpec((1, N_MOD, d), lambda i: (row_ffn(i), 0, 0)),
                full((1, d)), full((1, d)), full((d, 2 * D_FF)),
                full((9, D_FF)), full((1, D_FF)), full((D_FF, d)),
            ],
            out_specs=out_specs,
            out_shape=out_shape,
            compiler_params=_params(("arbitrary",)),
            name="ffn",
        )(x, x, x, mod_l, vec(nw[2]), vec(nw[3]), w_up[l].astype(BF16),
          conv_w[l].reshape(9, D_FF), vec(conv_b[l]), w_down[l].astype(BF16)))

    new_c, new_n, new_m = states
    return (x_parts[0].reshape(n_ctx, ctx_len, d), x_parts[1].reshape(n_lat, lat_len, d), new_c,
            new_n.reshape(n_ctx, depth, 2, N_HEADS, HEAD_DIM), new_m.reshape(n_ctx, depth, 2, N_HEADS))
```

```python
import functools

import jax
import jax.numpy as jnp
import numpy as np
from jax import lax
from jax.experimental import pallas as pl
from jax.experimental.pallas import tpu as pltpu

F32 = jnp.float32
BF16 = jnp.bfloat16

D_MODEL = 1024
N_HEADS = 4
HEAD_DIM = D_MODEL // N_HEADS
CHUNK = 128
POOL_WINDOWS = (2, 4, 8, 16)
N_GROUPS = 4
GROUP_DIM = D_MODEL // N_GROUPS
D_FF = 2816
GRID_W = 64
N_MOD = 6
RMS_EPS = 1e-6
LANES = 128
SUBLANES = 8
VMEM_LIMIT = 56 * 1024 * 1024

COL_Q, COL_K, COL_V, COL_O, COL_P, COL_U, COL_VG, COL_M = 0, 1, 2, 3, 4, 5, 6, 7
MAIN_WIDTH = 10 * D_MODEL

NT_DIMS = (((1,), (1,)), ((), ()))


def _rms(x, w):
    return x * lax.rsqrt(jnp.mean(x * x, axis=-1, keepdims=True) + RMS_EPS) * w


def _log_sigmoid(x):
    return jnp.minimum(x, 0.0) - jnp.log1p(jnp.exp(-jnp.abs(x)))


def _params(sem):
    return pltpu.CompilerParams(dimension_semantics=sem, vmem_limit_bytes=VMEM_LIMIT)


def _mod_kernel(c_ref, w_ref, b_ref, o_ref):
    c = c_ref[...]
    s = (c * jax.nn.sigmoid(c)).astype(BF16)
    o_ref[...] = jnp.dot(s, w_ref[...].astype(BF16), preferred_element_type=F32) + b_ref[...]


def _modulation(cvec, w_mod, b_mod):
    depth = w_mod.shape[0]
    n = N_MOD * D_MODEL
    tn = 1536
    out = pl.pallas_call(
        _mod_kernel,
        grid=(depth, n // tn),
        in_specs=[
            pl.BlockSpec((SUBLANES, D_MODEL), lambda l, j: (0, 0)),
            pl.BlockSpec((None, D_MODEL, tn), lambda l, j: (l, 0, j)),
            pl.BlockSpec((None, 1, tn), lambda l, j: (l, 0, j)),
        ],
        out_specs=pl.BlockSpec((None, SUBLANES, tn), lambda l, j: (l, 0, j)),
        out_shape=jax.ShapeDtypeStruct((depth, SUBLANES, n), F32),
        compiler_params=_params(("arbitrary", "arbitrary")),
        name="modulation",
    )(cvec, w_mod, b_mod.reshape(depth, 1, n))
    return out.reshape(depth, SUBLANES, N_MOD, D_MODEL)


def _modnorm(x, mod_ref, nw_ref, k_shift, k_scale):
    h = _rms(x, nw_ref[...])
    return h * (1.0 + mod_ref[0, k_scale:k_scale + 1, :]) + mod_ref[0, k_shift:k_shift + 1, :]


def _pick(refs, first):
    if len(refs) == 1:
        return refs[0][...]
    return jnp.where(first, refs[0][...], refs[1][...])


def _wprep_kernel(main_ref, next_ref, o_ref, *, n_plain, shift):
    t = pl.program_id(1)
    tile = o_ref.shape[1]

    @pl.when(t < n_plain)
    def _():
        o_ref[...] = main_ref[...].astype(BF16)

    @pl.when(t == COL_K * D_MODEL // tile)
    def _():
        k0 = COL_K * D_MODEL % tile
        o_ref[:, k0:k0 + D_MODEL] = (main_ref[:, k0:k0 + D_MODEL] * HEAD_DIM ** -0.5).astype(BF16)

    @pl.when(t >= n_plain)
    def _():
        wide = jnp.concatenate([main_ref[...], next_ref[...]], axis=1)
        o_ref[...] = pltpu.roll(wide, wide.shape[1] - shift, 1)[:, :tile].astype(BF16)


def _inproj_kernel(*refs, n_x, n_first):
    x_refs = refs[:n_x]
    mod_ref, nw_ref, w_ref, b_ref, wgt_ref, bgta_ref, bgtb_ref, z_ref, gt_ref, h_scr = refs[n_x:]

    @pl.when(pl.program_id(1) == 0)
    def _():
        x = _pick(x_refs, pl.program_id(0) < n_first)
        hb = _modnorm(x, mod_ref, nw_ref, 0, 1).astype(BF16)
        h_scr[...] = hb
        gt_ref[...] = (lax.dot_general(wgt_ref[...], hb, NT_DIMS, preferred_element_type=F32)
                       + bgta_ref[...] + bgtb_ref[...])

    z = jnp.dot(h_scr[...], w_ref[...], preferred_element_type=F32) + b_ref[...]
    z_ref[...] = z.astype(z_ref.dtype)


def _gates_kernel(g_ref, a_ref, b_ref, cm_ref):
    g = g_ref[...]
    fwd = lax.broadcasted_iota(jnp.int32, g.shape, 0) % SUBLANES == 0
    lane = lax.broadcasted_iota(jnp.int32, g.shape, 1) % CHUNK
    lf = _log_sigmoid(pltpu.roll(g, g.shape[0] - 2, 0))
    pre, suf = _chunk_scans(lf, jnp.add, 0.0, lane)
    b = jnp.where(fwd, pre, suf)
    a = g - b
    pre, suf = _chunk_scans(a, jnp.maximum, -jnp.inf, lane)
    a_ref[...] = a
    b_ref[...] = b
    cm_ref[...] = jnp.where(fwd, pre, suf)


def _chunk_scans(x, op, fill, lane):
    width = x.shape[1]
    fwd = bwd = x
    k = 1
    while k < CHUNK:
        fwd = op(fwd, jnp.where(lane >= k, pltpu.roll(fwd, k, 1), fill))
        bwd = op(bwd, jnp.where(lane < CHUNK - k, pltpu.roll(bwd, width - k, 1), fill))
        k *= 2
    return fwd, bwd


def _mlstm_kernel(*refs, seq_len, has_init, emit_state, n_prev):
    q_ref, k_ref, v_ref, o_ref, a_ref, b_ref, cm_ref = refs[:7]
    pos = 7
    if has_init:
        c0_ref, n0_ref, m0_ref = refs[pos:pos + 3]
        pos += 3
    if n_prev:
        cprev_ref, nprev_ref, mprev_ref = refs[pos:pos + 3]
        pos += 3
    ya_ref = refs[pos]
    pos += 1
    if emit_state:
        cout_ref, nout_ref, mout_ref = refs[pos:pos + 3]
        pos += 3
    rows_scr, wk16_scr, dec_scr, vt_scr, vtw_scr, cb_scr, n1_scr, c_scr = refs[pos:pos + 8]

    n_chunks = seq_len // CHUNK
    unroll = min(4, n_chunks)
    row81 =lax.broadcasted_iota(jnp.int32, (SUBLANES, 1), 0)
    is_fwd1 = row81 == 0
    row_n = lax.broadcasted_iota(jnp.int32, (SUBLANES, HEAD_DIM), 0)

    chunk = lambda ref, c: ref[:, c * CHUNK:(c + 1) * CHUNK]
    a_c = [chunk(a_ref, c) for c in range(n_chunks)]
    b_c = [chunk(b_ref, c) for c in range(n_chunks)]
    cm_c = [chunk(cm_ref, c) for c in range(n_chunks)]
    row8 = lax.broadcasted_iota(jnp.int32, (SUBLANES, CHUNK), 0)
    lane8 = lax.broadcasted_iota(jnp.int32, (SUBLANES, CHUNK), 1)
    last = lane8 == jnp.where(row8 == 0, CHUNK - 1, 0)
    amax_c = [jnp.max(a, axis=1, keepdims=True) for a in a_c]
    bend_c = [jnp.sum(jnp.where(last, b, 0.0), axis=1, keepdims=True) for b in b_c]

    if has_init:
        m_state = jnp.where(is_fwd1, m0_ref[0], jnp.where(row81 == 1, m0_ref[1], 0.0))
        n_init = jnp.where(row_n == 0, n0_ref[0], n0_ref[1])
        c_scr[...] = c0_ref[...]
    else:
        m_state = jnp.zeros((SUBLANES, 1), F32)
        n_init = jnp.zeros((SUBLANES, HEAD_DIM), F32)
        c_scr[...] = jnp.zeros(c_scr.shape, F32)

    m_before, top = [], []
    for j in range(n_chunks):
        jr = n_chunks - 1 - j
        amax = jnp.where(is_fwd1, amax_c[j], amax_c[jr])
        bend = jnp.where(is_fwd1, bend_c[j], bend_c[jr])
        m_before.append(m_state)
        top.append(jnp.maximum(m_state, amax))
        m_state = bend + top[-1]

    for c in range(n_chunks):
        cr = n_chunks - 1 - c
        m_c = jnp.where(is_fwd1, m_before[c], m_before[cr])
        top_c = jnp.where(is_fwd1, top[c], top[cr])
        big_m = jnp.maximum(m_c, cm_c[c])
        w_k = jnp.exp(a_c[c] - top_c)
        rows_scr[c, 0] = a_c[c]
        rows_scr[c, 1] = big_m
        rows_scr[c, 2] = jnp.exp(m_c - big_m)
        rows_scr[c, 3] = jnp.exp(-b_c[c] - big_m)
        rows_scr[c, 4] = w_k
        wk16_scr[c] = jnp.concatenate([w_k, jnp.zeros_like(w_k)], axis=0).astype(BF16)
        dec_scr[c] = jnp.broadcast_to(jnp.exp(m_c - top_c), (SUBLANES, HEAD_DIM))

    def load_k(r0):
        return k_ref[pl.ds(r0, CHUNK), :]

    def pass_a(j, nst):
        c = n_chunks - 1 - j
        r0 = pl.multiple_of(c * CHUNK, CHUNK)
        ks = load_k(r0)
        vt = v_ref[pl.ds(r0, CHUNK), :].astype(F32).T
        w_k = rows_scr[c, 4]
        vt_scr[c] = vt.astype(BF16)
        vtw_scr[c] = (vt * w_k[0:1, :]).astype(BF16)
        c_in = c_scr[1]
        cb_scr[c] = c_in.astype(BF16)
        n1_scr[c] = nst
        dec = dec_scr[c]
        c_scr[1] = dec[1:2, :] * c_in + jnp.dot((vt * w_k[1:2, :]).astype(BF16), ks,
                                                preferred_element_type=F32)
        n_upd = jnp.dot(wk16_scr[c], ks, preferred_element_type=F32)[:SUBLANES]
        return dec * nst + n_upd

    n_rev = lax.fori_loop(0, n_chunks, pass_a, n_init, unroll=unroll)

    row_id = lax.broadcasted_iota(jnp.int32, (CHUNK, CHUNK), 0)
    col_id = lax.broadcasted_iota(jnp.int32, (CHUNK, CHUNK), 1)
    tri = (row_id <= col_id, row_id >= col_id)

    def pass_b(c, nst):
        r0 = pl.multiple_of(c * CHUNK, CHUNK)
        q = q_ref[pl.ds(r0, CHUNK), :]
        ks = load_k(r0)
        vt = vt_scr[c]
        a_r, bigm_r, winter_r, floor_r = (rows_scr[c, i] for i in range(4))
        st_all = lax.dot_general(ks, q, NT_DIMS, preferred_element_type=F32)
        n_rows = jnp.where(row_n == 0, nst, n1_scr[c])
        n_rows = jnp.concatenate([n_rows, jnp.zeros_like(n_rows)], axis=0).astype(BF16)
        qn = lax.dot_general(n_rows, q, NT_DIMS, preferred_element_type=F32)
        c0 = c_scr[0]
        sts, invs = [], []
        for d in range(2):
            row = slice(d, d + 1)
            a_st = jnp.broadcast_to(a_r[row, :], (CHUNK, CHUNK)).T
            st = st_all * jnp.exp(jnp.where(tri[d], a_st - bigm_r[row, :], -jnp.inf))
            den = jnp.sum(st, axis=0, keepdims=True) + winter_r[row, :] * qn[row, :]
            invs.append(1.0 / jnp.maximum(jnp.abs(den), floor_r[row, :]))
            sts.append(st.astype(BF16))
        intra = jnp.dot(vt, jnp.concatenate(sts, axis=1), preferred_element_type=F32)
        ht = None
        for d in range(2):
            row = slice(d, d + 1)
            cb = c0.astype(BF16) if d == 0 else cb_scr[c]
            num_t = (intra[:, d * CHUNK:(d + 1) * CHUNK]
                     + winter_r[row, :] * lax.dot_general(cb, q, NT_DIMS,
                                                          preferred_element_type=F32))
            ht = num_t * invs[d] if ht is None else ht + num_t * invs[d]
        gate = jax.nn.sigmoid(o_ref[pl.ds(r0, CHUNK), :].astype(F32))
        ya_ref[pl.ds(r0, CHUNK), :] = (gate * ht.T).astype(ya_ref.dtype)

        dec = dec_scr[c]
        c_scr[0] = dec[0:1, :] * c0 + jnp.dot(vtw_scr[c], ks, preferred_element_type=F32)
        n_upd = jnp.dot(wk16_scr[c], ks, preferred_element_type=F32)[:SUBLANES]
        return dec * nst + n_upd

    n_fwd = lax.fori_loop(0, n_chunks, pass_b, n_init, unroll=unroll)

    if emit_state:
        if n_prev:
            cout_ref[:n_prev] = cprev_ref[...]
            nout_ref[:n_prev] = nprev_ref[...]
            mout_ref[:n_prev] = mprev_ref[...]
        cout_ref[n_prev] = c_scr[...]
        nout_ref[n_prev, 0] = n_fwd[0:1, :]
        nout_ref[n_prev, 1] = n_rev[1:2, :]
        mout_ref[n_prev, 0] = m_state[0:1, :]
        mout_ref[n_prev, 1] = m_state[1:2, :]


def _mlstm(z, scans, tok0, n_batch, seq_len, init=None, layer=0, prev=None):
    blk0 = tok0 // seq_len
    n_chunks = seq_len // CHUNK
    has_init = init is not None
    emit_state = not has_init
    n_prev = 0 if prev is None else prev[0].shape[1]

    def zspec(col):
        return pl.BlockSpec((seq_len, HEAD_DIM),
                            lambda b, h: (blk0 + b, col * N_HEADS + h))

    scan_spec = pl.BlockSpec((SUBLANES, seq_len), lambda b, h: (h, blk0 + b))
    in_specs = [zspec(COL_Q), zspec(COL_K), zspec(COL_V), zspec(COL_O),
                scan_spec, scan_spec, scan_spec]
    args = [z, z, z, z, *scans]

    def state_specs(n_layers):
        return [
            pl.BlockSpec((None, n_layers, 2, None, HEAD_DIM, HEAD_DIM),
                         lambda b, h: (b, 0, 0, h, 0, 0)),
            pl.BlockSpec((None, n_layers, 2, None, 1, HEAD_DIM), lambda b, h: (b, 0, 0, h, 0, 0)),
            pl.BlockSpec((None, n_layers, 2, None, 1, 1), lambda b, h: (b, 0, 0, h, 0, 0)),
        ]

    if has_init:
        sc, sn, sm = init
        nb, depth = sc.shape[:2]
        in_specs += [
            pl.BlockSpec((None, None, 2, None, HEAD_DIM, HEAD_DIM),
                         lambda b, h: (b, layer, 0, h, 0, 0)),
            pl.BlockSpec((None, None, 2, None, 1, HEAD_DIM), lambda b, h: (b, layer, 0, h, 0, 0)),
            pl.BlockSpec((None, None, 2, None, 1, 1), lambda b, h: (b, layer, 0, h, 0, 0)),
        ]
        args += [sc, sn.reshape(nb, depth, 2, N_HEADS, 1, HEAD_DIM),
                 sm.reshape(nb, depth, 2, N_HEADS, 1, 1)]
    if n_prev:
        in_specs += state_specs(n_prev)
        args += list(prev)
    out_specs = [pl.BlockSpec((seq_len, HEAD_DIM), lambda b, h: (b, h))]
    out_shape = [jax.ShapeDtypeStruct((n_batch * seq_len, D_MODEL), BF16)]
    if emit_state:
        n_out = n_prev + 1
        out_specs += state_specs(n_out)
        out_shape += [
            jax.ShapeDtypeStruct((n_batch, n_out, 2, N_HEADS, HEAD_DIM, HEAD_DIM), F32),
            jax.ShapeDtypeStruct((n_batch, n_out, 2, N_HEADS, 1, HEAD_DIM), F32),
            jax.ShapeDtypeStruct((n_batch, n_out, 2, N_HEADS, 1, 1), F32),
        ]
    outs = pl.pallas_call(
        functools.partial(_mlstm_kernel, seq_len=seq_len, has_init=has_init, emit_state=emit_state,
                          n_prev=n_prev),
        grid=(n_batch, N_HEADS),
        in_specs=in_specs,
        out_specs=out_specs,
        out_shape=out_shape,
        scratch_shapes=[
            pltpu.VMEM((n_chunks, 5, SUBLANES, CHUNK), F32),
            pltpu.VMEM((n_chunks, 2 * SUBLANES, CHUNK), BF16),
            pltpu.VMEM((n_chunks, SUBLANES, HEAD_DIM), F32),
            pltpu.VMEM((n_chunks, HEAD_DIM, CHUNK), BF16),
            pltpu.VMEM((n_chunks, HEAD_DIM, CHUNK), BF16),
            pltpu.VMEM((n_chunks, HEAD_DIM, HEAD_DIM), BF16),
            pltpu.VMEM((n_chunks, SUBLANES, HEAD_DIM), F32),
            pltpu.VMEM((2, HEAD_DIM, HEAD_DIM), F32),
        ],
        compiler_params=_params(("arbitrary", "arbitrary")),
        name="mlstm_ctx" if emit_state else "mlstm_lat",
    )(*args)
    if emit_state:
        return outs[0], tuple(outs[1:])
    return outs[0], None


TOK_TILE = 256
POOL_HALO = 64


def _mix_merge_kernel(*refs, n_x, n_first, n_ctx_tok, ctx_len, lat_len):
    x_refs = refs[:n_x]
    (ya_ctx_ref, ya_lat_ref, pp_ref, pc_ref, pn_ref, u_ref, vg_ref, m0_ref, m1_ref, m2_ref,
     mod_ref, nw_ref, band_ref, pw_ref, ps_ref, gnw_ref, ws_ref, gb_ref, wbr_ref, wout_ref,
     o_ref) = refs[n_x:]
    tok0 = pl.program_id(0) * TOK_TILE
    first = pl.program_id(0) < n_first
    is_ctx = tok0 < n_ctx_tok
    seq_len = jnp.where(is_ctx, ctx_len, lat_len)
    base = jnp.where(is_ctx, 0, n_ctx_tok)
    seq_start = base + ((tok0 - base) // seq_len) * seq_len
    seq_end = seq_start + seq_len

    t_col = tok0 + lax.broadcasted_iota(jnp.int32, (TOK_TILE, 1), 0)
    p_prev = jnp.where(tok0 > seq_start, pp_ref[...], jnp.zeros_like(pp_ref))
    p_next = jnp.where(tok0 + TOK_TILE < seq_end, pn_ref[...], jnp.zeros_like(pn_ref))
    p_ext = jnp.concatenate([p_prev, pc_ref[...], p_next], axis=0)
    yb = []
    for g, win in enumerate(POOL_WINDOWS):
        half = win // 2
        sl = slice(g * GROUP_DIM, (g + 1) * GROUP_DIM)
        acc = jnp.dot(band_ref[g], p_ext[:, sl], preferred_element_type=F32)
        count = (jnp.minimum(t_col + half, seq_end) - jnp.maximum(t_col - half, seq_start)).astype(F32)
        y = (acc / count - pc_ref[:, sl].astype(F32)).astype(BF16)
        yb.append(jnp.dot(y, pw_ref[g], preferred_element_type=F32) * ps_ref[:, sl])
    yb = jnp.concatenate(yb, axis=1).astype(BF16)

    u = jax.nn.gelu(u_ref[...].astype(F32))
    vg = _rms(jax.nn.gelu(vg_ref[...].astype(F32)), gnw_ref[...]).astype(BF16)
    yc = []
    for ch in range(TOK_TILE // CHUNK):
        rows = slice(ch * CHUNK, (ch + 1) * CHUNK)
        parts = []
        for g in range(N_GROUPS):
            sl = slice(g * GROUP_DIM, (g + 1) * GROUP_DIM)
            s = jnp.dot(ws_ref[g], vg[rows, sl], preferred_element_type=F32) + gb_ref[:, g:g + 1]
            parts.append(u[rows, sl] * s)
        yc.append(jnp.concatenate(parts, axis=1))
    yc = jnp.concatenate(yc, axis=0).astype(BF16)

    ys = (_pick((ya_ctx_ref, ya_lat_ref), first), yb, yc)
    mixed = None
    for n, (y, m_ref) in enumerate(zip(ys, (m0_ref, m1_ref, m2_ref))):
        br = jnp.dot(y, wbr_ref[n], preferred_element_type=F32)
        term = jax.nn.sigmoid(m_ref[...].astype(F32)) * br
        mixed = term if mixed is None else mixed + term
    o = jnp.dot(mixed.astype(BF16), wout_ref[...], preferred_element_type=F32)
    o_ref[...] = _pick(x_refs, first) + mod_ref[0, 2:3, :] * _rms(o, nw_ref[...])


FFN_TILE = 512
FFN_SLAB = 256


def _ffn_kernel(xp_ref, xc_ref, xn_ref, mod_ref, nw2_ref, nw3_ref, wup_ref, cw_ref, cb_ref, wd_ref,
                *rest, n_ctx_tok, ctx_len, lat_len):
    out_refs = rest
    i = pl.program_id(0)
    tok0 = i * FFN_TILE
    is_ctx = tok0 < n_ctx_tok
    tiles_per_lat = lat_len // FFN_TILE
    lat_tile = jnp.maximum(tok0 - n_ctx_tok, 0) // FFN_TILE % tiles_per_lat
    has_up = jnp.logical_and(jnp.logical_not(is_ctx), lat_tile != 0)
    has_dn = jnp.logical_and(jnp.logical_not(is_ctx), lat_tile != tiles_per_lat - 1)
    ext_rows = FFN_TILE + 2 * GRID_W
    r = lax.broadcasted_iota(jnp.int32, (FFN_TILE, FFN_SLAB), 0)
    col_ctx = r % ctx_len
    col_lat = r % GRID_W
    keep_left = jnp.where(jnp.where(is_ctx, col_ctx, col_lat) != 0, 1.0, 0.0)
    keep_right = jnp.where(jnp.where(is_ctx, col_ctx - (ctx_len - 1), col_lat - (GRID_W - 1)) != 0,
                           1.0, 0.0)
    r1 = lax.broadcasted_iota(jnp.int32, (ext_rows, 1), 0)
    in_grid = jnp.logical_and(jnp.logical_or(r1 >= GRID_W, has_up),
                              jnp.logical_or(r1 < GRID_W + FFN_TILE, has_dn))
    tap_on = jnp.where(lax.broadcasted_iota(jnp.int32, (9, 1), 0) // 3 == 1, 1.0,
                       jnp.where(is_ctx, 0.0, 1.0))
    cw = cw_ref[...] * tap_on

    x_ext = jnp.concatenate([xp_ref[...], xc_ref[...], xn_ref[...]], axis=0)
    h_ext = jnp.where(in_grid, _modnorm(x_ext, mod_ref, nw2_ref, 3, 4), 0.0).astype(BF16)
    h_cur = h_ext[GRID_W:GRID_W + FFN_TILE]

    ext_all = jnp.dot(h_ext, wup_ref[:, :D_FF], preferred_element_type=F32)
    gate_all = jnp.dot(h_cur, wup_ref[:, D_FF:], preferred_element_type=F32)

    u = []
    for c in range(D_FF // FFN_SLAB):
        sl = slice(c * FFN_SLAB, (c + 1) * FFN_SLAB)
        ext = ext_all[:, sl]
        q = []
        for dx in range(3):
            qs = None
            for dy in range(3):
                term = cw[dy * 3 + dx:dy * 3 + dx + 1, sl] * ext[dy * GRID_W:dy * GRID_W + FFN_TILE, :]
                qs = term if qs is None else qs + term
            q.append(qs)
        acc = (q[1] + cb_ref[:, sl] + pltpu.roll(q[0], 1, 0) * keep_left
               + pltpu.roll(q[2], FFN_TILE - 1, 0) * keep_right)
        u.append((jax.nn.gelu(acc) * gate_all[:, sl]).astype(BF16))
    f = jnp.dot(jnp.concatenate(u, axis=1), wd_ref[...], preferred_element_type=F32)

    y = xc_ref[...] + mod_ref[0, 5:6, :] * _rms(f, nw3_ref[...])
    if len(out_refs) == 1:
        out_refs[0][...] = y
    else:
        @pl.when(is_ctx)
        def _():
            out_refs[0][...] = y

        @pl.when(jnp.logical_not(is_ctx))
        def _():
            out_refs[1][...] = y


def _gate_rows(w_gate):
    idx, keep = [], []
    for p in range(N_HEADS * SUBLANES):
        h, j = divmod(p, SUBLANES)
        d, kind = j % 2, j // 2
        idx.append(d * 2 * N_HEADS + kind * N_HEADS + h if j < 4 else 0)
        keep.append(1.0 if j < 4 else 0.0)
    return w_gate[..., jnp.array(idx)] * jnp.array(keep, w_gate.dtype)


def kernel(x_prompt, x_sample, state_C, state_n, state_m, c, c_ctx, w_mod, b_mod, norm_w, w_in, b_in,
           mlstm_gate_b, pool_w, pool_scale, gmlp_norm_w, gmlp_ws, gmlp_b, w_br, w_out, w_up, conv_w,
           conv_b, w_down):
    n_ctx, ctx_len, d = x_prompt.shape
    n_lat, lat_len, _ = x_sample.shape
    depth = w_in.shape[0]
    n_ctx_tok = n_ctx * ctx_len
    n_lat_tok = n_lat * lat_len
    n_tok = n_ctx_tok + n_lat_tok
    assert d == D_MODEL and FFN_TILE % ctx_len == 0 and lat_len % FFN_TILE == 0
    assert n_ctx_tok % FFN_TILE == 0 and ctx_len % CHUNK == 0
    n_gate = 2 * 2 * N_HEADS
    gate0 = 4 * D_MODEL

    x_parts = (x_prompt.reshape(n_ctx_tok, d), x_sample.reshape(n_lat_tok, d))

    def tok_specs(n_parts, tm):
        if n_parts == 1:
            return [pl.BlockSpec((tm, d), lambda i, *_: (i, 0))]
        n_a = n_ctx_tok // tm
        return [pl.BlockSpec((tm, d), lambda i, *_: (jnp.minimum(i, n_a - 1), 0)),
                pl.BlockSpec((tm, d), lambda i, *_: (jnp.maximum(i - n_a, 0), 0))]

    cvec = jnp.zeros((SUBLANES, d), F32).at[0].set(c_ctx).at[1:1 + n_lat].set(c)
    mod = _modulation(cvec, w_mod, b_mod)

    def mod_spec(tm):
        def row(i):
            t0 = i * tm
            return jnp.where(t0 < n_ctx_tok, 0, 1 + jnp.maximum(t0 - n_ctx_tok, 0) // lat_len)
        return row

    tm_proj = 1024
    tn_in = 2048
    n_col_tiles = MAIN_WIDTH // tn_in
    assert gate0 % tn_in == 0 and n_gate <= LANES
    w_main = pl.pallas_call(
        functools.partial(_wprep_kernel, n_plain=gate0 // tn_in, shift=n_gate),
        grid=(depth, n_col_tiles),
        in_specs=[pl.BlockSpec((None, d, tn_in), lambda l, t: (l, 0, t)),
                  pl.BlockSpec((None, d, LANES), lambda l, t: (l, 0, (t + 1) * (tn_in // LANES)))],
        out_specs=pl.BlockSpec((None, d, tn_in), lambda l, t: (l, 0, t)),
        out_shape=jax.ShapeDtypeStruct((depth, d, MAIN_WIDTH), BF16),
        compiler_params=_params(("arbitrary", "arbitrary")),
        name="wprep",
    )(w_in, w_in)
    k_scale = jnp.ones((MAIN_WIDTH,), F32).at[COL_K * d:(COL_K + 1) * d].set(HEAD_DIM ** -0.5)
    b_main = (jnp.concatenate([b_in[:, :gate0], b_in[:, gate0 + n_gate:]], axis=1)
              * k_scale).reshape(depth, 1, MAIN_WIDTH)
    row_in = mod_spec(tm_proj)
    row_tok = mod_spec(TOK_TILE)
    row_ffn = mod_spec(FFN_TILE)
    vec = lambda a: a.reshape(1, -1)
    states = None
    t_idx = np.arange(TOK_TILE)[:, None]
    u_idx = np.arange(TOK_TILE + 2 * POOL_HALO)[None, :] - POOL_HALO
    pool_bands = jnp.asarray(np.stack([(u_idx >= t_idx - w // 2) & (u_idx < t_idx + w // 2)
                                       for w in POOL_WINDOWS]), BF16)
    n_grow = N_HEADS * SUBLANES
    scan_lanes = 2048

    for l in range(depth):
        last_layer = l == depth - 1
        wg_rows = _gate_rows(w_in[l, :, gate0:gate0 + n_gate]).T.astype(BF16)
        bga_rows = _gate_rows(b_in[l, gate0:gate0 + n_gate]).reshape(-1, 1)
        bgb_rows = _gate_rows(mlstm_gate_b[l].reshape(n_gate)).reshape(-1, 1)
        mod_l = mod[l]
        nw = norm_w[l]

        z, gt = pl.pallas_call(
            functools.partial(_inproj_kernel, n_x=len(x_parts), n_first=n_ctx_tok // tm_proj),
            grid=(n_tok // tm_proj, n_col_tiles),
            in_specs=tok_specs(len(x_parts), tm_proj) + [
                pl.BlockSpec((1, N_MOD, d), lambda i, j: (row_in(i), 0, 0)),
                pl.BlockSpec((1, d), lambda i, j: (0, 0)),
                pl.BlockSpec((None, d, tn_in), lambda i, j, l=l: (l, 0, j)),
                pl.BlockSpec((None, 1, tn_in), lambda i, j, l=l: (l, 0, j)),
                pl.BlockSpec((n_grow, d), lambda i, j: (0, 0)),
                pl.BlockSpec((n_grow, 1), lambda i, j: (0, 0)),
                pl.BlockSpec((n_grow, 1), lambda i, j: (0, 0)),
            ],
            out_specs=[
                pl.BlockSpec((tm_proj, tn_in), lambda i, j: (i, j)),
                pl.BlockSpec((n_grow, tm_proj), lambda i, j: (0, i)),
            ],
            out_shape=[
                jax.ShapeDtypeStruct((n_tok, MAIN_WIDTH), BF16),
                jax.ShapeDtypeStruct((n_grow, n_tok), F32),
            ],
            scratch_shapes=[pltpu.VMEM((tm_proj, d), BF16)],
            compiler_params=_params(("arbitrary", "arbitrary")),
            name="inproj",
        )(*x_parts, mod_l, vec(nw[0]), w_main, b_main, wg_rows, bga_rows, bgb_rows)

        scan_spec = pl.BlockSpec((n_grow, scan_lanes), lambda i: (0, i))
        scans = pl.pallas_call(
            _gates_kernel,
            grid=(n_tok // scan_lanes,),
            in_specs=[scan_spec],
            out_specs=[scan_spec] * 3,
            out_shape=[jax.ShapeDtypeStruct((n_grow, n_tok), F32)] * 3,
            compiler_params=_params(("arbitrary",)),
            name="gates",
        )(gt)

        ya_ctx, states = _mlstm(z, scans, 0, n_ctx, ctx_len, prev=states)
        ya_lat, _ = _mlstm(z, scans, n_ctx_tok, n_lat, lat_len,
                           init=(state_C, state_n, state_m), layer=l)

        tm = TOK_TILE
        halo_per_tile = tm // POOL_HALO
        n_halo = n_tok // POOL_HALO
        zcol = lambda col: pl.BlockSpec((tm, d), lambda i: (i, col))
        full = lambda shape: pl.BlockSpec(shape, lambda i: (0,) * len(shape))
        x = pl.pallas_call(
            functools.partial(_mix_merge_kernel, n_x=len(x_parts), n_first=n_ctx_tok // tm,
                              n_ctx_tok=n_ctx_tok, ctx_len=ctx_len, lat_len=lat_len),
            grid=(n_tok // tm,),
            in_specs=tok_specs(len(x_parts), tm) + tok_specs(2, tm) + [
                pl.BlockSpec((POOL_HALO, d),
                             lambda i: (jnp.maximum(i * halo_per_tile - 1, 0), COL_P)),
                zcol(COL_P),
                pl.BlockSpec((POOL_HALO, d),
                             lambda i: (jnp.minimum((i + 1) * halo_per_tile, n_halo - 1), COL_P)),
                zcol(COL_U), zcol(COL_VG), zcol(COL_M), zcol(COL_M + 1), zcol(COL_M + 2),
                pl.BlockSpec((1, N_MOD, d), lambda i: (row_tok(i), 0, 0)),
                full((1, d)),
                full(pool_bands.shape),
                full((N_GROUPS, GROUP_DIM, GROUP_DIM)), full((1, d)), full((1, d)),
                full((N_GROUPS, CHUNK, CHUNK)), full((CHUNK, N_GROUPS)),
                full((3, d, d)), full((d, d)),
            ],
            out_specs=pl.BlockSpec((tm, d), lambda i: (i, 0)),
            out_shape=jax.ShapeDtypeStruct((n_tok, d), F32),
            compiler_params=_params(("arbitrary",)),
            name="mix_merge",
        )(*x_parts, ya_ctx, ya_lat, z, z, z, z, z, z, z, z, mod_l, vec(nw[1]),
          pool_bands, pool_w[l].astype(BF16), vec(pool_scale[l]), vec(gmlp_norm_w[l]),
          gmlp_ws[l].astype(BF16), gmlp_b[l].T, w_br[l].astype(BF16), w_out[l].astype(BF16))
        x_parts = (x,)

        tf = FFN_TILE
        rows_per_tile = tf // GRID_W
        n_rows = n_tok // GRID_W
        n_ctx_tiles = n_ctx_tok // tf
        if last_layer:
            out_specs = [pl.BlockSpec((tf, d), lambda i: (jnp.minimum(i, n_ctx_tiles - 1), 0)),
                         pl.BlockSpec((tf, d), lambda i: (jnp.maximum(i - n_ctx_tiles, 0), 0))]
            out_shape = [jax.ShapeDtypeStruct((n_ctx_tok, d), F32),
                         jax.ShapeDtypeStruct((n_lat_tok, d), F32)]
        else:
            out_specs = [pl.BlockSpec((tf, d), lambda i: (i, 0))]
            out_shape = [jax.ShapeDtypeStruct((n_tok, d), F32)]
        x_parts = tuple(pl.pallas_call(
            functools.partial(_ffn_kernel, n_ctx_tok=n_ctx_tok, ctx_len=ctx_len, lat_len=lat_len),
            grid=(n_tok // tf,),
            in_specs=[
                pl.BlockSpec((GRID_W, d), lambda i: (jnp.maximum(i * rows_per_tile - 1, 0), 0)),
                pl.BlockSpec((tf, d), lambda i: (i, 0)),
                pl.BlockSpec((GRID_W, d),
                             lambda i: (jnp.minimum((i + 1) * rows_per_tile, n_rows - 1), 0)),
                pl.BlockSpec((1, N_MOD, d), lambda i: (row_ffn(i), 0, 0)),
                full((1, d)), full((1, d)), full((d, 2 * D_FF)),
                full((9, D_FF)), full((1, D_FF)), full((D_FF, d)),
            ],
            out_specs=out_specs,
            out_shape=out_shape,
            compiler_params=_params(("arbitrary",)),
            name="ffn",
        )(x, x, x, mod_l, vec(nw[2]), vec(nw[3]), w_up[l].astype(BF16),
          conv_w[l].reshape(9, D_FF), vec(conv_b[l]), w_down[l].astype(BF16)))

    new_c, new_n, new_m = states
    return (x_parts[0].reshape(n_ctx, ctx_len, d), x_parts[1].reshape(n_lat, lat_len, d), new_c,
            new_n.reshape(n_ctx, depth, 2, N_HEADS, HEAD_DIM), new_m.reshape(n_ctx, depth, 2, N_HEADS))
```

```python
import functools

import jax
import jax.numpy as jnp
import numpy as np
from jax import lax
from jax.experimental import pallas as pl
from jax.experimental.pallas import tpu as pltpu

F32 = jnp.float32
BF16 = jnp.bfloat16

D_MODEL = 1024
N_HEADS = 4
HEAD_DIM = D_MODEL // N_HEADS
CHUNK = 128
POOL_WINDOWS = (2, 4, 8, 16)
N_GROUPS = 4
GROUP_DIM = D_MODEL // N_GROUPS
D_FF = 2816
GRID_W = 64
N_MOD = 6
RMS_EPS = 1e-6
LANES = 128
SUBLANES = 8
VMEM_LIMIT = 56 * 1024 * 1024

COL_Q, COL_K, COL_V, COL_O, COL_P, COL_U, COL_VG, COL_M = 0, 1, 2, 3, 4, 5, 6, 7
MAIN_WIDTH = 10 * D_MODEL

NT_DIMS = (((1,), (1,)), ((), ()))


def _rms(x, w):
    return x * lax.rsqrt(jnp.mean(x * x, axis=-1, keepdims=True) + RMS_EPS) * w


def _log_sigmoid(x):
    return jnp.minimum(x, 0.0) - jnp.log1p(jnp.exp(-jnp.abs(x)))


def _params(sem):
    return pltpu.CompilerParams(dimension_semantics=sem, vmem_limit_bytes=VMEM_LIMIT)


def _mod_kernel(c_ref, w_ref, b_ref, o_ref):
    c = c_ref[...]
    s = (c * jax.nn.sigmoid(c)).astype(BF16)
    o_ref[...] = jnp.dot(s, w_ref[...].astype(BF16), preferred_element_type=F32) + b_ref[...]


def _modulation(cvec, w_mod, b_mod):
    depth = w_mod.shape[0]
    n = N_MOD * D_MODEL
    tn = 1536
    out = pl.pallas_call(
        _mod_kernel,
        grid=(depth, n // tn),
        in_specs=[
            pl.BlockSpec((SUBLANES, D_MODEL), lambda l, j: (0, 0)),
            pl.BlockSpec((None, D_MODEL, tn), lambda l, j: (l, 0, j)),
            pl.BlockSpec((None, 1, tn), lambda l, j: (l, 0, j)),
        ],
        out_specs=pl.BlockSpec((None, SUBLANES, tn), lambda l, j: (l, 0, j)),
        out_shape=jax.ShapeDtypeStruct((depth, SUBLANES, n), F32),
        compiler_params=_params(("arbitrary", "arbitrary")),
        name="modulation",
    )(cvec, w_mod, b_mod.reshape(depth, 1, n))
    return out.reshape(depth, SUBLANES, N_MOD, D_MODEL)


def _modnorm(x, mod_ref, nw_ref, k_shift, k_scale):
    h = _rms(x, nw_ref[...])
    return h * (1.0 + mod_ref[0, k_scale:k_scale + 1, :]) + mod_ref[0, k_shift:k_shift + 1, :]


def _pick(refs, first):
    if len(refs) == 1:
        return refs[0][...]
    return jnp.where(first, refs[0][...], refs[1][...])


def _inproj_kernel(*refs, n_x, n_first):
    x_refs = refs[:n_x]
    mod_ref, nw_ref, w_ref, b_ref, wgt_ref, bgta_ref, bgtb_ref, z_ref, gt_ref, h_scr = refs[n_x:]

    @pl.when(pl.program_id(1) == 0)
    def _():
        x = _pick(x_refs, pl.program_id(0) < n_first)
        hb = _modnorm(x, mod_ref, nw_ref, 0, 1).astype(BF16)
        h_scr[...] = hb
        gt_ref[...] = (lax.dot_general(wgt_ref[...], hb, NT_DIMS, preferred_element_type=F32)
                       + bgta_ref[...] + bgtb_ref[...])

    z = jnp.dot(h_scr[...], w_ref[...], preferred_element_type=F32) + b_ref[...]
    z_ref[...] = z.astype(z_ref.dtype)


def _gates_kernel(g_ref, a_ref, b_ref, cm_ref):
    g = g_ref[...]
    fwd = lax.broadcasted_iota(jnp.int32, g.shape, 0) % SUBLANES == 0
    lane = lax.broadcasted_iota(jnp.int32, g.shape, 1) % CHUNK
    lf = _log_sigmoid(pltpu.roll(g, g.shape[0] - 2, 0))
    pre, suf = _chunk_scans(lf, jnp.add, 0.0, lane)
    b = jnp.where(fwd, pre, suf)
    a = g - b
    pre, suf = _chunk_scans(a, jnp.maximum, -jnp.inf, lane)
    a_ref[...] = a
    b_ref[...] = b
    cm_ref[...] = jnp.where(fwd, pre, suf)


def _chunk_scans(x, op, fill, lane):
    width = x.shape[1]
    fwd = bwd = x
    k = 1
    while k < CHUNK:
        fwd = op(fwd, jnp.where(lane >= k, pltpu.roll(fwd, k, 1), fill))
        bwd = op(bwd, jnp.where(lane < CHUNK - k, pltpu.roll(bwd, width - k, 1), fill))
        k *= 2
    return fwd, bwd


def _mlstm_kernel(*refs, seq_len, has_init, emit_state, n_prev):
    q_ref, k_ref, v_ref, o_ref, a_ref, b_ref, cm_ref = refs[:7]
    pos = 7
    if has_init:
        c0_ref, n0_ref, m0_ref = refs[pos:pos + 3]
        pos += 3
    if n_prev:
        cprev_ref, nprev_ref, mprev_ref = refs[pos:pos + 3]
        pos += 3
    ya_ref = refs[pos]
    pos += 1
    if emit_state:
        cout_ref, nout_ref, mout_ref = refs[pos:pos + 3]
        pos += 3
    rows_scr, wk16_scr, dec_scr, vt_scr, vtw_scr, cb_scr, n1_scr, c_scr = refs[pos:pos + 8]

    n_chunks = seq_len // CHUNK
    unroll = min(4, n_chunks)
    row81 =lax.broadcasted_iota(jnp.int32, (SUBLANES, 1), 0)
    is_fwd1 = row81 == 0
    row_n = lax.broadcasted_iota(jnp.int32, (SUBLANES, HEAD_DIM), 0)

    chunk = lambda ref, c: ref[:, c * CHUNK:(c + 1) * CHUNK]
    a_c = [chunk(a_ref, c) for c in range(n_chunks)]
    b_c = [chunk(b_ref, c) for c in range(n_chunks)]
    cm_c = [chunk(cm_ref, c) for c in range(n_chunks)]
    row8 = lax.broadcasted_iota(jnp.int32, (SUBLANES, CHUNK), 0)
    lane8 = lax.broadcasted_iota(jnp.int32, (SUBLANES, CHUNK), 1)
    last = lane8 == jnp.where(row8 == 0, CHUNK - 1, 0)
    amax_c = [jnp.max(a, axis=1, keepdims=True) for a in a_c]
    bend_c = [jnp.sum(jnp.where(last, b, 0.0), axis=1, keepdims=True) for b in b_c]

    if has_init:
        m_state = jnp.where(is_fwd1, m0_ref[0], jnp.where(row81 == 1, m0_ref[1], 0.0))
        n_init = jnp.where(row_n == 0, n0_ref[0], n0_ref[1])
        c_scr[...] = c0_ref[...]
    else:
        m_state = jnp.zeros((SUBLANES, 1), F32)
        n_init = jnp.zeros((SUBLANES, HEAD_DIM), F32)
        c_scr[...] = jnp.zeros(c_scr.shape, F32)

    m_before, top = [], []
    for j in range(n_chunks):
        jr = n_chunks - 1 - j
        amax = jnp.where(is_fwd1, amax_c[j], amax_c[jr])
        bend = jnp.where(is_fwd1, bend_c[j], bend_c[jr])
        m_before.append(m_state)
        top.append(jnp.maximum(m_state, amax))
        m_state = bend + top[-1]

    for c in range(n_chunks):
        cr = n_chunks - 1 - c
        m_c = jnp.where(is_fwd1, m_before[c], m_before[cr])
        top_c = jnp.where(is_fwd1, top[c], top[cr])
        big_m = jnp.maximum(m_c, cm_c[c])
        w_k = jnp.exp(a_c[c] - top_c)
        rows_scr[c, 0] = a_c[c]
        rows_scr[c, 1] = big_m
        rows_scr[c, 2] = jnp.exp(m_c - big_m)
        rows_scr[c, 3] = jnp.exp(-b_c[c] - big_m)
        rows_scr[c, 4] = w_k
        wk16_scr[c] = jnp.concatenate([w_k, jnp.zeros_like(w_k)], axis=0).astype(BF16)
        dec_scr[c] = jnp.broadcast_to(jnp.exp(m_c - top_c), (SUBLANES, HEAD_DIM))

    def load_k(r0):
        return k_ref[pl.ds(r0, CHUNK), :]

    def pass_a(j, nst):
        c = n_chunks - 1 - j
        r0 = pl.multiple_of(c * CHUNK, CHUNK)
        ks = load_k(r0)
        vt = v_ref[pl.ds(r0, CHUNK), :].astype(F32).T
        w_k = rows_scr[c, 4]
        vt_scr[c] = vt.astype(BF16)
        vtw_scr[c] = (vt * w_k[0:1, :]).astype(BF16)
        c_in = c_scr[1]
        cb_scr[c] = c_in.astype(BF16)
        n1_scr[c] = nst
        dec = dec_scr[c]
        c_scr[1] = dec[1:2, :] * c_in + jnp.dot((vt * w_k[1:2, :]).astype(BF16), ks,
                                                preferred_element_type=F32)
        n_upd = jnp.dot(wk16_scr[c], ks, preferred_element_type=F32)[:SUBLANES]
        return dec * nst + n_upd

    n_rev = lax.fori_loop(0, n_chunks, pass_a, n_init, unroll=unroll)

    row_id = lax.broadcasted_iota(jnp.int32, (CHUNK, CHUNK), 0)
    col_id = lax.broadcasted_iota(jnp.int32, (CHUNK, CHUNK), 1)
    tri = (row_id <= col_id, row_id >= col_id)

    def pass_b(c, nst):
        r0 = pl.multiple_of(c * CHUNK, CHUNK)
        q = q_ref[pl.ds(r0, CHUNK), :]
        ks = load_k(r0)
        vt = vt_scr[c]
        a_r, bigm_r, winter_r, floor_r = (rows_scr[c, i] for i in range(4))
        st_all = lax.dot_general(ks, q, NT_DIMS, preferred_element_type=F32)
        n_rows = jnp.where(row_n == 0, nst, n1_scr[c])
        n_rows = jnp.concatenate([n_rows, jnp.zeros_like(n_rows)], axis=0).astype(BF16)
        qn = lax.dot_general(n_rows, q, NT_DIMS, preferred_element_type=F32)
        c0 = c_scr[0]
        sts, invs = [], []
        for d in range(2):
            row = slice(d, d + 1)
            a_st = jnp.broadcast_to(a_r[row, :], (CHUNK, CHUNK)).T
            st = st_all * jnp.exp(jnp.where(tri[d], a_st - bigm_r[row, :], -jnp.inf))
            den = jnp.sum(st, axis=0, keepdims=True) + winter_r[row, :] * qn[row, :]
            invs.append(1.0 / jnp.maximum(jnp.abs(den), floor_r[row, :]))
            sts.append(st.astype(BF16))
        intra = jnp.dot(vt, jnp.concatenate(sts, axis=1), preferred_element_type=F32)
        ht = None
        for d in range(2):
            row = slice(d, d + 1)
            cb = c0.astype(BF16) if d == 0 else cb_scr[c]
            num_t = (intra[:, d * CHUNK:(d + 1) * CHUNK]
                     + winter_r[row, :] * lax.dot_general(cb, q, NT_DIMS,
                                                          preferred_element_type=F32))
            ht = num_t * invs[d] if ht is None else ht + num_t * invs[d]
        gate = jax.nn.sigmoid(o_ref[pl.ds(r0, CHUNK), :].astype(F32))
        ya_ref[pl.ds(r0, CHUNK), :] = (gate * ht.T).astype(ya_ref.dtype)

        dec = dec_scr[c]
        c_scr[0] = dec[0:1, :] * c0 + jnp.dot(vtw_scr[c], ks, preferred_element_type=F32)
        n_upd = jnp.dot(wk16_scr[c], ks, preferred_element_type=F32)[:SUBLANES]
        return dec * nst + n_upd

    n_fwd = lax.fori_loop(0, n_chunks, pass_b, n_init, unroll=unroll)

    if emit_state:
        if n_prev:
            cout_ref[:n_prev] = cprev_ref[...]
            nout_ref[:n_prev] = nprev_ref[...]
            mout_ref[:n_prev] = mprev_ref[...]
        cout_ref[n_prev] = c_scr[...]
        nout_ref[n_prev, 0] = n_fwd[0:1, :]
        nout_ref[n_prev, 1] = n_rev[1:2, :]
        mout_ref[n_prev, 0] = m_state[0:1, :]
        mout_ref[n_prev, 1] = m_state[1:2, :]


def _mlstm(z, scans, tok0, n_batch, seq_len, init=None, layer=0, prev=None):
    blk0 = tok0 // seq_len
    n_chunks = seq_len // CHUNK
    has_init = init is not None
    emit_state = not has_init
    n_prev = 0 if prev is None else prev[0].shape[1]

    def zspec(col):
        return pl.BlockSpec((seq_len, HEAD_DIM),
                            lambda b, h: (blk0 + b, col * N_HEADS + h))

    scan_spec = pl.BlockSpec((SUBLANES, seq_len), lambda b, h: (h, blk0 + b))
    in_specs = [zspec(COL_Q), zspec(COL_K), zspec(COL_V), zspec(COL_O),
                scan_spec, scan_spec, scan_spec]
    args = [z, z, z, z, *scans]

    def state_specs(n_layers):
        return [
            pl.BlockSpec((None, n_layers, 2, None, HEAD_DIM, HEAD_DIM),
                         lambda b, h: (b, 0, 0, h, 0, 0)),
            pl.BlockSpec((None, n_layers, 2, None, 1, HEAD_DIM), lambda b, h: (b, 0, 0, h, 0, 0)),
            pl.BlockSpec((None, n_layers, 2, None, 1, 1), lambda b, h: (b, 0, 0, h, 0, 0)),
        ]

    if has_init:
        sc, sn, sm = init
        nb, depth = sc.shape[:2]
        in_specs += [
            pl.BlockSpec((None, None, 2, None, HEAD_DIM, HEAD_DIM),
                         lambda b, h: (b, layer, 0, h, 0, 0)),
            pl.BlockSpec((None, None, 2, None, 1, HEAD_DIM), lambda b, h: (b, layer, 0, h, 0, 0)),
            pl.BlockSpec((None, None, 2, None, 1, 1), lambda b, h: (b, layer, 0, h, 0, 0)),
        ]
        args += [sc, sn.reshape(nb, depth, 2, N_HEADS, 1, HEAD_DIM),
                 sm.reshape(nb, depth, 2, N_HEADS, 1, 1)]
    if n_prev:
        in_specs += state_specs(n_prev)
        args += list(prev)
    out_specs = [pl.BlockSpec((seq_len, HEAD_DIM), lambda b, h: (b, h))]
    out_shape = [jax.ShapeDtypeStruct((n_batch * seq_len, D_MODEL), BF16)]
    if emit_state:
        n_out = n_prev + 1
        out_specs += state_specs(n_out)
        out_shape += [
            jax.ShapeDtypeStruct((n_batch, n_out, 2, N_HEADS, HEAD_DIM, HEAD_DIM), F32),
            jax.ShapeDtypeStruct((n_batch, n_out, 2, N_HEADS, 1, HEAD_DIM), F32),
            jax.ShapeDtypeStruct((n_batch, n_out, 2, N_HEADS, 1, 1), F32),
        ]
    outs = pl.pallas_call(
        functools.partial(_mlstm_kernel, seq_len=seq_len, has_init=has_init, emit_state=emit_state,
                          n_prev=n_prev),
        grid=(n_batch, N_HEADS),
        in_specs=in_specs,
        out_specs=out_specs,
        out_shape=out_shape,
        scratch_shapes=[
            pltpu.VMEM((n_chunks, 5, SUBLANES, CHUNK), F32),
            pltpu.VMEM((n_chunks, 2 * SUBLANES, CHUNK), BF16),
            pltpu.VMEM((n_chunks, SUBLANES, HEAD_DIM), F32),
            pltpu.VMEM((n_chunks, HEAD_DIM, CHUNK), BF16),
            pltpu.VMEM((n_chunks, HEAD_DIM, CHUNK), BF16),
            pltpu.VMEM((n_chunks, HEAD_DIM, HEAD_DIM), BF16),
            pltpu.VMEM((n_chunks, SUBLANES, HEAD_DIM), F32),
            pltpu.VMEM((2, HEAD_DIM, HEAD_DIM), F32),
        ],
        compiler_params=_params(("arbitrary", "arbitrary")),
        name="mlstm_ctx" if emit_state else "mlstm_lat",
    )(*args)
    if emit_state:
        return outs[0], tuple(outs[1:])
    return outs[0], None


TOK_TILE = 256
POOL_HALO = 64


def _mix_merge_kernel(*refs, n_x, n_first, n_ctx_tok, ctx_len, lat_len):
    x_refs = refs[:n_x]
    (ya_ctx_ref, ya_lat_ref, pp_ref, pc_ref, pn_ref, u_ref, vg_ref, m0_ref, m1_ref, m2_ref,
     mod_ref, nw_ref, band_ref, pw_ref, ps_ref, gnw_ref, ws_ref, gb_ref, wbr_ref, wout_ref,
     o_ref) = refs[n_x:]
    tok0 = pl.program_id(0) * TOK_TILE
    first = pl.program_id(0) < n_first
    is_ctx = tok0 < n_ctx_tok
    seq_len = jnp.where(is_ctx, ctx_len, lat_len)
    base = jnp.where(is_ctx, 0, n_ctx_tok)
    seq_start = base + ((tok0 - base) // seq_len) * seq_len
    seq_end = seq_start + seq_len

    t_col = tok0 + lax.broadcasted_iota(jnp.int32, (TOK_TILE, 1), 0)
    p_prev = jnp.where(tok0 > seq_start, pp_ref[...], jnp.zeros_like(pp_ref))
    p_next = jnp.where(tok0 + TOK_TILE < seq_end, pn_ref[...], jnp.zeros_like(pn_ref))
    p_ext = jnp.concatenate([p_prev, pc_ref[...], p_next], axis=0)
    yb = []
    for g, win in enumerate(POOL_WINDOWS):
        half = win // 2
        sl = slice(g * GROUP_DIM, (g + 1) * GROUP_DIM)
        acc = jnp.dot(band_ref[g], p_ext[:, sl], preferred_element_type=F32)
        count = (jnp.minimum(t_col + half, seq_end) - jnp.maximum(t_col - half, seq_start)).astype(F32)
        y = (acc / count - pc_ref[:, sl].astype(F32)).astype(BF16)
        yb.append(jnp.dot(y, pw_ref[g], preferred_element_type=F32) * ps_ref[:, sl])
    yb = jnp.concatenate(yb, axis=1).astype(BF16)

    u = jax.nn.gelu(u_ref[...].astype(F32))
    vg = _rms(jax.nn.gelu(vg_ref[...].astype(F32)), gnw_ref[...]).astype(BF16)
    yc = []
    for ch in range(TOK_TILE // CHUNK):
        rows = slice(ch * CHUNK, (ch + 1) * CHUNK)
        parts = []
        for g in range(N_GROUPS):
            sl = slice(g * GROUP_DIM, (g + 1) * GROUP_DIM)
            s = jnp.dot(ws_ref[g], vg[rows, sl], preferred_element_type=F32) + gb_ref[:, g:g + 1]
            parts.append(u[rows, sl] * s)
        yc.append(jnp.concatenate(parts, axis=1))
    yc = jnp.concatenate(yc, axis=0).astype(BF16)

    ys = (_pick((ya_ctx_ref, ya_lat_ref), first), yb, yc)
    mixed = None
    for n, (y, m_ref) in enumerate(zip(ys, (m0_ref, m1_ref, m2_ref))):
        br = jnp.dot(y, wbr_ref[n], preferred_element_type=F32)
        term = jax.nn.sigmoid(m_ref[...].astype(F32)) * br
        mixed = term if mixed is None else mixed + term
    o = jnp.dot(mixed.astype(BF16), wout_ref[...], preferred_element_type=F32)
    o_ref[...] = _pick(x_refs, first) + mod_ref[0, 2:3, :] * _rms(o, nw_ref[...])


FFN_TILE = 512
FFN_SLAB = 256


def _ffn_kernel(xp_ref, xc_ref, xn_ref, mod_ref, nw2_ref, nw3_ref, wup_ref, cw_ref, cb_ref, wd_ref,
                *rest, n_ctx_tok, ctx_len, lat_len):
    out_refs = rest
    i = pl.program_id(0)
    tok0 = i * FFN_TILE
    is_ctx = tok0 < n_ctx_tok
    tiles_per_lat = lat_len // FFN_TILE
    lat_tile = jnp.maximum(tok0 - n_ctx_tok, 0) // FFN_TILE % tiles_per_lat
    has_up = jnp.logical_and(jnp.logical_not(is_ctx), lat_tile != 0)
    has_dn = jnp.logical_and(jnp.logical_not(is_ctx), lat_tile != tiles_per_lat - 1)
    ext_rows = FFN_TILE + 2 * GRID_W
    r = lax.broadcasted_iota(jnp.int32, (FFN_TILE, FFN_SLAB), 0)
    col_ctx = r % ctx_len
    col_lat = r % GRID_W
    keep_left = jnp.where(jnp.where(is_ctx, col_ctx, col_lat) != 0, 1.0, 0.0)
    keep_right = jnp.where(jnp.where(is_ctx, col_ctx - (ctx_len - 1), col_lat - (GRID_W - 1)) != 0,
                           1.0, 0.0)
    r1 = lax.broadcasted_iota(jnp.int32, (ext_rows, 1), 0)
    in_grid = jnp.logical_and(jnp.logical_or(r1 >= GRID_W, has_up),
                              jnp.logical_or(r1 < GRID_W + FFN_TILE, has_dn))
    tap_on = jnp.where(lax.broadcasted_iota(jnp.int32, (9, 1), 0) // 3 == 1, 1.0,
                       jnp.where(is_ctx, 0.0, 1.0))
    cw = cw_ref[...] * tap_on

    x_ext = jnp.concatenate([xp_ref[...], xc_ref[...], xn_ref[...]], axis=0)
    h_ext = jnp.where(in_grid, _modnorm(x_ext, mod_ref, nw2_ref, 3, 4), 0.0).astype(BF16)
    h_cur = h_ext[GRID_W:GRID_W + FFN_TILE]

    ext_all = jnp.dot(h_ext, wup_ref[:, :D_FF], preferred_element_type=F32)
    gate_all = jnp.dot(h_cur, wup_ref[:, D_FF:], preferred_element_type=F32)

    u = []
    for c in range(D_FF // FFN_SLAB):
        sl = slice(c * FFN_SLAB, (c + 1) * FFN_SLAB)
        ext = ext_all[:, sl]
        q = []
        for dx in range(3):
            qs = None
            for dy in range(3):
                term = cw[dy * 3 + dx:dy * 3 + dx + 1, sl] * ext[dy * GRID_W:dy * GRID_W + FFN_TILE, :]
                qs = term if qs is None else qs + term
            q.append(qs)
        acc = (q[1] + cb_ref[:, sl] + pltpu.roll(q[0], 1, 0) * keep_left
               + pltpu.roll(q[2], FFN_TILE - 1, 0) * keep_right)
        u.append((jax.nn.gelu(acc) * gate_all[:, sl]).astype(BF16))
    f = jnp.dot(jnp.concatenate(u, axis=1), wd_ref[...], preferred_element_type=F32)

    y = xc_ref[...] + mod_ref[0, 5:6, :] * _rms(f, nw3_ref[...])
    if len(out_refs) == 1:
        out_refs[0][...] = y
    else:
        @pl.when(is_ctx)
        def _():
            out_refs[0][...] = y

        @pl.when(jnp.logical_not(is_ctx))
        def _():
            out_refs[1][...] = y


def _gate_rows(w_gate):
    idx, keep = [], []
    for p in range(N_HEADS * SUBLANES):
        h, j = divmod(p, SUBLANES)
        d, kind = j % 2, j // 2
        idx.append(d * 2 * N_HEADS + kind * N_HEADS + h if j < 4 else 0)
        keep.append(1.0 if j < 4 else 0.0)
    return w_gate[..., jnp.array(idx)] * jnp.array(keep, w_gate.dtype)


def kernel(x_prompt, x_sample, state_C, state_n, state_m, c, c_ctx, w_mod, b_mod, norm_w, w_in, b_in,
           mlstm_gate_b, pool_w, pool_scale, gmlp_norm_w, gmlp_ws, gmlp_b, w_br, w_out, w_up, conv_w,
           conv_b, w_down):
    n_ctx, ctx_len, d = x_prompt.shape
    n_lat, lat_len, _ = x_sample.shape
    depth = w_in.shape[0]
    n_ctx_tok = n_ctx * ctx_len
    n_lat_tok = n_lat * lat_len
    n_tok = n_ctx_tok + n_lat_tok
    assert d == D_MODEL and FFN_TILE % ctx_len == 0 and lat_len % FFN_TILE == 0
    assert n_ctx_tok % FFN_TILE == 0 and ctx_len % TOK_TILE == 0 and lat_len % TOK_TILE == 0
    n_gate = 2 * 2 * N_HEADS
    gate0 = 4 * D_MODEL

    x_parts = (x_prompt.reshape(n_ctx_tok, d), x_sample.reshape(n_lat_tok, d))

    def tok_specs(n_parts, tm):
        if n_parts == 1:
            return [pl.BlockSpec((tm, d), lambda i, *_: (i, 0))]
        n_a = n_ctx_tok // tm
        return [pl.BlockSpec((tm, d), lambda i, *_: (jnp.minimum(i, n_a - 1), 0)),
                pl.BlockSpec((tm, d), lambda i, *_: (jnp.maximum(i - n_a, 0), 0))]

    cvec = jnp.zeros((SUBLANES, d), F32).at[0].set(c_ctx).at[1:1 + n_lat].set(c)
    mod = _modulation(cvec, w_mod, b_mod)

    def mod_spec(tm):
        def row(i):
            t0 = i * tm
            return jnp.where(t0 < n_ctx_tok, 0, 1 + jnp.maximum(t0 - n_ctx_tok, 0) // lat_len)
        return row

    tm_proj = 1024
    tn_in = 2048
    n_col_tiles = MAIN_WIDTH // tn_in
    k_scale = jnp.ones((MAIN_WIDTH,), F32).at[COL_K * d:(COL_K + 1) * d].set(HEAD_DIM ** -0.5)
    w_in16 = w_in.astype(BF16)
    w_main = (jnp.concatenate([w_in16[:, :, :gate0], w_in16[:, :, gate0 + n_gate:]], axis=2)
              * k_scale.astype(BF16))
    b_main = (jnp.concatenate([b_in[:, :gate0], b_in[:, gate0 + n_gate:]], axis=1)
              * k_scale).reshape(depth, 1, MAIN_WIDTH)
    row_in = mod_spec(tm_proj)
    row_tok = mod_spec(TOK_TILE)
    row_ffn = mod_spec(FFN_TILE)
    vec = lambda a: a.reshape(1, -1)
    states = None
    t_idx = np.arange(TOK_TILE)[:, None]
    u_idx = np.arange(TOK_TILE + 2 * POOL_HALO)[None, :] - POOL_HALO
    pool_bands = jnp.asarray(np.stack([(u_idx >= t_idx - w // 2) & (u_idx < t_idx + w // 2)
                                       for w in POOL_WINDOWS]), BF16)
    n_grow = N_HEADS * SUBLANES
    scan_lanes = 2048

    for l in range(depth):
        last_layer = l == depth - 1
        wg_rows = _gate_rows(w_in[l, :, gate0:gate0 + n_gate]).T.astype(BF16)
        bga_rows = _gate_rows(b_in[l, gate0:gate0 + n_gate]).reshape(-1, 1)
        bgb_rows = _gate_rows(mlstm_gate_b[l].reshape(n_gate)).reshape(-1, 1)
        mod_l = mod[l]
        nw = norm_w[l]

        z, gt = pl.pallas_call(
            functools.partial(_inproj_kernel, n_x=len(x_parts), n_first=n_ctx_tok // tm_proj),
            grid=(n_tok // tm_proj, n_col_tiles),
            in_specs=tok_specs(len(x_parts), tm_proj) + [
                pl.BlockSpec((1, N_MOD, d), lambda i, j: (row_in(i), 0, 0)),
                pl.BlockSpec((1, d), lambda i, j: (0, 0)),
                pl.BlockSpec((None, d, tn_in), lambda i, j, l=l: (l, 0, j)),
                pl.BlockSpec((None, 1, tn_in), lambda i, j, l=l: (l, 0, j)),
                pl.BlockSpec((n_grow, d), lambda i, j: (0, 0)),
                pl.BlockSpec((n_grow, 1), lambda i, j: (0, 0)),
                pl.BlockSpec((n_grow, 1), lambda i, j: (0, 0)),
            ],
            out_specs=[
                pl.BlockSpec((tm_proj, tn_in), lambda i, j: (i, j)),
                pl.BlockSpec((n_grow, tm_proj), lambda i, j: (0, i)),
            ],
            out_shape=[
                jax.ShapeDtypeStruct((n_tok, MAIN_WIDTH), BF16),
                jax.ShapeDtypeStruct((n_grow, n_tok), F32),
            ],
            scratch_shapes=[pltpu.VMEM((tm_proj, d), BF16)],
            compiler_params=_params(("arbitrary", "arbitrary")),
            name="inproj",
        )(*x_parts, mod_l, vec(nw[0]), w_main, b_main, wg_rows, bga_rows, bgb_rows)

        scan_spec = pl.BlockSpec((n_grow, scan_lanes), lambda i: (0, i))
        scans = pl.pallas_call(
            _gates_kernel,
            grid=(n_tok // scan_lanes,),
            in_specs=[scan_spec],
            out_specs=[scan_spec] * 3,
            out_shape=[jax.ShapeDtypeStruct((n_grow, n_tok), F32)] * 3,
            compiler_params=_params(("arbitrary",)),
            name="gates",
        )(gt)

        ya_ctx, states = _mlstm(z, scans, 0, n_ctx, ctx_len, prev=states)
        ya_lat, _ = _mlstm(z, scans, n_ctx_tok, n_lat, lat_len,
                           init=(state_C, state_n, state_m), layer=l)

        tm = TOK_TILE
        halo_per_tile = tm // POOL_HALO
        n_halo = n_tok // POOL_HALO
        zcol = lambda col: pl.BlockSpec((tm, d), lambda i: (i, col))
        full = lambda shape: pl.BlockSpec(shape, lambda i: (0,) * len(shape))
        x = pl.pallas_call(
            functools.partial(_mix_merge_kernel, n_x=len(x_parts), n_first=n_ctx_tok // tm,
                              n_ctx_tok=n_ctx_tok, ctx_len=ctx_len, lat_len=lat_len),
            grid=(n_tok // tm,),
            in_specs=tok_specs(len(x_parts), tm) + tok_specs(2, tm) + [
                pl.BlockSpec((POOL_HALO, d),
                             lambda i: (jnp.maximum(i * halo_per_tile - 1, 0), COL_P)),
                zcol(COL_P),
                pl.BlockSpec((POOL_HALO, d),
                             lambda i: (jnp.minimum((i + 1) * halo_per_tile, n_halo - 1), COL_P)),
                zcol(COL_U), zcol(COL_VG), zcol(COL_M), zcol(COL_M + 1), zcol(COL_M + 2),
                pl.BlockSpec((1, N_MOD, d), lambda i: (row_tok(i), 0, 0)),
                full((1, d)),
                full(pool_bands.shape),
                full((N_GROUPS, GROUP_DIM, GROUP_DIM)), full((1, d)), full((1, d)),
                full((N_GROUPS, CHUNK, CHUNK)), full((CHUNK, N_GROUPS)),
                full((3, d, d)), full((d, d)),
            ],
            out_specs=pl.BlockSpec((tm, d), lambda i: (i, 0)),
            out_shape=jax.ShapeDtypeStruct((n_tok, d), F32),
            compiler_params=_params(("arbitrary",)),
            name="mix_merge",
        )(*x_parts, ya_ctx, ya_lat, z, z, z, z, z, z, z, z, mod_l, vec(nw[1]),
          pool_bands, pool_w[l].astype(BF16), vec(pool_scale[l]), vec(gmlp_norm_w[l]),
          gmlp_ws[l].astype(BF16), gmlp_b[l].T, w_br[l].astype(BF16), w_out[l].astype(BF16))
        x_parts = (x,)

        tf = FFN_TILE
        rows_per_tile = tf // GRID_W
        n_rows = n_tok // GRID_W
        n_ctx_tiles = n_ctx_tok // tf
        if last_layer:
            out_specs = [pl.BlockSpec((tf, d), lambda i: (jnp.minimum(i, n_ctx_tiles - 1), 0)),
                         pl.BlockSpec((tf, d), lambda i: (jnp.maximum(i - n_ctx_tiles, 0), 0))]
            out_shape = [jax.ShapeDtypeStruct((n_ctx_tok, d), F32),
                         jax.ShapeDtypeStruct((n_lat_tok, d), F32)]
        else:
            out_specs = [pl.BlockSpec((tf, d), lambda i: (i, 0))]
            out_shape = [jax.ShapeDtypeStruct((n_tok, d), F32)]
        x_parts = tuple(pl.pallas_call(
            functools.partial(_ffn_kernel, n_ctx_tok=n_ctx_tok, ctx_len=ctx_len, lat_len=lat_len),
            grid=(n_tok // tf,),
            in_specs=[
                pl.BlockSpec((GRID_W, d), lambda i: (jnp.maximum(i * rows_per_tile - 1, 0), 0)),
                pl.BlockSpec((tf, d), lambda i: (i, 0)),
                pl.BlockSpec((GRID_W, d),
                             lambda i: (jnp.minimum((i + 1) * rows_per_tile, n_rows - 1), 0)),
                pl.BlockSpec((1, N_MOD, d), lambda i: (row_ffn(i), 0, 0)),
                full((1, d)), full((1, d)), full((d, 2 * D_FF)),
                full((9, D_FF)), full((1, D_FF)), full((D_FF, d)),
            ],
            out_specs=out_specs,
            out_shape=out_shape,
            compiler_params=_params(("arbitrary",)),
            name="ffn",
        )(x, x, x, mod_l, vec(nw[2]), vec(nw[3]), w_up[l].astype(BF16),
          conv_w[l].reshape(9, D_FF), vec(conv_b[l]), w_down[l].astype(BF16)))

    new_c, new_n, new_m = states
    return (x_parts[0].reshape(n_ctx, ctx_len, d), x_parts[1].reshape(n_lat, lat_len, d), new_c,
            new_n.reshape(n_ctx, depth, 2, N_HEADS, HEAD_DIM), new_m.reshape(n_ctx, depth, 2, N_HEADS))
```

```python
import functools

import jax
import jax.numpy as jnp
import numpy as np
from jax import lax
from jax.experimental import pallas as pl
from jax.experimental.pallas import tpu as pltpu

F32 = jnp.float32
BF16 = jnp.bfloat16

D_MODEL = 1024
N_HEADS = 4
HEAD_DIM = D_MODEL // N_HEADS
CHUNK = 128
POOL_WINDOWS = (2, 4, 8, 16)
N_GROUPS = 4
GROUP_DIM = D_MODEL // N_GROUPS
D_FF = 2816
GRID_W = 64
N_MOD = 6
RMS_EPS = 1e-6
LANES = 128
SUBLANES = 8
VMEM_LIMIT = 56 * 1024 * 1024

COL_Q, COL_K, COL_V, COL_O, COL_P, COL_U, COL_VG, COL_M = 0, 1, 2, 3, 4, 5, 6, 7
MAIN_WIDTH = 10 * D_MODEL

NT_DIMS = (((1,), (1,)), ((), ()))


def _rms(x, w):
    return x * lax.rsqrt(jnp.mean(x * x, axis=-1, keepdims=True) + RMS_EPS) * w


def _log_sigmoid(x):
    return jnp.minimum(x, 0.0) - jnp.log1p(jnp.exp(-jnp.abs(x)))


def _params(sem):
    return pltpu.CompilerParams(dimension_semantics=sem, vmem_limit_bytes=VMEM_LIMIT)


def _mod_kernel(c_ref, w_ref, b_ref, o_ref):
    c = c_ref[...]
    s = (c * jax.nn.sigmoid(c)).astype(BF16)
    o_ref[...] = jnp.dot(s, w_ref[...].astype(BF16), preferred_element_type=F32) + b_ref[...]


def _modulation(cvec, w_mod, b_mod):
    depth = w_mod.shape[0]
    n = N_MOD * D_MODEL
    tn = 1536
    out = pl.pallas_call(
        _mod_kernel,
        grid=(depth, n // tn),
        in_specs=[
            pl.BlockSpec((SUBLANES, D_MODEL), lambda l, j: (0, 0)),
            pl.BlockSpec((None, D_MODEL, tn), lambda l, j: (l, 0, j)),
            pl.BlockSpec((None, 1, tn), lambda l, j: (l, 0, j)),
        ],
        out_specs=pl.BlockSpec((None, SUBLANES, tn), lambda l, j: (l, 0, j)),
        out_shape=jax.ShapeDtypeStruct((depth, SUBLANES, n), F32),
        compiler_params=_params(("arbitrary", "arbitrary")),
        name="modulation",
    )(cvec, w_mod, b_mod.reshape(depth, 1, n))
    return out.reshape(depth, SUBLANES, N_MOD, D_MODEL)


def _modnorm(x, mod_ref, nw_ref, k_shift, k_scale):
    h = _rms(x, nw_ref[...])
    return h * (1.0 + mod_ref[0, k_scale:k_scale + 1, :]) + mod_ref[0, k_shift:k_shift + 1, :]


def _pick(refs, first):
    if len(refs) == 1:
        return refs[0][...]
    return jnp.where(first, refs[0][...], refs[1][...])


def _inproj_kernel(*refs, n_x, n_first):
    x_refs = refs[:n_x]
    mod_ref, nw_ref, w_ref, b_ref, wgt_ref, bgta_ref, bgtb_ref, z_ref, gt_ref, h_scr = refs[n_x:]

    @pl.when(pl.program_id(1) == 0)
    def _():
        x = _pick(x_refs, pl.program_id(0) < n_first)
        hb = _modnorm(x, mod_ref, nw_ref, 0, 1).astype(BF16)
        h_scr[...] = hb
        gt_ref[...] = (lax.dot_general(wgt_ref[...], hb, NT_DIMS, preferred_element_type=F32)
                       + bgta_ref[...] + bgtb_ref[...])

    z = jnp.dot(h_scr[...], w_ref[...], preferred_element_type=F32) + b_ref[...]
    z_ref[...] = z.astype(z_ref.dtype)


def _gates_kernel(g_ref, a_ref, b_ref, cm_ref):
    g = g_ref[...]
    fwd = lax.broadcasted_iota(jnp.int32, g.shape, 0) % SUBLANES == 0
    lane = lax.broadcasted_iota(jnp.int32, g.shape, 1) % CHUNK
    lf = _log_sigmoid(pltpu.roll(g, g.shape[0] - 2, 0))
    pre, suf = _chunk_scans(lf, jnp.add, 0.0, lane)
    b = jnp.where(fwd, pre, suf)
    a = g - b
    pre, suf = _chunk_scans(a, jnp.maximum, -jnp.inf, lane)
    a_ref[...] = a
    b_ref[...] = b
    cm_ref[...] = jnp.where(fwd, pre, suf)


def _chunk_scans(x, op, fill, lane):
    width = x.shape[1]
    fwd = bwd = x
    k = 1
    while k < CHUNK:
        fwd = op(fwd, jnp.where(lane >= k, pltpu.roll(fwd, k, 1), fill))
        bwd = op(bwd, jnp.where(lane < CHUNK - k, pltpu.roll(bwd, width - k, 1), fill))
        k *= 2
    return fwd, bwd


def _mlstm_kernel(*refs, seq_len, has_init, emit_state, n_prev):
    q_ref, k_ref, v_ref, o_ref, a_ref, b_ref, cm_ref = refs[:7]
    pos = 7
    if has_init:
        c0_ref, n0_ref, m0_ref = refs[pos:pos + 3]
        pos += 3
    if n_prev:
        cprev_ref, nprev_ref, mprev_ref = refs[pos:pos + 3]
        pos += 3
    ya_ref = refs[pos]
    pos += 1
    if emit_state:
        cout_ref, nout_ref, mout_ref = refs[pos:pos + 3]
        pos += 3
    rows_scr, wk16_scr, dec_scr, vt_scr, vtw_scr, cb_scr, n1_scr, c_scr = refs[pos:pos + 8]

    n_chunks = seq_len // CHUNK
    unroll = min(4, n_chunks)
    row81 =lax.broadcasted_iota(jnp.int32, (SUBLANES, 1), 0)
    is_fwd1 = row81 == 0
    row_n = lax.broadcasted_iota(jnp.int32, (SUBLANES, HEAD_DIM), 0)

    chunk = lambda ref, c: ref[:, c * CHUNK:(c + 1) * CHUNK]
    a_c = [chunk(a_ref, c) for c in range(n_chunks)]
    b_c = [chunk(b_ref, c) for c in range(n_chunks)]
    cm_c = [chunk(cm_ref, c) for c in range(n_chunks)]
    row8 = lax.broadcasted_iota(jnp.int32, (SUBLANES, CHUNK), 0)
    lane8 = lax.broadcasted_iota(jnp.int32, (SUBLANES, CHUNK), 1)
    last = lane8 == jnp.where(row8 == 0, CHUNK - 1, 0)
    amax_c = [jnp.max(a, axis=1, keepdims=True) for a in a_c]
    bend_c = [jnp.sum(jnp.where(last, b, 0.0), axis=1, keepdims=True) for b in b_c]

    if has_init:
        m_state = jnp.where(is_fwd1, m0_ref[0], jnp.where(row81 == 1, m0_ref[1], 0.0))
        n_init = jnp.where(row_n == 0, n0_ref[0], n0_ref[1])
        c_scr[...] = c0_ref[...]
    else:
        m_state = jnp.zeros((SUBLANES, 1), F32)
        n_init = jnp.zeros((SUBLANES, HEAD_DIM), F32)
        c_scr[...] = jnp.zeros(c_scr.shape, F32)

    m_before, top = [], []
    for j in range(n_chunks):
        jr = n_chunks - 1 - j
        amax = jnp.where(is_fwd1, amax_c[j], amax_c[jr])
        bend = jnp.where(is_fwd1, bend_c[j], bend_c[jr])
        m_before.append(m_state)
        top.append(jnp.maximum(m_state, amax))
        m_state = bend + top[-1]

    for c in range(n_chunks):
        cr = n_chunks - 1 - c
        m_c = jnp.where(is_fwd1, m_before[c], m_before[cr])
        top_c = jnp.where(is_fwd1, top[c], top[cr])
        big_m = jnp.maximum(m_c, cm_c[c])
        w_k = jnp.exp(a_c[c] - top_c)
        rows_scr[c, 0] = a_c[c]
        rows_scr[c, 1] = big_m
        rows_scr[c, 2] = jnp.exp(m_c - big_m)
        rows_scr[c, 3] = jnp.exp(-b_c[c] - big_m)
        rows_scr[c, 4] = w_k
        wk16_scr[c] = jnp.concatenate([w_k, jnp.zeros_like(w_k)], axis=0).astype(BF16)
        dec_scr[c] = jnp.broadcast_to(jnp.exp(m_c - top_c), (SUBLANES, HEAD_DIM))

    def load_k(r0):
        return k_ref[pl.ds(r0, CHUNK), :]

    def pass_a(j, nst):
        c = n_chunks - 1 - j
        r0 = pl.multiple_of(c * CHUNK, CHUNK)
        ks = load_k(r0)
        vt = v_ref[pl.ds(r0, CHUNK), :].astype(F32).T
        w_k = rows_scr[c, 4]
        vt_scr[c] = vt.astype(BF16)
        vtw_scr[c] = (vt * w_k[0:1, :]).astype(BF16)
        c_in = c_scr[1]
        cb_scr[c] = c_in.astype(BF16)
        n1_scr[c] = nst
        dec = dec_scr[c]
        c_scr[1] = dec[1:2, :] * c_in + jnp.dot((vt * w_k[1:2, :]).astype(BF16), ks,
                                                preferred_element_type=F32)
        n_upd = jnp.dot(wk16_scr[c], ks, preferred_element_type=F32)[:SUBLANES]
        return dec * nst + n_upd

    n_rev = lax.fori_loop(0, n_chunks, pass_a, n_init, unroll=unroll)

    row_id = lax.broadcasted_iota(jnp.int32, (CHUNK, CHUNK), 0)
    col_id = lax.broadcasted_iota(jnp.int32, (CHUNK, CHUNK), 1)
    tri = (row_id <= col_id, row_id >= col_id)

    def pass_b(c, nst):
        r0 = pl.multiple_of(c * CHUNK, CHUNK)
        q = q_ref[pl.ds(r0, CHUNK), :]
        ks = load_k(r0)
        vt = vt_scr[c]
        a_r, bigm_r, winter_r, floor_r = (rows_scr[c, i] for i in range(4))
        st_all = lax.dot_general(ks, q, NT_DIMS, preferred_element_type=F32)
        n_rows = jnp.where(row_n == 0, nst, n1_scr[c])
        n_rows = jnp.concatenate([n_rows, jnp.zeros_like(n_rows)], axis=0).astype(BF16)
        qn = lax.dot_general(n_rows, q, NT_DIMS, preferred_element_type=F32)
        c0 = c_scr[0]
        sts, invs = [], []
        for d in range(2):
            row = slice(d, d + 1)
            a_st = jnp.broadcast_to(a_r[row, :], (CHUNK, CHUNK)).T
            st = st_all * jnp.exp(jnp.where(tri[d], a_st - bigm_r[row, :], -jnp.inf))
            den = jnp.sum(st, axis=0, keepdims=True) + winter_r[row, :] * qn[row, :]
            invs.append(1.0 / jnp.maximum(jnp.abs(den), floor_r[row, :]))
            sts.append(st.astype(BF16))
        intra = jnp.dot(vt, jnp.concatenate(sts, axis=1), preferred_element_type=F32)
        ht = None
        for d in range(2):
            row = slice(d, d + 1)
            cb = c0.astype(BF16) if d == 0 else cb_scr[c]
            num_t = (intra[:, d * CHUNK:(d + 1) * CHUNK]
                     + winter_r[row, :] * lax.dot_general(cb, q, NT_DIMS,
                                                          preferred_element_type=F32))
            ht = num_t * invs[d] if ht is None else ht + num_t * invs[d]
        gate = jax.nn.sigmoid(o_ref[pl.ds(r0, CHUNK), :].astype(F32))
        ya_ref[pl.ds(r0, CHUNK), :] = (gate * ht.T).astype(ya_ref.dtype)

        dec = dec_scr[c]
        c_scr[0] = dec[0:1, :] * c0 + jnp.dot(vtw_scr[c], ks, preferred_element_type=F32)
        n_upd = jnp.dot(wk16_scr[c], ks, preferred_element_type=F32)[:SUBLANES]
        return dec * nst + n_upd

    n_fwd = lax.fori_loop(0, n_chunks, pass_b, n_init, unroll=unroll)

    if emit_state:
        if n_prev:
            cout_ref[:n_prev] = cprev_ref[...]
            nout_ref[:n_prev] = nprev_ref[...]
            mout_ref[:n_prev] = mprev_ref[...]
        cout_ref[n_prev] = c_scr[...]
        nout_ref[n_prev, 0] = n_fwd[0:1, :]
        nout_ref[n_prev, 1] = n_rev[1:2, :]
        mout_ref[n_prev, 0] = m_state[0:1, :]
        mout_ref[n_prev, 1] = m_state[1:2, :]


def _mlstm(z, scans, tok0, n_batch, seq_len, init=None, layer=0, prev=None):
    blk0 = tok0 // seq_len
    n_chunks = seq_len // CHUNK
    has_init = init is not None
    emit_state = not has_init
    n_prev = 0 if prev is None else prev[0].shape[1]

    def zspec(col):
        return pl.BlockSpec((seq_len, HEAD_DIM),
                            lambda b, h: (blk0 + b, col * N_HEADS + h))

    scan_spec = pl.BlockSpec((SUBLANES, seq_len), lambda b, h: (h, blk0 + b))
    in_specs = [zspec(COL_Q), zspec(COL_K), zspec(COL_V), zspec(COL_O),
                scan_spec, scan_spec, scan_spec]
    args = [z, z, z, z, *scans]

    def state_specs(n_layers):
        return [
            pl.BlockSpec((None, n_layers, 2, None, HEAD_DIM, HEAD_DIM),
                         lambda b, h: (b, 0, 0, h, 0, 0)),
            pl.BlockSpec((None, n_layers, 2, None, 1, HEAD_DIM), lambda b, h: (b, 0, 0, h, 0, 0)),
            pl.BlockSpec((None, n_layers, 2, None, 1, 1), lambda b, h: (b, 0, 0, h, 0, 0)),
        ]

    if has_init:
        sc, sn, sm = init
        nb, depth = sc.shape[:2]
        in_specs += [
            pl.BlockSpec((None, None, 2, None, HEAD_DIM, HEAD_DIM),
                         lambda b, h: (b, layer, 0, h, 0, 0)),
            pl.BlockSpec((None, None, 2, None, 1, HEAD_DIM), lambda b, h: (b, layer, 0, h, 0, 0)),
            pl.BlockSpec((None, None, 2, None, 1, 1), lambda b, h: (b, layer, 0, h, 0, 0)),
        ]
        args += [sc, sn.reshape(nb, depth, 2, N_HEADS, 1, HEAD_DIM),
                 sm.reshape(nb, depth, 2, N_HEADS, 1, 1)]
    if n_prev:
        in_specs += state_specs(n_prev)
        args += list(prev)
    out_specs = [pl.BlockSpec((seq_len, HEAD_DIM), lambda b, h: (b, h))]
    out_shape = [jax.ShapeDtypeStruct((n_batch * seq_len, D_MODEL), BF16)]
    if emit_state:
        n_out = n_prev + 1
        out_specs += state_specs(n_out)
        out_shape += [
            jax.ShapeDtypeStruct((n_batch, n_out, 2, N_HEADS, HEAD_DIM, HEAD_DIM), F32),
            jax.ShapeDtypeStruct((n_batch, n_out, 2, N_HEADS, 1, HEAD_DIM), F32),
            jax.ShapeDtypeStruct((n_batch, n_out, 2, N_HEADS, 1, 1), F32),
        ]
    outs = pl.pallas_call(
        functools.partial(_mlstm_kernel, seq_len=seq_len, has_init=has_init, emit_state=emit_state,
                          n_prev=n_prev),
        grid=(n_batch, N_HEADS),
        in_specs=in_specs,
        out_specs=out_specs,
        out_shape=out_shape,
        scratch_shapes=[
            pltpu.VMEM((n_chunks, 5, SUBLANES, CHUNK), F32),
            pltpu.VMEM((n_chunks, 2 * SUBLANES, CHUNK), BF16),
            pltpu.VMEM((n_chunks, SUBLANES, HEAD_DIM), F32),
            pltpu.VMEM((n_chunks, HEAD_DIM, CHUNK), BF16),
            pltpu.VMEM((n_chunks, HEAD_DIM, CHUNK), BF16),
            pltpu.VMEM((n_chunks, HEAD_DIM, HEAD_DIM), BF16),
            pltpu.VMEM((n_chunks, SUBLANES, HEAD_DIM), F32),
            pltpu.VMEM((2, HEAD_DIM, HEAD_DIM), F32),
        ],
        compiler_params=_params(("arbitrary", "arbitrary")),
        name="mlstm_ctx" if emit_state else "mlstm_lat",
    )(*args)
    if emit_state:
        return outs[0], tuple(outs[1:])
    return outs[0], None


TOK_TILE = 256
POOL_HALO = 64


def _mix_merge_kernel(*refs, n_x, n_first, n_ctx_tok, ctx_len, lat_len):
    x_refs = refs[:n_x]
    (ya_ctx_ref, ya_lat_ref, pp_ref, pc_ref, pn_ref, u_ref, vg_ref, m0_ref, m1_ref, m2_ref,
     mod_ref, nw_ref, band_ref, pw_ref, ps_ref, gnw_ref, ws_ref, gb_ref, wbr_ref, wout_ref,
     o_ref) = refs[n_x:]
    tok0 = pl.program_id(0) * TOK_TILE
    first = pl.program_id(0) < n_first
    is_ctx = tok0 < n_ctx_tok
    seq_len = jnp.where(is_ctx, ctx_len, lat_len)
    base = jnp.where(is_ctx, 0, n_ctx_tok)
    seq_start = base + ((tok0 - base) // seq_len) * seq_len
    seq_end = seq_start + seq_len

    t_col = tok0 + lax.broadcasted_iota(jnp.int32, (TOK_TILE, 1), 0)
    p_prev = jnp.where(tok0 > seq_start, pp_ref[...], jnp.zeros_like(pp_ref))
    p_next = jnp.where(tok0 + TOK_TILE < seq_end, pn_ref[...], jnp.zeros_like(pn_ref))
    p_ext = jnp.concatenate([p_prev, pc_ref[...], p_next], axis=0)
    yb = []
    for g, win in enumerate(POOL_WINDOWS):
        half = win // 2
        sl = slice(g * GROUP_DIM, (g + 1) * GROUP_DIM)
        acc = jnp.dot(band_ref[g], p_ext[:, sl], preferred_element_type=F32)
        count = (jnp.minimum(t_col + half, seq_end) - jnp.maximum(t_col - half, seq_start)).astype(F32)
        y = (acc / count - pc_ref[:, sl].astype(F32)).astype(BF16)
        yb.append(jnp.dot(y, pw_ref[g], preferred_element_type=F32) * ps_ref[:, sl])
    yb = jnp.concatenate(yb, axis=1).astype(BF16)

    u = jax.nn.gelu(u_ref[...].astype(F32))
    vg = _rms(jax.nn.gelu(vg_ref[...].astype(F32)), gnw_ref[...]).astype(BF16)
    yc = []
    for ch in range(TOK_TILE // CHUNK):
        rows = slice(ch * CHUNK, (ch + 1) * CHUNK)
        parts = []
        for g in range(N_GROUPS):
            sl = slice(g * GROUP_DIM, (g + 1) * GROUP_DIM)
            s = jnp.dot(ws_ref[g], vg[rows, sl], preferred_element_type=F32) + gb_ref[:, g:g + 1]
            parts.append(u[rows, sl] * s)
        yc.append(jnp.concatenate(parts, axis=1))
    yc = jnp.concatenate(yc, axis=0).astype(BF16)

    ys = (_pick((ya_ctx_ref, ya_lat_ref), first), yb, yc)
    mixed = None
    for n, (y, m_ref) in enumerate(zip(ys, (m0_ref, m1_ref, m2_ref))):
        br = jnp.dot(y, wbr_ref[n], preferred_element_type=F32)
        term = jax.nn.sigmoid(m_ref[...].astype(F32)) * br
        mixed = term if mixed is None else mixed + term
    o = jnp.dot(mixed.astype(BF16), wout_ref[...], preferred_element_type=F32)
    o_ref[...] = _pick(x_refs, first) + mod_ref[0, 2:3, :] * _rms(o, nw_ref[...])


FFN_TILE = 512
FFN_SLAB = 256


def _ffn_kernel(xp_ref, xc_ref, xn_ref, mod_ref, nw2_ref, nw3_ref, wup_ref, cw_ref, cb_ref, wd_ref,
                *rest, n_ctx_tok, ctx_len, lat_len):
    out_refs = rest
    i = pl.program_id(0)
    tok0 = i * FFN_TILE
    is_ctx = tok0 < n_ctx_tok
    tiles_per_lat = lat_len // FFN_TILE
    lat_tile = jnp.maximum(tok0 - n_ctx_tok, 0) // FFN_TILE % tiles_per_lat
    has_up = jnp.logical_and(jnp.logical_not(is_ctx), lat_tile != 0)
    has_dn = jnp.logical_and(jnp.logical_not(is_ctx), lat_tile != tiles_per_lat - 1)
    ext_rows = FFN_TILE + 2 * GRID_W
    r = lax.broadcasted_iota(jnp.int32, (FFN_TILE, FFN_SLAB), 0)
    col_ctx = r % ctx_len
    col_lat = r % GRID_W
    keep_left = jnp.where(jnp.where(is_ctx, col_ctx, col_lat) != 0, 1.0, 0.0)
    keep_right = jnp.where(jnp.where(is_ctx, col_ctx - (ctx_len - 1), col_lat - (GRID_W - 1)) != 0,
                           1.0, 0.0)
    r1 = lax.broadcasted_iota(jnp.int32, (ext_rows, 1), 0)
    in_grid = jnp.logical_and(jnp.logical_or(r1 >= GRID_W, has_up),
                              jnp.logical_or(r1 < GRID_W + FFN_TILE, has_dn))
    tap_on = jnp.where(lax.broadcasted_iota(jnp.int32, (9, 1), 0) // 3 == 1, 1.0,
                       jnp.where(is_ctx, 0.0, 1.0))
    cw = cw_ref[...] * tap_on

    x_ext = jnp.concatenate([xp_ref[...], xc_ref[...], xn_ref[...]], axis=0)
    h_ext = jnp.where(in_grid, _modnorm(x_ext, mod_ref, nw2_ref, 3, 4), 0.0).astype(BF16)
    h_cur = h_ext[GRID_W:GRID_W + FFN_TILE]

    ext_all = jnp.dot(h_ext, wup_ref[:, :D_FF], preferred_element_type=F32)
    gate_all = jnp.dot(h_cur, wup_ref[:, D_FF:], preferred_element_type=F32)

    u = []
    for c in range(D_FF // FFN_SLAB):
        sl = slice(c * FFN_SLAB, (c + 1) * FFN_SLAB)
        ext = ext_all[:, sl]
        q = []
        for dx in range(3):
            qs = None
            for dy in range(3):
                term = cw[dy * 3 + dx:dy * 3 + dx + 1, sl] * ext[dy * GRID_W:dy * GRID_W + FFN_TILE, :]
                qs = term if qs is None else qs + term
            q.append(qs)
        acc = (q[1] + cb_ref[:, sl] + pltpu.roll(q[0], 1, 0) * keep_left
               + pltpu.roll(q[2], FFN_TILE - 1, 0) * keep_right)
        u.append((jax.nn.gelu(acc) * gate_all[:, sl]).astype(BF16))
    f = jnp.dot(jnp.concatenate(u, axis=1), wd_ref[...], preferred_element_type=F32)

    y = xc_ref[...] + mod_ref[0, 5:6, :] * _rms(f, nw3_ref[...])
    if len(out_refs) == 1:
        out_refs[0][...] = y
    else:
        @pl.when(is_ctx)
        def _():
            out_refs[0][...] = y

        @pl.when(jnp.logical_not(is_ctx))
        def _():
            out_refs[1][...] = y


def _gate_rows(w_gate):
    idx, keep = [], []
    for p in range(N_HEADS * SUBLANES):
        h, j = divmod(p, SUBLANES)
        d, kind = j % 2, j // 2
        idx.append(d * 2 * N_HEADS + kind * N_HEADS + h if j < 4 else 0)
        keep.append(1.0 if j < 4 else 0.0)
    return w_gate[..., jnp.array(idx)] * jnp.array(keep, w_gate.dtype)


def kernel(x_prompt, x_sample, state_C, state_n, state_m, c, c_ctx, w_mod, b_mod, norm_w, w_in, b_in,
           mlstm_gate_b, pool_w, pool_scale, gmlp_norm_w, gmlp_ws, gmlp_b, w_br, w_out, w_up, conv_w,
           conv_b, w_down):
    n_ctx, ctx_len, d = x_prompt.shape
    n_lat, lat_len, _ = x_sample.shape
    depth = w_in.shape[0]
    n_ctx_tok = n_ctx * ctx_len
    n_lat_tok = n_lat * lat_len
    n_tok = n_ctx_tok + n_lat_tok
    assert d == D_MODEL and FFN_TILE % ctx_len == 0 and lat_len % FFN_TILE == 0
    assert n_ctx_tok % FFN_TILE == 0 and ctx_len % TOK_TILE == 0 and lat_len % TOK_TILE == 0
    n_gate = 2 * 2 * N_HEADS
    gate0 = 4 * D_MODEL

    x_parts = (x_prompt.reshape(n_ctx_tok, d), x_sample.reshape(n_lat_tok, d))

    def tok_specs(n_parts, tm, tile=lambda i: i):
        if n_parts == 1:
            return [pl.BlockSpec((tm, d), lambda i, *_: (tile(i), 0))]
        n_a = n_ctx_tok // tm
        return [pl.BlockSpec((tm, d), lambda i, *_: (jnp.minimum(tile(i), n_a - 1), 0)),
                pl.BlockSpec((tm, d), lambda i, *_: (jnp.maximum(tile(i) - n_a, 0), 0))]

    cvec = jnp.zeros((SUBLANES, d), F32).at[0].set(c_ctx).at[1:1 + n_lat].set(c)
    mod = _modulation(cvec, w_mod, b_mod)

    def mod_spec(tm):
        def row(i):
            t0 = i * tm
            return jnp.where(t0 < n_ctx_tok, 0, 1 + jnp.maximum(t0 - n_ctx_tok, 0) // lat_len)
        return row

    tn_in = 2048
    n_col_tiles = MAIN_WIDTH // tn_in
    k_scale = jnp.ones((MAIN_WIDTH,), F32).at[COL_K * d:(COL_K + 1) * d].set(HEAD_DIM ** -0.5)
    w_in16 = w_in.astype(BF16)
    w_main = (jnp.concatenate([w_in16[:, :, :gate0], w_in16[:, :, gate0 + n_gate:]], axis=2)
              * k_scale.astype(BF16))
    b_main = (jnp.concatenate([b_in[:, :gate0], b_in[:, gate0 + n_gate:]], axis=1)
              * k_scale).reshape(depth, 1, MAIN_WIDTH)
    row_tok = mod_spec(TOK_TILE)
    row_ffn = mod_spec(FFN_TILE)
    vec = lambda a: a.reshape(1, -1)
    states = None
    t_idx = np.arange(TOK_TILE)[:, None]
    u_idx = np.arange(TOK_TILE + 2 * POOL_HALO)[None, :] - POOL_HALO
    pool_bands = jnp.asarray(np.stack([(u_idx >= t_idx - w // 2) & (u_idx < t_idx + w // 2)
                                       for w in POOL_WINDOWS]), BF16)
    n_grow = N_HEADS * SUBLANES
    scan_lanes = 2048

    for l in range(depth):
        last_layer = l == depth - 1
        wg_rows = _gate_rows(w_in[l, :, gate0:gate0 + n_gate]).T.astype(BF16)
        bga_rows = _gate_rows(b_in[l, gate0:gate0 + n_gate]).reshape(-1, 1)
        bgb_rows = _gate_rows(mlstm_gate_b[l].reshape(n_gate)).reshape(-1, 1)
        mod_l = mod[l]
        nw = norm_w[l]
        tm_proj = 1024 if len(x_parts) == 2 else 2048
        assert n_ctx_tok % tm_proj == 0 and lat_len % tm_proj == 0
        row_in = mod_spec(tm_proj)

        z, gt = pl.pallas_call(
            functools.partial(_inproj_kernel, n_x=len(x_parts), n_first=n_ctx_tok // tm_proj),
            grid=(n_tok // tm_proj, n_col_tiles),
            in_specs=tok_specs(len(x_parts), tm_proj) + [
                pl.BlockSpec((1, N_MOD, d), lambda i, j: (row_in(i), 0, 0)),
                pl.BlockSpec((1, d), lambda i, j: (0, 0)),
                pl.BlockSpec((None, d, tn_in), lambda i, j, l=l: (l, 0, j)),
                pl.BlockSpec((None, 1, tn_in), lambda i, j, l=l: (l, 0, j)),
                pl.BlockSpec((n_grow, d), lambda i, j: (0, 0)),
                pl.BlockSpec((n_grow, 1), lambda i, j: (0, 0)),
                pl.BlockSpec((n_grow, 1), lambda i, j: (0, 0)),
            ],
            out_specs=[
                pl.BlockSpec((tm_proj, tn_in), lambda i, j: (i, j)),
                pl.BlockSpec((n_grow, tm_proj), lambda i, j: (0, i)),
            ],
            out_shape=[
                jax.ShapeDtypeStruct((n_tok, MAIN_WIDTH), BF16),
                jax.ShapeDtypeStruct((n_grow, n_tok), F32),
            ],
            scratch_shapes=[pltpu.VMEM((tm_proj, d), BF16)],
            compiler_params=_params(("arbitrary", "arbitrary")),
            name="inproj",
        )(*x_parts, mod_l, vec(nw[0]), w_main, b_main, wg_rows, bga_rows, bgb_rows)

        scan_spec = pl.BlockSpec((n_grow, scan_lanes), lambda i: (0, i))
        scans = pl.pallas_call(
            _gates_kernel,
            grid=(n_tok // scan_lanes,),
            in_specs=[scan_spec],
            out_specs=[scan_spec] * 3,
            out_shape=[jax.ShapeDtypeStruct((n_grow, n_tok), F32)] * 3,
            compiler_params=_params(("arbitrary",)),
            name="gates",
        )(gt)

        ya_ctx, states = _mlstm(z, scans, 0, n_ctx, ctx_len, prev=states)
        ya_lat, _ = _mlstm(z, scans, n_ctx_tok, n_lat, lat_len,
                           init=(state_C, state_n, state_m), layer=l)

        tm = TOK_TILE
        halo_per_tile = tm // POOL_HALO
        n_halo = n_tok // POOL_HALO
        zcol = lambda col: pl.BlockSpec((tm, d), lambda i: (i, col))
        full = lambda shape: pl.BlockSpec(shape, lambda i: (0,) * len(shape))
        x = pl.pallas_call(
            functools.partial(_mix_merge_kernel, n_x=len(x_parts), n_first=n_ctx_tok // tm,
                              n_ctx_tok=n_ctx_tok, ctx_len=ctx_len, lat_len=lat_len),
            grid=(n_tok // tm,),
            in_specs=tok_specs(len(x_parts), tm) + tok_specs(2, tm) + [
                pl.BlockSpec((POOL_HALO, d),
                             lambda i: (jnp.maximum(i * halo_per_tile - 1, 0), COL_P)),
                zcol(COL_P),
                pl.BlockSpec((POOL_HALO, d),
                             lambda i: (jnp.minimum((i + 1) * halo_per_tile, n_halo - 1), COL_P)),
                zcol(COL_U), zcol(COL_VG), zcol(COL_M), zcol(COL_M + 1), zcol(COL_M + 2),
                pl.BlockSpec((1, N_MOD, d), lambda i: (row_tok(i), 0, 0)),
                full((1, d)),
                full(pool_bands.shape),
                full((N_GROUPS, GROUP_DIM, GROUP_DIM)), full((1, d)), full((1, d)),
                full((N_GROUPS, CHUNK, CHUNK)), full((CHUNK, N_GROUPS)),
                full((3, d, d)), full((d, d)),
            ],
            out_specs=pl.BlockSpec((tm, d), lambda i: (i, 0)),
            out_shape=jax.ShapeDtypeStruct((n_tok, d), F32),
            compiler_params=_params(("arbitrary",)),
            name="mix_merge",
        )(*x_parts, ya_ctx, ya_lat, z, z, z, z, z, z, z, z, mod_l, vec(nw[1]),
          pool_bands, pool_w[l].astype(BF16), vec(pool_scale[l]), vec(gmlp_norm_w[l]),
          gmlp_ws[l].astype(BF16), gmlp_b[l].T, w_br[l].astype(BF16), w_out[l].astype(BF16))
        x_parts = (x,)

        tf = FFN_TILE
        rows_per_tile = tf // GRID_W
        n_rows = n_tok // GRID_W
        n_ctx_tiles = n_ctx_tok // tf
        if last_layer:
            out_specs = [pl.BlockSpec((tf, d), lambda i: (jnp.minimum(i, n_ctx_tiles - 1), 0)),
                         pl.BlockSpec((tf, d), lambda i: (jnp.maximum(i - n_ctx_tiles, 0), 0))]
            out_shape = [jax.ShapeDtypeStruct((n_ctx_tok, d), F32),
                         jax.ShapeDtypeStruct((n_lat_tok, d), F32)]
        else:
            out_specs = [pl.BlockSpec((tf, d), lambda i: (i, 0))]
            out_shape = [jax.ShapeDtypeStruct((n_tok, d), F32)]
        x_parts = tuple(pl.pallas_call(
            functools.partial(_ffn_kernel, n_ctx_tok=n_ctx_tok, ctx_len=ctx_len, lat_len=lat_len),
            grid=(n_tok // tf,),
            in_specs=[
                pl.BlockSpec((GRID_W, d), lambda i: (jnp.maximum(i * rows_per_tile - 1, 0), 0)),
                pl.BlockSpec((tf, d), lambda i: (i, 0)),
                pl.BlockSpec((GRID_W, d),
                             lambda i: (jnp.minimum((i + 1) * rows_per_tile, n_rows - 1), 0)),
                pl.BlockSpec((1, N_MOD, d), lambda i: (row_ffn(i), 0, 0)),
                full((1, d)), full((1, d)), full((d, 2 * D_FF)),
                full((9, D_FF)), full((1, D_FF)), full((D_FF, d)),
            ],
            out_specs=out_specs,
            out_shape=out_shape,
            compiler_params=_params(("arbitrary",)),
            name="ffn",
        )(x, x, x, mod_l, vec(nw[2]), vec(nw[3]), w_up[l].astype(BF16),
          conv_w[l].reshape(9, D_FF), vec(conv_b[l]), w_down[l].astype(BF16)))

    new_c, new_n, new_m = states
    return (x_parts[0].reshape(n_ctx, ctx_len, d), x_parts[1].reshape(n_lat, lat_len, d), new_c,
            new_n.reshape(n_ctx, depth, 2, N_HEADS, HEAD_DIM), new_m.reshape(n_ctx, depth, 2, N_HEADS))
```

```python
import functools

import jax
import jax.numpy as jnp
import numpy as np
from jax import lax
from jax.experimental import pallas as pl
from jax.experimental.pallas import tpu as pltpu

F32 = jnp.float32
BF16 = jnp.bfloat16

D_MODEL = 1024
N_HEADS = 4
HEAD_DIM = D_MODEL // N_HEADS
CHUNK = 128
POOL_WINDOWS = (2, 4, 8, 16)
N_GROUPS = 4
GROUP_DIM = D_MODEL // N_GROUPS
D_FF = 2816
GRID_W = 64
N_MOD = 6
RMS_EPS = 1e-6
LANES = 128
SUBLANES = 8
VMEM_LIMIT = 56 * 1024 * 1024

COL_Q, COL_K, COL_V, COL_O, COL_P, COL_U, COL_VG, COL_M = 0, 1, 2, 3, 4, 5, 6, 7
MAIN_WIDTH = 10 * D_MODEL

NT_DIMS = (((1,), (1,)), ((), ()))


def _rms(x, w):
    return x * lax.rsqrt(jnp.mean(x * x, axis=-1, keepdims=True) + RMS_EPS) * w


def _log_sigmoid(x):
    return jnp.minimum(x, 0.0) - jnp.log1p(jnp.exp(-jnp.abs(x)))


def _params(sem):
    return pltpu.CompilerParams(dimension_semantics=sem, vmem_limit_bytes=VMEM_LIMIT)


def _mod_kernel(c_ref, w_ref, b_ref, o_ref):
    c = c_ref[...]
    s = (c * jax.nn.sigmoid(c)).astype(BF16)
    o_ref[...] = jnp.dot(s, w_ref[...].astype(BF16), preferred_element_type=F32) + b_ref[...]


def _modulation(cvec, w_mod, b_mod):
    depth = w_mod.shape[0]
    n = N_MOD * D_MODEL
    tn = 1536
    out = pl.pallas_call(
        _mod_kernel,
        grid=(depth, n // tn),
        in_specs=[
            pl.BlockSpec((SUBLANES, D_MODEL), lambda l, j: (0, 0)),
            pl.BlockSpec((None, D_MODEL, tn), lambda l, j: (l, 0, j)),
            pl.BlockSpec((None, 1, tn), lambda l, j: (l, 0, j)),
        ],
        out_specs=pl.BlockSpec((None, SUBLANES, tn), lambda l, j: (l, 0, j)),
        out_shape=jax.ShapeDtypeStruct((depth, SUBLANES, n), F32),
        compiler_params=_params(("arbitrary", "arbitrary")),
        name="modulation",
    )(cvec, w_mod, b_mod.reshape(depth, 1, n))
    return out.reshape(depth, SUBLANES, N_MOD, D_MODEL)


def _modnorm(x, mod_ref, nw_ref, k_shift, k_scale):
    h = _rms(x, nw_ref[...])
    return h * (1.0 + mod_ref[0, k_scale:k_scale + 1, :]) + mod_ref[0, k_shift:k_shift + 1, :]


def _pick(refs, first):
    if len(refs) == 1:
        return refs[0][...]
    return jnp.where(first, refs[0][...], refs[1][...])


def _inproj_kernel(*refs, n_x, n_first):
    x_refs = refs[:n_x]
    mod_ref, nw_ref, w_ref, b_ref, wgt_ref, bgta_ref, bgtb_ref, z_ref, gt_ref, h_scr = refs[n_x:]

    @pl.when(pl.program_id(1) == 0)
    def _():
        x = _pick(x_refs, pl.program_id(0) < n_first)
        hb = _modnorm(x, mod_ref, nw_ref, 0, 1).astype(BF16)
        h_scr[...] = hb
        gt_ref[...] = (lax.dot_general(wgt_ref[...], hb, NT_DIMS, preferred_element_type=F32)
                       + bgta_ref[...] + bgtb_ref[...])

    z = jnp.dot(h_scr[...], w_ref[...], preferred_element_type=F32) + b_ref[...]
    z_ref[...] = z.astype(z_ref.dtype)


def _gates_kernel(g_ref, a_ref, b_ref, cm_ref):
    g = g_ref[...]
    fwd = lax.broadcasted_iota(jnp.int32, g.shape, 0) % SUBLANES == 0
    lane = lax.broadcasted_iota(jnp.int32, g.shape, 1) % CHUNK
    lf = _log_sigmoid(pltpu.roll(g, g.shape[0] - 2, 0))
    pre, suf = _chunk_scans(lf, jnp.add, 0.0, lane)
    b = jnp.where(fwd, pre, suf)
    a = g - b
    pre, suf = _chunk_scans(a, jnp.maximum, -jnp.inf, lane)
    a_ref[...] = a
    b_ref[...] = b
    cm_ref[...] = jnp.where(fwd, pre, suf)


def _chunk_scans(x, op, fill, lane):
    width = x.shape[1]
    fwd = bwd = x
    k = 1
    while k < CHUNK:
        fwd = op(fwd, jnp.where(lane >= k, pltpu.roll(fwd, k, 1), fill))
        bwd = op(bwd, jnp.where(lane < CHUNK - k, pltpu.roll(bwd, width - k, 1), fill))
        k *= 2
    return fwd, bwd


def _mlstm_kernel(*refs, seq_len, has_init, emit_state, n_prev):
    q_ref, k_ref, v_ref, o_ref, a_ref, b_ref, cm_ref = refs[:7]
    pos = 7
    if has_init:
        c0_ref, n0_ref, m0_ref = refs[pos:pos + 3]
        pos += 3
    if n_prev:
        cprev_ref, nprev_ref, mprev_ref = refs[pos:pos + 3]
        pos += 3
    ya_ref = refs[pos]
    pos += 1
    if emit_state:
        cout_ref, nout_ref, mout_ref = refs[pos:pos + 3]
        pos += 3
    rows_scr, wk16_scr, dec_scr, vt_scr, vtw_scr, cb_scr, n1_scr, c_scr = refs[pos:pos + 8]

    n_chunks = seq_len // CHUNK
    unroll = min(4, n_chunks)
    row81 =lax.broadcasted_iota(jnp.int32, (SUBLANES, 1), 0)
    is_fwd1 = row81 == 0
    row_n = lax.broadcasted_iota(jnp.int32, (SUBLANES, HEAD_DIM), 0)

    chunk = lambda ref, c: ref[:, c * CHUNK:(c + 1) * CHUNK]
    a_c = [chunk(a_ref, c) for c in range(n_chunks)]
    b_c = [chunk(b_ref, c) for c in range(n_chunks)]
    cm_c = [chunk(cm_ref, c) for c in range(n_chunks)]
    row8 = lax.broadcasted_iota(jnp.int32, (SUBLANES, CHUNK), 0)
    lane8 = lax.broadcasted_iota(jnp.int32, (SUBLANES, CHUNK), 1)
    last = lane8 == jnp.where(row8 == 0, CHUNK - 1, 0)
    amax_c = [jnp.max(a, axis=1, keepdims=True) for a in a_c]
    bend_c = [jnp.sum(jnp.where(last, b, 0.0), axis=1, keepdims=True) for b in b_c]

    if has_init:
        m_state = jnp.where(is_fwd1, m0_ref[0], jnp.where(row81 == 1, m0_ref[1], 0.0))
        n_init = jnp.where(row_n == 0, n0_ref[0], n0_ref[1])
        c_scr[...] = c0_ref[...]
    else:
        m_state = jnp.zeros((SUBLANES, 1), F32)
        n_init = jnp.zeros((SUBLANES, HEAD_DIM), F32)
        c_scr[...] = jnp.zeros(c_scr.shape, F32)

    m_before, top = [], []
    for j in range(n_chunks):
        jr = n_chunks - 1 - j
        amax = jnp.where(is_fwd1, amax_c[j], amax_c[jr])
        bend = jnp.where(is_fwd1, bend_c[j], bend_c[jr])
        m_before.append(m_state)
        top.append(jnp.maximum(m_state, amax))
        m_state = bend + top[-1]

    for c in range(n_chunks):
        cr = n_chunks - 1 - c
        m_c = jnp.where(is_fwd1, m_before[c], m_before[cr])
        top_c = jnp.where(is_fwd1, top[c], top[cr])
        big_m = jnp.maximum(m_c, cm_c[c])
        w_k = jnp.exp(a_c[c] - top_c)
        rows_scr[c, 0] = a_c[c]
        rows_scr[c, 1] = big_m
        rows_scr[c, 2] = jnp.exp(m_c - big_m)
        rows_scr[c, 3] = jnp.exp(-b_c[c] - big_m)
        rows_scr[c, 4] = w_k
        wk16_scr[c] = jnp.concatenate([w_k, jnp.zeros_like(w_k)], axis=0).astype(BF16)
        dec_scr[c] = jnp.broadcast_to(jnp.exp(m_c - top_c), (SUBLANES, HEAD_DIM))

    def load_k(r0):
        return k_ref[pl.ds(r0, CHUNK), :]

    def pass_a(j, nst):
        c = n_chunks - 1 - j
        r0 = pl.multiple_of(c * CHUNK, CHUNK)
        ks = load_k(r0)
        vt = v_ref[pl.ds(r0, CHUNK), :].astype(F32).T
        w_k = rows_scr[c, 4]
        vt_scr[c] = vt.astype(BF16)
        vtw_scr[c] = (vt * w_k[0:1, :]).astype(BF16)
        c_in = c_scr[1]
        cb_scr[c] = c_in.astype(BF16)
        n1_scr[c] = nst
        dec = dec_scr[c]
        c_scr[1] = dec[1:2, :] * c_in + jnp.dot((vt * w_k[1:2, :]).astype(BF16), ks,
                                                preferred_element_type=F32)
        n_upd = jnp.dot(wk16_scr[c], ks, preferred_element_type=F32)[:SUBLANES]
        return dec * nst + n_upd

    n_rev = lax.fori_loop(0, n_chunks, pass_a, n_init, unroll=unroll)

    row_id = lax.broadcasted_iota(jnp.int32, (CHUNK, CHUNK), 0)
    col_id = lax.broadcasted_iota(jnp.int32, (CHUNK, CHUNK), 1)
    tri = (row_id <= col_id, row_id >= col_id)

    def pass_b(c, nst):
        r0 = pl.multiple_of(c * CHUNK, CHUNK)
        q = q_ref[pl.ds(r0, CHUNK), :]
        ks = load_k(r0)
        vt = vt_scr[c]
        a_r, bigm_r, winter_r, floor_r = (rows_scr[c, i] for i in range(4))
        st_all = lax.dot_general(ks, q, NT_DIMS, preferred_element_type=F32)
        n_rows = jnp.where(row_n == 0, nst, n1_scr[c])
        n_rows = jnp.concatenate([n_rows, jnp.zeros_like(n_rows)], axis=0).astype(BF16)
        qn = lax.dot_general(n_rows, q, NT_DIMS, preferred_element_type=F32)
        c0 = c_scr[0]
        sts, invs = [], []
        for d in range(2):
            row = slice(d, d + 1)
            a_st = jnp.broadcast_to(a_r[row, :], (CHUNK, CHUNK)).T
            st = st_all * jnp.exp(jnp.where(tri[d], a_st - bigm_r[row, :], -jnp.inf))
            den = jnp.sum(st, axis=0, keepdims=True) + winter_r[row, :] * qn[row, :]
            invs.append(1.0 / jnp.maximum(jnp.abs(den), floor_r[row, :]))
            sts.append(st.astype(BF16))
        intra = jnp.dot(vt, jnp.concatenate(sts, axis=1), preferred_element_type=F32)
        ht = None
        for d in range(2):
            row = slice(d, d + 1)
            cb = c0.astype(BF16) if d == 0 else cb_scr[c]
            num_t = (intra[:, d * CHUNK:(d + 1) * CHUNK]
                     + winter_r[row, :] * lax.dot_general(cb, q, NT_DIMS,
                                                          preferred_element_type=F32))
            ht = num_t * invs[d] if ht is None else ht + num_t * invs[d]
        gate = jax.nn.sigmoid(o_ref[pl.ds(r0, CHUNK), :].astype(F32))
        ya_ref[pl.ds(r0, CHUNK), :] = (gate * ht.T).astype(ya_ref.dtype)

        dec = dec_scr[c]
        c_scr[0] = dec[0:1, :] * c0 + jnp.dot(vtw_scr[c], ks, preferred_element_type=F32)
        n_upd = jnp.dot(wk16_scr[c], ks, preferred_element_type=F32)[:SUBLANES]
        return dec * nst + n_upd

    n_fwd = lax.fori_loop(0, n_chunks, pass_b, n_init, unroll=unroll)

    if emit_state:
        if n_prev:
            cout_ref[:n_prev] = cprev_ref[...]
            nout_ref[:n_prev] = nprev_ref[...]
            mout_ref[:n_prev] = mprev_ref[...]
        cout_ref[n_prev] = c_scr[...]
        nout_ref[n_prev, 0] = n_fwd[0:1, :]
        nout_ref[n_prev, 1] = n_rev[1:2, :]
        mout_ref[n_prev, 0] = m_state[0:1, :]
        mout_ref[n_prev, 1] = m_state[1:2, :]


def _mlstm(z, scans, tok0, n_batch, seq_len, init=None, layer=0, prev=None):
    blk0 = tok0 // seq_len
    n_chunks = seq_len // CHUNK
    has_init = init is not None
    emit_state = not has_init
    n_prev = 0 if prev is None else prev[0].shape[1]

    def zspec(col):
        return pl.BlockSpec((seq_len, HEAD_DIM),
                            lambda b, h: (blk0 + b, col * N_HEADS + h))

    scan_spec = pl.BlockSpec((SUBLANES, seq_len), lambda b, h: (h, blk0 + b))
    in_specs = [zspec(COL_Q), zspec(COL_K), zspec(COL_V), zspec(COL_O),
                scan_spec, scan_spec, scan_spec]
    args = [z, z, z, z, *scans]

    def state_specs(n_layers):
        return [
            pl.BlockSpec((None, n_layers, 2, None, HEAD_DIM, HEAD_DIM),
                         lambda b, h: (b, 0, 0, h, 0, 0)),
            pl.BlockSpec((None, n_layers, 2, None, 1, HEAD_DIM), lambda b, h: (b, 0, 0, h, 0, 0)),
            pl.BlockSpec((None, n_layers, 2, None, 1, 1), lambda b, h: (b, 0, 0, h, 0, 0)),
        ]

    if has_init:
        sc, sn, sm = init
        nb, depth = sc.shape[:2]
        in_specs += [
            pl.BlockSpec((None, None, 2, None, HEAD_DIM, HEAD_DIM),
                         lambda b, h: (b, layer, 0, h, 0, 0)),
            pl.BlockSpec((None, None, 2, None, 1, HEAD_DIM), lambda b, h: (b, layer, 0, h, 0, 0)),
            pl.BlockSpec((None, None, 2, None, 1, 1), lambda b, h: (b, layer, 0, h, 0, 0)),
        ]
        args += [sc, sn.reshape(nb, depth, 2, N_HEADS, 1, HEAD_DIM),
                 sm.reshape(nb, depth, 2, N_HEADS, 1, 1)]
    if n_prev:
        in_specs += state_specs(n_prev)
        args += list(prev)
    out_specs = [pl.BlockSpec((seq_len, HEAD_DIM), lambda b, h: (b, h))]
    out_shape = [jax.ShapeDtypeStruct((n_batch * seq_len, D_MODEL), BF16)]
    if emit_state:
        n_out = n_prev + 1
        out_specs += state_specs(n_out)
        out_shape += [
            jax.ShapeDtypeStruct((n_batch, n_out, 2, N_HEADS, HEAD_DIM, HEAD_DIM), F32),
            jax.ShapeDtypeStruct((n_batch, n_out, 2, N_HEADS, 1, HEAD_DIM), F32),
            jax.ShapeDtypeStruct((n_batch, n_out, 2, N_HEADS, 1, 1), F32),
        ]
    outs = pl.pallas_call(
        functools.partial(_mlstm_kernel, seq_len=seq_len, has_init=has_init, emit_state=emit_state,
                          n_prev=n_prev),
        grid=(n_batch, N_HEADS),
        in_specs=in_specs,
        out_specs=out_specs,
        out_shape=out_shape,
        scratch_shapes=[
            pltpu.VMEM((n_chunks, 5, SUBLANES, CHUNK), F32),
            pltpu.VMEM((n_chunks, 2 * SUBLANES, CHUNK), BF16),
            pltpu.VMEM((n_chunks, SUBLANES, HEAD_DIM), F32),
            pltpu.VMEM((n_chunks, HEAD_DIM, CHUNK), BF16),
            pltpu.VMEM((n_chunks, HEAD_DIM, CHUNK), BF16),
            pltpu.VMEM((n_chunks, HEAD_DIM, HEAD_DIM), BF16),
            pltpu.VMEM((n_chunks, SUBLANES, HEAD_DIM), F32),
            pltpu.VMEM((2, HEAD_DIM, HEAD_DIM), F32),
        ],
        compiler_params=_params(("arbitrary", "arbitrary")),
        name="mlstm_ctx" if emit_state else "mlstm_lat",
    )(*args)
    if emit_state:
        return outs[0], tuple(outs[1:])
    return outs[0], None


TOK_TILE = 256
POOL_HALO = 64


def _mix_merge_kernel(*refs, n_x, n_first, n_ctx_tok, ctx_len, lat_len):
    x_refs = refs[:n_x]
    (ya_ctx_ref, ya_lat_ref, pp_ref, pc_ref, pn_ref, u_ref, vg_ref, m0_ref, m1_ref, m2_ref,
     mod_ref, nw_ref, band_ref, pw_ref, ps_ref, gnw_ref, ws_ref, gb_ref, wbr32_ref, wout32_ref,
     o_ref, wbr_ref, wout_ref) = refs[n_x:]

    @pl.when(pl.program_id(0) == 0)
    def _():
        for n in range(wbr_ref.shape[0]):
            wbr_ref[n] = wbr32_ref[n].astype(BF16)
        wout_ref[...] = wout32_ref[...].astype(BF16)

    tok0 = pl.program_id(0) * TOK_TILE
    first = pl.program_id(0) < n_first
    is_ctx = tok0 < n_ctx_tok
    seq_len = jnp.where(is_ctx, ctx_len, lat_len)
    base = jnp.where(is_ctx, 0, n_ctx_tok)
    seq_start = base + ((tok0 - base) // seq_len) * seq_len
    seq_end = seq_start + seq_len

    t_col = tok0 + lax.broadcasted_iota(jnp.int32, (TOK_TILE, 1), 0)
    p_prev = jnp.where(tok0 > seq_start, pp_ref[...], jnp.zeros_like(pp_ref))
    p_next = jnp.where(tok0 + TOK_TILE < seq_end, pn_ref[...], jnp.zeros_like(pn_ref))
    p_ext = jnp.concatenate([p_prev, pc_ref[...], p_next], axis=0)
    yb = []
    for g, win in enumerate(POOL_WINDOWS):
        half = win // 2
        sl = slice(g * GROUP_DIM, (g + 1) * GROUP_DIM)
        acc = jnp.dot(band_ref[g], p_ext[:, sl], preferred_element_type=F32)
        count = (jnp.minimum(t_col + half, seq_end) - jnp.maximum(t_col - half, seq_start)).astype(F32)
        y = (acc / count - pc_ref[:, sl].astype(F32)).astype(BF16)
        yb.append(jnp.dot(y, pw_ref[g], preferred_element_type=F32) * ps_ref[:, sl])
    yb = jnp.concatenate(yb, axis=1).astype(BF16)

    u = jax.nn.gelu(u_ref[...].astype(F32))
    vg = _rms(jax.nn.gelu(vg_ref[...].astype(F32)), gnw_ref[...]).astype(BF16)
    yc = []
    for ch in range(TOK_TILE // CHUNK):
        rows = slice(ch * CHUNK, (ch + 1) * CHUNK)
        parts = []
        for g in range(N_GROUPS):
            sl = slice(g * GROUP_DIM, (g + 1) * GROUP_DIM)
            s = jnp.dot(ws_ref[g], vg[rows, sl], preferred_element_type=F32) + gb_ref[:, g:g + 1]
            parts.append(u[rows, sl] * s)
        yc.append(jnp.concatenate(parts, axis=1))
    yc = jnp.concatenate(yc, axis=0).astype(BF16)

    ys = (_pick((ya_ctx_ref, ya_lat_ref), first), yb, yc)
    mixed = None
    for n, (y, m_ref) in enumerate(zip(ys, (m0_ref, m1_ref, m2_ref))):
        br = jnp.dot(y, wbr_ref[n], preferred_element_type=F32)
        term = jax.nn.sigmoid(m_ref[...].astype(F32)) * br
        mixed = term if mixed is None else mixed + term
    o = jnp.dot(mixed.astype(BF16), wout_ref[...], preferred_element_type=F32)
    o_ref[...] = _pick(x_refs, first) + mod_ref[0, 2:3, :] * _rms(o, nw_ref[...])


FFN_TILE = 512
FFN_SLAB = 256


def _ffn_kernel(xp_ref, xc_ref, xn_ref, mod_ref, nw2_ref, nw3_ref, wup_ref, cw_ref, cb_ref, wd_ref,
                *rest, n_ctx_tok, ctx_len, lat_len):
    out_refs = rest
    i = pl.program_id(0)
    tok0 = i * FFN_TILE
    is_ctx = tok0 < n_ctx_tok
    tiles_per_lat = lat_len // FFN_TILE
    lat_tile = jnp.maximum(tok0 - n_ctx_tok, 0) // FFN_TILE % tiles_per_lat
    has_up = jnp.logical_and(jnp.logical_not(is_ctx), lat_tile != 0)
    has_dn = jnp.logical_and(jnp.logical_not(is_ctx), lat_tile != tiles_per_lat - 1)
    ext_rows = FFN_TILE + 2 * GRID_W
    r = lax.broadcasted_iota(jnp.int32, (FFN_TILE, FFN_SLAB), 0)
    col_ctx = r % ctx_len
    col_lat = r % GRID_W
    keep_left = jnp.where(jnp.where(is_ctx, col_ctx, col_lat) != 0, 1.0, 0.0)
    keep_right = jnp.where(jnp.where(is_ctx, col_ctx - (ctx_len - 1), col_lat - (GRID_W - 1)) != 0,
                           1.0, 0.0)
    r1 = lax.broadcasted_iota(jnp.int32, (ext_rows, 1), 0)
    in_grid = jnp.logical_and(jnp.logical_or(r1 >= GRID_W, has_up),
                              jnp.logical_or(r1 < GRID_W + FFN_TILE, has_dn))
    tap_on = jnp.where(lax.broadcasted_iota(jnp.int32, (9, 1), 0) // 3 == 1, 1.0,
                       jnp.where(is_ctx, 0.0, 1.0))
    cw = cw_ref[...] * tap_on

    x_ext = jnp.concatenate([xp_ref[...], xc_ref[...], xn_ref[...]], axis=0)
    h_ext = jnp.where(in_grid, _modnorm(x_ext, mod_ref, nw2_ref, 3, 4), 0.0).astype(BF16)
    h_cur = h_ext[GRID_W:GRID_W + FFN_TILE]

    ext_all = jnp.dot(h_ext, wup_ref[:, :D_FF], preferred_element_type=F32)
    gate_all = jnp.dot(h_cur, wup_ref[:, D_FF:], preferred_element_type=F32)

    u = []
    for c in range(D_FF // FFN_SLAB):
        sl = slice(c * FFN_SLAB, (c + 1) * FFN_SLAB)
        ext = ext_all[:, sl]
        q = []
        for dx in range(3):
            qs = None
            for dy in range(3):
                term = cw[dy * 3 + dx:dy * 3 + dx + 1, sl] * ext[dy * GRID_W:dy * GRID_W + FFN_TILE, :]
                qs = term if qs is None else qs + term
            q.append(qs)
        acc = (q[1] + cb_ref[:, sl] + pltpu.roll(q[0], 1, 0) * keep_left
               + pltpu.roll(q[2], FFN_TILE - 1, 0) * keep_right)
        u.append((jax.nn.gelu(acc) * gate_all[:, sl]).astype(BF16))
    f = jnp.dot(jnp.concatenate(u, axis=1), wd_ref[...], preferred_element_type=F32)

    y = xc_ref[...] + mod_ref[0, 5:6, :] * _rms(f, nw3_ref[...])
    if len(out_refs) == 1:
        out_refs[0][...] = y
    else:
        @pl.when(is_ctx)
        def _():
            out_refs[0][...] = y

        @pl.when(jnp.logical_not(is_ctx))
        def _():
            out_refs[1][...] = y


def _gate_rows(w_gate):
    idx, keep = [], []
    for p in range(N_HEADS * SUBLANES):
        h, j = divmod(p, SUBLANES)
        d, kind = j % 2, j // 2
        idx.append(d * 2 * N_HEADS + kind * N_HEADS + h if j < 4 else 0)
        keep.append(1.0 if j < 4 else 0.0)
    return w_gate[..., jnp.array(idx)] * jnp.array(keep, w_gate.dtype)


def kernel(x_prompt, x_sample, state_C, state_n, state_m, c, c_ctx, w_mod, b_mod, norm_w, w_in, b_in,
           mlstm_gate_b, pool_w, pool_scale, gmlp_norm_w, gmlp_ws, gmlp_b, w_br, w_out, w_up, conv_w,
           conv_b, w_down):
    n_ctx, ctx_len, d = x_prompt.shape
    n_lat, lat_len, _ = x_sample.shape
    depth = w_in.shape[0]
    n_ctx_tok = n_ctx * ctx_len
    n_lat_tok = n_lat * lat_len
    n_tok = n_ctx_tok + n_lat_tok
    assert d == D_MODEL and FFN_TILE % ctx_len == 0 and lat_len % FFN_TILE == 0
    assert n_ctx_tok % FFN_TILE == 0 and ctx_len % TOK_TILE == 0 and lat_len % TOK_TILE == 0
    n_gate = 2 * 2 * N_HEADS
    gate0 = 4 * D_MODEL

    x_parts = (x_prompt.reshape(n_ctx_tok, d), x_sample.reshape(n_lat_tok, d))

    def tok_specs(n_parts, tm, tile=lambda i: i):
        if n_parts == 1:
            return [pl.BlockSpec((tm, d), lambda i, *_: (tile(i), 0))]
        n_a = n_ctx_tok // tm
        return [pl.BlockSpec((tm, d), lambda i, *_: (jnp.minimum(tile(i), n_a - 1), 0)),
                pl.BlockSpec((tm, d), lambda i, *_: (jnp.maximum(tile(i) - n_a, 0), 0))]

    cvec = jnp.zeros((SUBLANES, d), F32).at[0].set(c_ctx).at[1:1 + n_lat].set(c)
    mod = _modulation(cvec, w_mod, b_mod)

    def mod_spec(tm):
        def row(i):
            t0 = i * tm
            return jnp.where(t0 < n_ctx_tok, 0, 1 + jnp.maximum(t0 - n_ctx_tok, 0) // lat_len)
        return row

    tn_in = 2048
    n_col_tiles = MAIN_WIDTH // tn_in
    k_scale = jnp.ones((MAIN_WIDTH,), F32).at[COL_K * d:(COL_K + 1) * d].set(HEAD_DIM ** -0.5)
    w_in16 = w_in.astype(BF16)
    w_main = (jnp.concatenate([w_in16[:, :, :gate0], w_in16[:, :, gate0 + n_gate:]], axis=2)
              * k_scale.astype(BF16))
    b_main = (jnp.concatenate([b_in[:, :gate0], b_in[:, gate0 + n_gate:]], axis=1)
              * k_scale).reshape(depth, 1, MAIN_WIDTH)
    row_tok = mod_spec(TOK_TILE)
    row_ffn = mod_spec(FFN_TILE)
    vec = lambda a: a.reshape(1, -1)
    states = None
    t_idx = np.arange(TOK_TILE)[:, None]
    u_idx = np.arange(TOK_TILE + 2 * POOL_HALO)[None, :] - POOL_HALO
    pool_bands = jnp.asarray(np.stack([(u_idx >= t_idx - w // 2) & (u_idx < t_idx + w // 2)
                                       for w in POOL_WINDOWS]), BF16)
    n_grow = N_HEADS * SUBLANES
    scan_lanes = 2048

    for l in range(depth):
        last_layer = l == depth - 1
        wg_rows = _gate_rows(w_in[l, :, gate0:gate0 + n_gate]).T.astype(BF16)
        bga_rows = _gate_rows(b_in[l, gate0:gate0 + n_gate]).reshape(-1, 1)
        bgb_rows = _gate_rows(mlstm_gate_b[l].reshape(n_gate)).reshape(-1, 1)
        mod_l = mod[l]
        nw = norm_w[l]
        tm_proj = 1024 if len(x_parts) == 2 else 2048
        assert n_ctx_tok % tm_proj == 0 and lat_len % tm_proj == 0
        row_in = mod_spec(tm_proj)

        z, gt = pl.pallas_call(
            functools.partial(_inproj_kernel, n_x=len(x_parts), n_first=n_ctx_tok // tm_proj),
            grid=(n_tok // tm_proj, n_col_tiles),
            in_specs=tok_specs(len(x_parts), tm_proj) + [
                pl.BlockSpec((1, N_MOD, d), lambda i, j: (row_in(i), 0, 0)),
                pl.BlockSpec((1, d), lambda i, j: (0, 0)),
                pl.BlockSpec((None, d, tn_in), lambda i, j, l=l: (l, 0, j)),
                pl.BlockSpec((None, 1, tn_in), lambda i, j, l=l: (l, 0, j)),
                pl.BlockSpec((n_grow, d), lambda i, j: (0, 0)),
                pl.BlockSpec((n_grow, 1), lambda i, j: (0, 0)),
                pl.BlockSpec((n_grow, 1), lambda i, j: (0, 0)),
            ],
            out_specs=[
                pl.BlockSpec((tm_proj, tn_in), lambda i, j: (i, j)),
                pl.BlockSpec((n_grow, tm_proj), lambda i, j: (0, i)),
            ],
            out_shape=[
                jax.ShapeDtypeStruct((n_tok, MAIN_WIDTH), BF16),
                jax.ShapeDtypeStruct((n_grow, n_tok), F32),
            ],
            scratch_shapes=[pltpu.VMEM((tm_proj, d), BF16)],
            compiler_params=_params(("arbitrary", "arbitrary")),
            name="inproj",
        )(*x_parts, mod_l, vec(nw[0]), w_main, b_main, wg_rows, bga_rows, bgb_rows)

        scan_spec = pl.BlockSpec((n_grow, scan_lanes), lambda i: (0, i))
        scans = pl.pallas_call(
            _gates_kernel,
            grid=(n_tok // scan_lanes,),
            in_specs=[scan_spec],
            out_specs=[scan_spec] * 3,
            out_shape=[jax.ShapeDtypeStruct((n_grow, n_tok), F32)] * 3,
            compiler_params=_params(("arbitrary",)),
            name="gates",
        )(gt)

        ya_ctx, states = _mlstm(z, scans, 0, n_ctx, ctx_len, prev=states)
        ya_lat, _ = _mlstm(z, scans, n_ctx_tok, n_lat, lat_len,
                           init=(state_C, state_n, state_m), layer=l)

        tm = TOK_TILE
        halo_per_tile = tm // POOL_HALO
        n_halo = n_tok // POOL_HALO
        zcol = lambda col: pl.BlockSpec((tm, d), lambda i: (i, col))
        full = lambda shape: pl.BlockSpec(shape, lambda i: (0,) * len(shape))
        x = pl.pallas_call(
            functools.partial(_mix_merge_kernel, n_x=len(x_parts), n_first=n_ctx_tok // tm,
                              n_ctx_tok=n_ctx_tok, ctx_len=ctx_len, lat_len=lat_len),
            grid=(n_tok // tm,),
            in_specs=tok_specs(len(x_parts), tm) + tok_specs(2, tm) + [
                pl.BlockSpec((POOL_HALO, d),
                             lambda i: (jnp.maximum(i * halo_per_tile - 1, 0), COL_P)),
                zcol(COL_P),
                pl.BlockSpec((POOL_HALO, d),
                             lambda i: (jnp.minimum((i + 1) * halo_per_tile, n_halo - 1), COL_P)),
                zcol(COL_U), zcol(COL_VG), zcol(COL_M), zcol(COL_M + 1), zcol(COL_M + 2),
                pl.BlockSpec((1, N_MOD, d), lambda i: (row_tok(i), 0, 0)),
                full((1, d)),
                full(pool_bands.shape),
                full((N_GROUPS, GROUP_DIM, GROUP_DIM)), full((1, d)), full((1, d)),
                full((N_GROUPS, CHUNK, CHUNK)), full((CHUNK, N_GROUPS)),
                pl.BlockSpec((None, 3, d, d), lambda i, l=l: (l, 0, 0, 0)),
                pl.BlockSpec((None, d, d), lambda i, l=l: (l, 0, 0)),
            ],
            out_specs=pl.BlockSpec((tm, d), lambda i: (i, 0)),
            out_shape=jax.ShapeDtypeStruct((n_tok, d), F32),
            scratch_shapes=[pltpu.VMEM((3, d, d), BF16), pltpu.VMEM((d, d), BF16)],
            compiler_params=_params(("arbitrary",)),
            name="mix_merge",
        )(*x_parts, ya_ctx, ya_lat, z, z, z, z, z, z, z, z, mod_l, vec(nw[1]),
          pool_bands, pool_w[l].astype(BF16), vec(pool_scale[l]), vec(gmlp_norm_w[l]),
          gmlp_ws[l].astype(BF16), gmlp_b[l].T, w_br, w_out)
        x_parts = (x,)

        tf = FFN_TILE
        rows_per_tile = tf // GRID_W
        n_rows = n_tok // GRID_W
        n_ctx_tiles = n_ctx_tok // tf
        if last_layer:
            out_specs = [pl.BlockSpec((tf, d), lambda i: (jnp.minimum(i, n_ctx_tiles - 1), 0)),
                         pl.BlockSpec((tf, d), lambda i: (jnp.maximum(i - n_ctx_tiles, 0), 0))]
            out_shape = [jax.ShapeDtypeStruct((n_ctx_tok, d), F32),
                         jax.ShapeDtypeStruct((n_lat_tok, d), F32)]
        else:
            out_specs = [pl.BlockSpec((tf, d), lambda i: (i, 0))]
            out_shape = [jax.ShapeDtypeStruct((n_tok, d), F32)]
        x_parts = tuple(pl.pallas_call(
            functools.partial(_ffn_kernel, n_ctx_tok=n_ctx_tok, ctx_len=ctx_len, lat_len=lat_len),
            grid=(n_tok // tf,),
            in_specs=[
                pl.BlockSpec((GRID_W, d), lambda i: (jnp.maximum(i * rows_per_tile - 1, 0), 0)),
                pl.BlockSpec((tf, d), lambda i: (i, 0)),
                pl.BlockSpec((GRID_W, d),
                             lambda i: (jnp.minimum((i + 1) * rows_per_tile, n_rows - 1), 0)),
                pl.BlockSpec((1, N_MOD, d), lambda i: (row_ffn(i), 0, 0)),
                full((1, d)), full((1, d)), full((d, 2 * D_FF)),
                full((9, D_FF)), full((1, D_FF)), full((D_FF, d)),
            ],
            out_specs=out_specs,
            out_shape=out_shape,
            compiler_params=_params(("arbitrary",)),
            name="ffn",
        )(x, x, x, mod_l, vec(nw[2]), vec(nw[3]), w_up[l].astype(BF16),
          conv_w[l].reshape(9, D_FF), vec(conv_b[l]), w_down[l].astype(BF16)))

    new_c, new_n, new_m = states
    return (x_parts[0].reshape(n_ctx, ctx_len, d), x_parts[1].reshape(n_lat, lat_len, d), new_c,
            new_n.reshape(n_ctx, depth, 2, N_HEADS, HEAD_DIM), new_m.reshape(n_ctx, depth, 2, N_HEADS))
```

```python
import functools

import jax
import jax.numpy as jnp
import numpy as np
from jax import lax
from jax.experimental import pallas as pl
from jax.experimental.pallas import tpu as pltpu

F32 = jnp.float32
BF16 = jnp.bfloat16

D_MODEL = 1024
N_HEADS = 4
HEAD_DIM = D_MODEL // N_HEADS
CHUNK = 128
POOL_WINDOWS = (2, 4, 8, 16)
N_GROUPS = 4
GROUP_DIM = D_MODEL // N_GROUPS
D_FF = 2816
GRID_W = 64
N_MOD = 6
RMS_EPS = 1e-6
LANES = 128
SUBLANES = 8
VMEM_LIMIT = 56 * 1024 * 1024

COL_Q, COL_K, COL_V, COL_O, COL_P, COL_U, COL_VG, COL_M = 0, 1, 2, 3, 4, 5, 6, 7
MAIN_WIDTH = 10 * D_MODEL

NT_DIMS = (((1,), (1,)), ((), ()))


def _rms(x, w):
    return x * lax.rsqrt(jnp.mean(x * x, axis=-1, keepdims=True) + RMS_EPS) * w


def _log_sigmoid(x):
    return jnp.minimum(x, 0.0) - jnp.log1p(jnp.exp(-jnp.abs(x)))


def _params(sem):
    return pltpu.CompilerParams(dimension_semantics=sem, vmem_limit_bytes=VMEM_LIMIT)


def _mod_kernel(c_ref, w_ref, b_ref, o_ref):
    c = c_ref[...]
    s = (c * jax.nn.sigmoid(c)).astype(BF16)
    o_ref[...] = jnp.dot(s, w_ref[...].astype(BF16), preferred_element_type=F32) + b_ref[...]


def _modulation(cvec, w_mod, b_mod):
    depth = w_mod.shape[0]
    n = N_MOD * D_MODEL
    tn = 1536
    out = pl.pallas_call(
        _mod_kernel,
        grid=(depth, n // tn),
        in_specs=[
            pl.BlockSpec((SUBLANES, D_MODEL), lambda l, j: (0, 0)),
            pl.BlockSpec((None, D_MODEL, tn), lambda l, j: (l, 0, j)),
            pl.BlockSpec((None, 1, tn), lambda l, j: (l, 0, j)),
        ],
        out_specs=pl.BlockSpec((None, SUBLANES, tn), lambda l, j: (l, 0, j)),
        out_shape=jax.ShapeDtypeStruct((depth, SUBLANES, n), F32),
        compiler_params=_params(("arbitrary", "arbitrary")),
        name="modulation",
    )(cvec, w_mod, b_mod.reshape(depth, 1, n))
    return out.reshape(depth, SUBLANES, N_MOD, D_MODEL)


def _modnorm(x, mod_ref, nw_ref, k_shift, k_scale):
    h = _rms(x, nw_ref[...])
    return h * (1.0 + mod_ref[0, k_scale:k_scale + 1, :]) + mod_ref[0, k_shift:k_shift + 1, :]


def _pick(refs, first):
    if len(refs) == 1:
        return refs[0][...]
    return jnp.where(first, refs[0][...], refs[1][...])


def _inproj_kernel(*refs, n_x, n_first):
    x_refs = refs[:n_x]
    mod_ref, nw_ref, w_ref, b_ref, wgt_ref, bgta_ref, bgtb_ref, z_ref, gt_ref, h_scr = refs[n_x:]

    @pl.when(pl.program_id(1) == 0)
    def _():
        x = _pick(x_refs, pl.program_id(0) < n_first)
        hb = _modnorm(x, mod_ref, nw_ref, 0, 1).astype(BF16)
        h_scr[...] = hb
        gt_ref[...] = (lax.dot_general(wgt_ref[...], hb, NT_DIMS, preferred_element_type=F32)
                       + bgta_ref[...] + bgtb_ref[...])

    z = jnp.dot(h_scr[...], w_ref[...], preferred_element_type=F32) + b_ref[...]
    z_ref[...] = z.astype(z_ref.dtype)


def _gates_kernel(g_ref, a_ref, b_ref, cm_ref):
    g = g_ref[...]
    fwd = lax.broadcasted_iota(jnp.int32, g.shape, 0) % SUBLANES == 0
    lane = lax.broadcasted_iota(jnp.int32, g.shape, 1) % CHUNK
    lf = _log_sigmoid(pltpu.roll(g, g.shape[0] - 2, 0))
    pre, suf = _chunk_scans(lf, jnp.add, 0.0, lane)
    b = jnp.where(fwd, pre, suf)
    a = g - b
    pre, suf = _chunk_scans(a, jnp.maximum, -jnp.inf, lane)
    a_ref[...] = a
    b_ref[...] = b
    cm_ref[...] = jnp.where(fwd, pre, suf)


def _chunk_scans(x, op, fill, lane):
    width = x.shape[1]
    fwd = bwd = x
    k = 1
    while k < CHUNK:
        fwd = op(fwd, jnp.where(lane >= k, pltpu.roll(fwd, k, 1), fill))
        bwd = op(bwd, jnp.where(lane < CHUNK - k, pltpu.roll(bwd, width - k, 1), fill))
        k *= 2
    return fwd, bwd


def _mlstm_kernel(*refs, seq_len, has_init, emit_state, n_prev):
    q_ref, k_ref, v_ref, o_ref, a_ref, b_ref, cm_ref = refs[:7]
    pos = 7
    if has_init:
        c0_ref, n0_ref, m0_ref = refs[pos:pos + 3]
        pos += 3
    if n_prev:
        cprev_ref, nprev_ref, mprev_ref = refs[pos:pos + 3]
        pos += 3
    ya_ref = refs[pos]
    pos += 1
    if emit_state:
        cout_ref, nout_ref, mout_ref = refs[pos:pos + 3]
        pos += 3
    rows_scr, wk16_scr, dec_scr, vt_scr, vtw_scr, cb_scr, n1_scr, c_scr = refs[pos:pos + 8]

    n_chunks = seq_len // CHUNK
    unroll = min(4, n_chunks)
    row81 =lax.broadcasted_iota(jnp.int32, (SUBLANES, 1), 0)
    is_fwd1 = row81 == 0
    row_n = lax.broadcasted_iota(jnp.int32, (SUBLANES, HEAD_DIM), 0)

    chunk = lambda ref, c: ref[:, c * CHUNK:(c + 1) * CHUNK]
    a_c = [chunk(a_ref, c) for c in range(n_chunks)]
    b_c = [chunk(b_ref, c) for c in range(n_chunks)]
    cm_c = [chunk(cm_ref, c) for c in range(n_chunks)]
    row8 = lax.broadcasted_iota(jnp.int32, (SUBLANES, CHUNK), 0)
    lane8 = lax.broadcasted_iota(jnp.int32, (SUBLANES, CHUNK), 1)
    last = lane8 == jnp.where(row8 == 0, CHUNK - 1, 0)
    amax_c = [jnp.max(a, axis=1, keepdims=True) for a in a_c]
    bend_c = [jnp.sum(jnp.where(last, b, 0.0), axis=1, keepdims=True) for b in b_c]

    if has_init:
        m_state = jnp.where(is_fwd1, m0_ref[0], jnp.where(row81 == 1, m0_ref[1], 0.0))
        n_init = jnp.where(row_n == 0, n0_ref[0], n0_ref[1])
        c_scr[...] = c0_ref[...]
    else:
        m_state = jnp.zeros((SUBLANES, 1), F32)
        n_init = jnp.zeros((SUBLANES, HEAD_DIM), F32)
        c_scr[...] = jnp.zeros(c_scr.shape, F32)

    m_before, top = [], []
    for j in range(n_chunks):
        jr = n_chunks - 1 - j
        amax = jnp.where(is_fwd1, amax_c[j], amax_c[jr])
        bend = jnp.where(is_fwd1, bend_c[j], bend_c[jr])
        m_before.append(m_state)
        top.append(jnp.maximum(m_state, amax))
        m_state = bend + top[-1]

    for c in range(n_chunks):
        cr = n_chunks - 1 - c
        m_c = jnp.where(is_fwd1, m_before[c], m_before[cr])
        top_c = jnp.where(is_fwd1, top[c], top[cr])
        big_m = jnp.maximum(m_c, cm_c[c])
        w_k = jnp.exp(a_c[c] - top_c)
        rows_scr[c, 0] = a_c[c]
        rows_scr[c, 1] = big_m
        rows_scr[c, 2] = jnp.exp(m_c - big_m)
        rows_scr[c, 3] = jnp.exp(-b_c[c] - big_m)
        rows_scr[c, 4] = w_k
        wk16_scr[c] = jnp.concatenate([w_k, jnp.zeros_like(w_k)], axis=0).astype(BF16)
        dec_scr[c] = jnp.broadcast_to(jnp.exp(m_c - top_c), (SUBLANES, HEAD_DIM))

    def load_k(r0):
        return k_ref[pl.ds(r0, CHUNK), :]

    def pass_a(j, nst):
        c = n_chunks - 1 - j
        r0 = pl.multiple_of(c * CHUNK, CHUNK)
        ks = load_k(r0)
        vt = v_ref[pl.ds(r0, CHUNK), :].astype(F32).T
        w_k = rows_scr[c, 4]
        vt_scr[c] = vt.astype(BF16)
        vtw_scr[c] = (vt * w_k[0:1, :]).astype(BF16)
        c_in = c_scr[1]
        cb_scr[c] = c_in.astype(BF16)
        n1_scr[c] = nst
        dec = dec_scr[c]
        c_scr[1] = dec[1:2, :] * c_in + jnp.dot((vt * w_k[1:2, :]).astype(BF16), ks,
                                                preferred_element_type=F32)
        n_upd = jnp.dot(wk16_scr[c], ks, preferred_element_type=F32)[:SUBLANES]
        return dec * nst + n_upd

    n_rev = lax.fori_loop(0, n_chunks, pass_a, n_init, unroll=unroll)

    row_id = lax.broadcasted_iota(jnp.int32, (CHUNK, CHUNK), 0)
    col_id = lax.broadcasted_iota(jnp.int32, (CHUNK, CHUNK), 1)
    tri = (row_id <= col_id, row_id >= col_id)

    def pass_b(c, nst):
        r0 = pl.multiple_of(c * CHUNK, CHUNK)
        q = q_ref[pl.ds(r0, CHUNK), :]
        ks = load_k(r0)
        vt = vt_scr[c]
        a_r, bigm_r, winter_r, floor_r = (rows_scr[c, i] for i in range(4))
        st_all = lax.dot_general(ks, q, NT_DIMS, preferred_element_type=F32)
        n_rows = jnp.where(row_n == 0, nst, n1_scr[c])
        n_rows = jnp.concatenate([n_rows, jnp.zeros_like(n_rows)], axis=0).astype(BF16)
        qn = lax.dot_general(n_rows, q, NT_DIMS, preferred_element_type=F32)
        c0 = c_scr[0]
        sts, invs = [], []
        for d in range(2):
            row = slice(d, d + 1)
            a_st = jnp.broadcast_to(a_r[row, :], (CHUNK, CHUNK)).T
            st = st_all * jnp.exp(jnp.where(tri[d], a_st - bigm_r[row, :], -jnp.inf))
            den = jnp.sum(st, axis=0, keepdims=True) + winter_r[row, :] * qn[row, :]
            invs.append(1.0 / jnp.maximum(jnp.abs(den), floor_r[row, :]))
            sts.append(st.astype(BF16))
        intra = jnp.dot(vt, jnp.concatenate(sts, axis=1), preferred_element_type=F32)
        ht = None
        for d in range(2):
            row = slice(d, d + 1)
            cb = c0.astype(BF16) if d == 0 else cb_scr[c]
            num_t = (intra[:, d * CHUNK:(d + 1) * CHUNK]
                     + winter_r[row, :] * lax.dot_general(cb, q, NT_DIMS,
                                                          preferred_element_type=F32))
            ht = num_t * invs[d] if ht is None else ht + num_t * invs[d]
        gate = jax.nn.sigmoid(o_ref[pl.ds(r0, CHUNK), :].astype(F32))
        ya_ref[pl.ds(r0, CHUNK), :] = (gate * ht.T).astype(ya_ref.dtype)

        dec = dec_scr[c]
        c_scr[0] = dec[0:1, :] * c0 + jnp.dot(vtw_scr[c], ks, preferred_element_type=F32)
        n_upd = jnp.dot(wk16_scr[c], ks, preferred_element_type=F32)[:SUBLANES]
        return dec * nst + n_upd

    n_fwd = lax.fori_loop(0, n_chunks, pass_b, n_init, unroll=unroll)

    if emit_state:
        if n_prev:
            cout_ref[:n_prev] = cprev_ref[...]
            nout_ref[:n_prev] = nprev_ref[...]
            mout_ref[:n_prev] = mprev_ref[...]
        cout_ref[n_prev] = c_scr[...]
        nout_ref[n_prev, 0] = n_fwd[0:1, :]
        nout_ref[n_prev, 1] = n_rev[1:2, :]
        mout_ref[n_prev, 0] = m_state[0:1, :]
        mout_ref[n_prev, 1] = m_state[1:2, :]


def _mlstm(z, scans, tok0, n_batch, seq_len, init=None, layer=0, prev=None):
    blk0 = tok0 // seq_len
    n_chunks = seq_len // CHUNK
    has_init = init is not None
    emit_state = not has_init
    n_prev = 0 if prev is None else prev[0].shape[1]

    def zspec(col):
        return pl.BlockSpec((seq_len, HEAD_DIM),
                            lambda b, h: (blk0 + b, col * N_HEADS + h))

    scan_spec = pl.BlockSpec((SUBLANES, seq_len), lambda b, h: (h, blk0 + b))
    in_specs = [zspec(COL_Q), zspec(COL_K), zspec(COL_V), zspec(COL_O),
                scan_spec, scan_spec, scan_spec]
    args = [z, z, z, z, *scans]

    def state_specs(n_layers):
        return [
            pl.BlockSpec((None, n_layers, 2, None, HEAD_DIM, HEAD_DIM),
                         lambda b, h: (b, 0, 0, h, 0, 0)),
            pl.BlockSpec((None, n_layers, 2, None, 1, HEAD_DIM), lambda b, h: (b, 0, 0, h, 0, 0)),
            pl.BlockSpec((None, n_layers, 2, None, 1, 1), lambda b, h: (b, 0, 0, h, 0, 0)),
        ]

    if has_init:
        sc, sn, sm = init
        nb, depth = sc.shape[:2]
        in_specs += [
            pl.BlockSpec((None, None, 2, None, HEAD_DIM, HEAD_DIM),
                         lambda b, h: (b, layer, 0, h, 0, 0)),
            pl.BlockSpec((None, None, 2, None, 1, HEAD_DIM), lambda b, h: (b, layer, 0, h, 0, 0)),
            pl.BlockSpec((None, None, 2, None, 1, 1), lambda b, h: (b, layer, 0, h, 0, 0)),
        ]
        args += [sc, sn.reshape(nb, depth, 2, N_HEADS, 1, HEAD_DIM),
                 sm.reshape(nb, depth, 2, N_HEADS, 1, 1)]
    if n_prev:
        in_specs += state_specs(n_prev)
        args += list(prev)
    out_specs = [pl.BlockSpec((seq_len, HEAD_DIM), lambda b, h: (b, h))]
    out_shape = [jax.ShapeDtypeStruct((n_batch * seq_len, D_MODEL), BF16)]
    if emit_state:
        n_out = n_prev + 1
        out_specs += state_specs(n_out)
        out_shape += [
            jax.ShapeDtypeStruct((n_batch, n_out, 2, N_HEADS, HEAD_DIM, HEAD_DIM), F32),
            jax.ShapeDtypeStruct((n_batch, n_out, 2, N_HEADS, 1, HEAD_DIM), F32),
            jax.ShapeDtypeStruct((n_batch, n_out, 2, N_HEADS, 1, 1), F32),
        ]
    outs = pl.pallas_call(
        functools.partial(_mlstm_kernel, seq_len=seq_len, has_init=has_init, emit_state=emit_state,
                          n_prev=n_prev),
        grid=(n_batch, N_HEADS),
        in_specs=in_specs,
        out_specs=out_specs,
        out_shape=out_shape,
        scratch_shapes=[
            pltpu.VMEM((n_chunks, 5, SUBLANES, CHUNK), F32),
            pltpu.VMEM((n_chunks, 2 * SUBLANES, CHUNK), BF16),
            pltpu.VMEM((n_chunks, SUBLANES, HEAD_DIM), F32),
            pltpu.VMEM((n_chunks, HEAD_DIM, CHUNK), BF16),
            pltpu.VMEM((n_chunks, HEAD_DIM, CHUNK), BF16),
            pltpu.VMEM((n_chunks, HEAD_DIM, HEAD_DIM), BF16),
            pltpu.VMEM((n_chunks, SUBLANES, HEAD_DIM), F32),
            pltpu.VMEM((2, HEAD_DIM, HEAD_DIM), F32),
        ],
        compiler_params=_params(("arbitrary", "arbitrary")),
        name="mlstm_ctx" if emit_state else "mlstm_lat",
    )(*args)
    if emit_state:
        return outs[0], tuple(outs[1:])
    return outs[0], None


TOK_TILE = 256
POOL_HALO = 64


def _mix_merge_kernel(*refs, n_x, n_first, n_ctx_tok, ctx_len, lat_len):
    x_refs = refs[:n_x]
    (ya_ctx_ref, ya_lat_ref, pp_ref, pc_ref, pn_ref, u_ref, vg_ref, m0_ref, m1_ref, m2_ref,
     mod_ref, nw_ref, band_ref, pw_ref, ps_ref, gnw_ref, ws_ref, gb_ref, wbr32_ref, wout32_ref,
     o_ref, wbr_ref, wout_ref) = refs[n_x:]

    @pl.when(pl.program_id(0) == 0)
    def _():
        for n in range(wbr_ref.shape[0]):
            wbr_ref[n] = wbr32_ref[n].astype(BF16)
        wout_ref[...] = wout32_ref[...].astype(BF16)

    tok0 = pl.program_id(0) * TOK_TILE
    first = pl.program_id(0) < n_first
    is_ctx = tok0 < n_ctx_tok
    seq_len = jnp.where(is_ctx, ctx_len, lat_len)
    base = jnp.where(is_ctx, 0, n_ctx_tok)
    seq_start = base + ((tok0 - base) // seq_len) * seq_len
    seq_end = seq_start + seq_len

    t_col = tok0 + lax.broadcasted_iota(jnp.int32, (TOK_TILE, 1), 0)
    p_prev = jnp.where(tok0 > seq_start, pp_ref[...], jnp.zeros_like(pp_ref))
    p_next = jnp.where(tok0 + TOK_TILE < seq_end, pn_ref[...], jnp.zeros_like(pn_ref))
    p_ext = jnp.concatenate([p_prev, pc_ref[...], p_next], axis=0)
    yb = []
    for g, win in enumerate(POOL_WINDOWS):
        half = win // 2
        sl = slice(g * GROUP_DIM, (g + 1) * GROUP_DIM)
        acc = jnp.dot(band_ref[g], p_ext[:, sl], preferred_element_type=F32)
        count = (jnp.minimum(t_col + half, seq_end) - jnp.maximum(t_col - half, seq_start)).astype(F32)
        y = (acc / count - pc_ref[:, sl].astype(F32)).astype(BF16)
        yb.append(jnp.dot(y, pw_ref[g], preferred_element_type=F32) * ps_ref[:, sl])
    yb = jnp.concatenate(yb, axis=1).astype(BF16)

    u = jax.nn.gelu(u_ref[...].astype(F32))
    vg = _rms(jax.nn.gelu(vg_ref[...].astype(F32)), gnw_ref[...]).astype(BF16)
    yc = []
    for ch in range(TOK_TILE // CHUNK):
        rows = slice(ch * CHUNK, (ch + 1) * CHUNK)
        parts = []
        for g in range(N_GROUPS):
            sl = slice(g * GROUP_DIM, (g + 1) * GROUP_DIM)
            s = jnp.dot(ws_ref[g], vg[rows, sl], preferred_element_type=F32) + gb_ref[:, g:g + 1]
            parts.append(u[rows, sl] * s)
        yc.append(jnp.concatenate(parts, axis=1))
    yc = jnp.concatenate(yc, axis=0).astype(BF16)

    ys = (_pick((ya_ctx_ref, ya_lat_ref), first), yb, yc)
    mixed = None
    for n, (y, m_ref) in enumerate(zip(ys, (m0_ref, m1_ref, m2_ref))):
        br = jnp.dot(y, wbr_ref[n], preferred_element_type=F32)
        term = jax.nn.sigmoid(m_ref[...].astype(F32)) * br
        mixed = term if mixed is None else mixed + term
    o = jnp.dot(mixed.astype(BF16), wout_ref[...], preferred_element_type=F32)
    o_ref[...] = _pick(x_refs, first) + mod_ref[0, 2:3, :] * _rms(o, nw_ref[...])


FFN_TILE = 512
FFN_SLAB = 256


def _ffn_kernel(xp_ref, xc_ref, xn_ref, mod_ref, nw2_ref, nw3_ref, wup_ref, cw_ref, cb_ref, wd32_ref,
                *rest, n_ctx_tok, ctx_len, lat_len):
    *out_refs, wd_ref = rest
    i = pl.program_id(0)

    @pl.when(i == 0)
    def _():
        wd_ref[...] = wd32_ref[...].astype(BF16)

    tok0 = i * FFN_TILE
    is_ctx = tok0 < n_ctx_tok
    tiles_per_lat = lat_len // FFN_TILE
    lat_tile = jnp.maximum(tok0 - n_ctx_tok, 0) // FFN_TILE % tiles_per_lat
    has_up = jnp.logical_and(jnp.logical_not(is_ctx), lat_tile != 0)
    has_dn = jnp.logical_and(jnp.logical_not(is_ctx), lat_tile != tiles_per_lat - 1)
    ext_rows = FFN_TILE + 2 * GRID_W
    r = lax.broadcasted_iota(jnp.int32, (FFN_TILE, FFN_SLAB), 0)
    col_ctx = r % ctx_len
    col_lat = r % GRID_W
    keep_left = jnp.where(jnp.where(is_ctx, col_ctx, col_lat) != 0, 1.0, 0.0)
    keep_right = jnp.where(jnp.where(is_ctx, col_ctx - (ctx_len - 1), col_lat - (GRID_W - 1)) != 0,
                           1.0, 0.0)
    r1 = lax.broadcasted_iota(jnp.int32, (ext_rows, 1), 0)
    in_grid = jnp.logical_and(jnp.logical_or(r1 >= GRID_W, has_up),
                              jnp.logical_or(r1 < GRID_W + FFN_TILE, has_dn))
    tap_on = jnp.where(lax.broadcasted_iota(jnp.int32, (9, 1), 0) // 3 == 1, 1.0,
                       jnp.where(is_ctx, 0.0, 1.0))
    cw = cw_ref[...] * tap_on

    x_ext = jnp.concatenate([xp_ref[...], xc_ref[...], xn_ref[...]], axis=0)
    h_ext = jnp.where(in_grid, _modnorm(x_ext, mod_ref, nw2_ref, 3, 4), 0.0).astype(BF16)
    h_cur = h_ext[GRID_W:GRID_W + FFN_TILE]

    ext_all = jnp.dot(h_ext, wup_ref[:, :D_FF], preferred_element_type=F32)
    gate_all = jnp.dot(h_cur, wup_ref[:, D_FF:], preferred_element_type=F32)

    u = []
    for c in range(D_FF // FFN_SLAB):
        sl = slice(c * FFN_SLAB, (c + 1) * FFN_SLAB)
        ext = ext_all[:, sl]
        q = []
        for dx in range(3):
            qs = None
            for dy in range(3):
                term = cw[dy * 3 + dx:dy * 3 + dx + 1, sl] * ext[dy * GRID_W:dy * GRID_W + FFN_TILE, :]
                qs = term if qs is None else qs + term
            q.append(qs)
        acc = (q[1] + cb_ref[:, sl] + pltpu.roll(q[0], 1, 0) * keep_left
               + pltpu.roll(q[2], FFN_TILE - 1, 0) * keep_right)
        u.append((jax.nn.gelu(acc) * gate_all[:, sl]).astype(BF16))
    f = jnp.dot(jnp.concatenate(u, axis=1), wd_ref[...], preferred_element_type=F32)

    y = xc_ref[...] + mod_ref[0, 5:6, :] * _rms(f, nw3_ref[...])
    if len(out_refs) == 1:
        out_refs[0][...] = y
    else:
        @pl.when(is_ctx)
        def _():
            out_refs[0][...] = y

        @pl.when(jnp.logical_not(is_ctx))
        def _():
            out_refs[1][...] = y


def _gate_rows(w_gate):
    idx, keep = [], []
    for p in range(N_HEADS * SUBLANES):
        h, j = divmod(p, SUBLANES)
        d, kind = j % 2, j // 2
        idx.append(d * 2 * N_HEADS + kind * N_HEADS + h if j < 4 else 0)
        keep.append(1.0 if j < 4 else 0.0)
    return w_gate[..., jnp.array(idx)] * jnp.array(keep, w_gate.dtype)


def kernel(x_prompt, x_sample, state_C, state_n, state_m, c, c_ctx, w_mod, b_mod, norm_w, w_in, b_in,
           mlstm_gate_b, pool_w, pool_scale, gmlp_norm_w, gmlp_ws, gmlp_b, w_br, w_out, w_up, conv_w,
           conv_b, w_down):
    n_ctx, ctx_len, d = x_prompt.shape
    n_lat, lat_len, _ = x_sample.shape
    depth = w_in.shape[0]
    n_ctx_tok = n_ctx * ctx_len
    n_lat_tok = n_lat * lat_len
    n_tok = n_ctx_tok + n_lat_tok
    assert d == D_MODEL and FFN_TILE % ctx_len == 0 and lat_len % FFN_TILE == 0
    assert n_ctx_tok % FFN_TILE == 0 and ctx_len % TOK_TILE == 0 and lat_len % TOK_TILE == 0
    n_gate = 2 * 2 * N_HEADS
    gate0 = 4 * D_MODEL

    x_parts = (x_prompt.reshape(n_ctx_tok, d), x_sample.reshape(n_lat_tok, d))

    def tok_specs(n_parts, tm, tile=lambda i: i):
        if n_parts == 1:
            return [pl.BlockSpec((tm, d), lambda i, *_: (tile(i), 0))]
        n_a = n_ctx_tok // tm
        return [pl.BlockSpec((tm, d), lambda i, *_: (jnp.minimum(tile(i), n_a - 1), 0)),
                pl.BlockSpec((tm, d), lambda i, *_: (jnp.maximum(tile(i) - n_a, 0), 0))]

    cvec = jnp.zeros((SUBLANES, d), F32).at[0].set(c_ctx).at[1:1 + n_lat].set(c)
    mod = _modulation(cvec, w_mod, b_mod)

    def mod_spec(tm):
        def row(i):
            t0 = i * tm
            return jnp.where(t0 < n_ctx_tok, 0, 1 + jnp.maximum(t0 - n_ctx_tok, 0) // lat_len)
        return row

    tn_in = 2048
    n_col_tiles = MAIN_WIDTH // tn_in
    k_scale = jnp.ones((MAIN_WIDTH,), F32).at[COL_K * d:(COL_K + 1) * d].set(HEAD_DIM ** -0.5)
    w_in16 = w_in.astype(BF16)
    w_main = (jnp.concatenate([w_in16[:, :, :gate0], w_in16[:, :, gate0 + n_gate:]], axis=2)
              * k_scale.astype(BF16))
    b_main = (jnp.concatenate([b_in[:, :gate0], b_in[:, gate0 + n_gate:]], axis=1)
              * k_scale).reshape(depth, 1, MAIN_WIDTH)
    row_tok = mod_spec(TOK_TILE)
    row_ffn = mod_spec(FFN_TILE)
    vec = lambda a: a.reshape(1, -1)
    states = None
    t_idx = np.arange(TOK_TILE)[:, None]
    u_idx = np.arange(TOK_TILE + 2 * POOL_HALO)[None, :] - POOL_HALO
    pool_bands = jnp.asarray(np.stack([(u_idx >= t_idx - w // 2) & (u_idx < t_idx + w // 2)
                                       for w in POOL_WINDOWS]), BF16)
    n_grow = N_HEADS * SUBLANES
    scan_lanes = 2048

    for l in range(depth):
        last_layer = l == depth - 1
        wg_rows = _gate_rows(w_in[l, :, gate0:gate0 + n_gate]).T.astype(BF16)
        bga_rows = _gate_rows(b_in[l, gate0:gate0 + n_gate]).reshape(-1, 1)
        bgb_rows = _gate_rows(mlstm_gate_b[l].reshape(n_gate)).reshape(-1, 1)
        mod_l = mod[l]
        nw = norm_w[l]
        tm_proj = 1024 if len(x_parts) == 2 else 2048
        assert n_ctx_tok % tm_proj == 0 and lat_len % tm_proj == 0
        row_in = mod_spec(tm_proj)

        z, gt = pl.pallas_call(
            functools.partial(_inproj_kernel, n_x=len(x_parts), n_first=n_ctx_tok // tm_proj),
            grid=(n_tok // tm_proj, n_col_tiles),
            in_specs=tok_specs(len(x_parts), tm_proj) + [
                pl.BlockSpec((1, N_MOD, d), lambda i, j: (row_in(i), 0, 0)),
                pl.BlockSpec((1, d), lambda i, j: (0, 0)),
                pl.BlockSpec((None, d, tn_in), lambda i, j, l=l: (l, 0, j)),
                pl.BlockSpec((None, 1, tn_in), lambda i, j, l=l: (l, 0, j)),
                pl.BlockSpec((n_grow, d), lambda i, j: (0, 0)),
                pl.BlockSpec((n_grow, 1), lambda i, j: (0, 0)),
                pl.BlockSpec((n_grow, 1), lambda i, j: (0, 0)),
            ],
            out_specs=[
                pl.BlockSpec((tm_proj, tn_in), lambda i, j: (i, j)),
                pl.BlockSpec((n_grow, tm_proj), lambda i, j: (0, i)),
            ],
            out_shape=[
                jax.ShapeDtypeStruct((n_tok, MAIN_WIDTH), BF16),
                jax.ShapeDtypeStruct((n_grow, n_tok), F32),
            ],
            scratch_shapes=[pltpu.VMEM((tm_proj, d), BF16)],
            compiler_params=_params(("arbitrary", "arbitrary")),
            name="inproj",
        )(*x_parts, mod_l, vec(nw[0]), w_main, b_main, wg_rows, bga_rows, bgb_rows)

        scan_spec = pl.BlockSpec((n_grow, scan_lanes), lambda i: (0, i))
        scans = pl.pallas_call(
            _gates_kernel,
            grid=(n_tok // scan_lanes,),
            in_specs=[scan_spec],
            out_specs=[scan_spec] * 3,
            out_shape=[jax.ShapeDtypeStruct((n_grow, n_tok), F32)] * 3,
            compiler_params=_params(("arbitrary",)),
            name="gates",
        )(gt)

        ya_ctx, states = _mlstm(z, scans, 0, n_ctx, ctx_len, prev=states)
        ya_lat, _ = _mlstm(z, scans, n_ctx_tok, n_lat, lat_len,
                           init=(state_C, state_n, state_m), layer=l)

        tm = TOK_TILE
        halo_per_tile = tm // POOL_HALO
        n_halo = n_tok // POOL_HALO
        zcol = lambda col: pl.BlockSpec((tm, d), lambda i: (i, col))
        full = lambda shape: pl.BlockSpec(shape, lambda i: (0,) * len(shape))
        x = pl.pallas_call(
            functools.partial(_mix_merge_kernel, n_x=len(x_parts), n_first=n_ctx_tok // tm,
                              n_ctx_tok=n_ctx_tok, ctx_len=ctx_len, lat_len=lat_len),
            grid=(n_tok // tm,),
            in_specs=tok_specs(len(x_parts), tm) + tok_specs(2, tm) + [
                pl.BlockSpec((POOL_HALO, d),
                             lambda i: (jnp.maximum(i * halo_per_tile - 1, 0), COL_P)),
                zcol(COL_P),
                pl.BlockSpec((POOL_HALO, d),
                             lambda i: (jnp.minimum((i + 1) * halo_per_tile, n_halo - 1), COL_P)),
                zcol(COL_U), zcol(COL_VG), zcol(COL_M), zcol(COL_M + 1), zcol(COL_M + 2),
                pl.BlockSpec((1, N_MOD, d), lambda i: (row_tok(i), 0, 0)),
                full((1, d)),
                full(pool_bands.shape),
                full((N_GROUPS, GROUP_DIM, GROUP_DIM)), full((1, d)), full((1, d)),
                full((N_GROUPS, CHUNK, CHUNK)), full((CHUNK, N_GROUPS)),
                pl.BlockSpec((None, 3, d, d), lambda i, l=l: (l, 0, 0, 0)),
                pl.BlockSpec((None, d, d), lambda i, l=l: (l, 0, 0)),
            ],
            out_specs=pl.BlockSpec((tm, d), lambda i: (i, 0)),
            out_shape=jax.ShapeDtypeStruct((n_tok, d), F32),
            scratch_shapes=[pltpu.VMEM((3, d, d), BF16), pltpu.VMEM((d, d), BF16)],
            compiler_params=_params(("arbitrary",)),
            name="mix_merge",
        )(*x_parts, ya_ctx, ya_lat, z, z, z, z, z, z, z, z, mod_l, vec(nw[1]),
          pool_bands, pool_w[l].astype(BF16), vec(pool_scale[l]), vec(gmlp_norm_w[l]),
          gmlp_ws[l].astype(BF16), gmlp_b[l].T, w_br, w_out)
        x_parts = (x,)

        tf = FFN_TILE
        rows_per_tile = tf // GRID_W
        n_rows = n_tok // GRID_W
        n_ctx_tiles = n_ctx_tok // tf
        if last_layer:
            out_specs = [pl.BlockSpec((tf, d), lambda i: (jnp.minimum(i, n_ctx_tiles - 1), 0)),
                         pl.BlockSpec((tf, d), lambda i: (jnp.maximum(i - n_ctx_tiles, 0), 0))]
            out_shape = [jax.ShapeDtypeStruct((n_ctx_tok, d), F32),
                         jax.ShapeDtypeStruct((n_lat_tok, d), F32)]
        else:
            out_specs = [pl.BlockSpec((tf, d), lambda i: (i, 0))]
            out_shape = [jax.ShapeDtypeStruct((n_tok, d), F32)]
        x_parts = tuple(pl.pallas_call(
            functools.partial(_ffn_kernel, n_ctx_tok=n_ctx_tok, ctx_len=ctx_len, lat_len=lat_len),
            grid=(n_tok // tf,),
            in_specs=[
                pl.BlockSpec((GRID_W, d), lambda i: (jnp.maximum(i * rows_per_tile - 1, 0), 0)),
                pl.BlockSpec((tf, d), lambda i: (i, 0)),
                pl.BlockSpec((GRID_W, d),
                             lambda i: (jnp.minimum((i + 1) * rows_per_tile, n_rows - 1), 0)),
                pl.BlockSpec((1, N_MOD, d), lambda i: (row_ffn(i), 0, 0)),
                full((1, d)), full((1, d)), full((d, 2 * D_FF)),
                full((9, D_FF)), full((1, D_FF)),
                pl.BlockSpec((None, D_FF, d), lambda i, l=l: (l, 0, 0)),
            ],
            out_specs=out_specs,
            out_shape=out_shape,
            scratch_shapes=[pltpu.VMEM((D_FF, d), BF16)],
            compiler_params=_params(("arbitrary",)),
            name="ffn",
        )(x, x, x, mod_l, vec(nw[2]), vec(nw[3]), w_up[l].astype(BF16),
          conv_w[l].reshape(9, D_FF), vec(conv_b[l]), w_down))

    new_c, new_n, new_m = states
    return (x_parts[0].reshape(n_ctx, ctx_len, d), x_parts[1].reshape(n_lat, lat_len, d), new_c,
            new_n.reshape(n_ctx, depth, 2, N_HEADS, HEAD_DIM), new_m.reshape(n_ctx, depth, 2, N_HEADS))
```

```python
import functools

import jax
import jax.numpy as jnp
import numpy as np
from jax import lax
from jax.experimental import pallas as pl
from jax.experimental.pallas import tpu as pltpu

F32 = jnp.float32
BF16 = jnp.bfloat16

D_MODEL = 1024
N_HEADS = 4
HEAD_DIM = D_MODEL // N_HEADS
CHUNK = 128
POOL_WINDOWS = (2, 4, 8, 16)
N_GROUPS = 4
GROUP_DIM = D_MODEL // N_GROUPS
D_FF = 2816
GRID_W = 64
N_MOD = 6
RMS_EPS = 1e-6
LANES = 128
SUBLANES = 8
VMEM_LIMIT = 56 * 1024 * 1024

COL_Q, COL_K, COL_V, COL_O, COL_P, COL_U, COL_VG, COL_M = 0, 1, 2, 3, 4, 5, 6, 7
MAIN_WIDTH = 10 * D_MODEL

NT_DIMS = (((1,), (1,)), ((), ()))


def _rms(x, w):
    return x * lax.rsqrt(jnp.mean(x * x, axis=-1, keepdims=True) + RMS_EPS) * w


def _log_sigmoid(x):
    return jnp.minimum(x, 0.0) - jnp.log1p(jnp.exp(-jnp.abs(x)))


def _params(sem):
    return pltpu.CompilerParams(dimension_semantics=sem, vmem_limit_bytes=VMEM_LIMIT)


def _mod_kernel(c_ref, w_ref, b_ref, o_ref):
    c = c_ref[...]
    s = (c * jax.nn.sigmoid(c)).astype(BF16)
    o_ref[...] = jnp.dot(s, w_ref[...].astype(BF16), preferred_element_type=F32) + b_ref[...]


def _modulation(cvec, w_mod, b_mod):
    depth = w_mod.shape[0]
    n = N_MOD * D_MODEL
    tn = 1536
    out = pl.pallas_call(
        _mod_kernel,
        grid=(depth, n // tn),
        in_specs=[
            pl.BlockSpec((SUBLANES, D_MODEL), lambda l, j: (0, 0)),
            pl.BlockSpec((None, D_MODEL, tn), lambda l, j: (l, 0, j)),
            pl.BlockSpec((None, 1, tn), lambda l, j: (l, 0, j)),
        ],
        out_specs=pl.BlockSpec((None, SUBLANES, tn), lambda l, j: (l, 0, j)),
        out_shape=jax.ShapeDtypeStruct((depth, SUBLANES, n), F32),
        compiler_params=_params(("arbitrary", "arbitrary")),
        name="modulation",
    )(cvec, w_mod, b_mod.reshape(depth, 1, n))
    return out.reshape(depth, SUBLANES, N_MOD, D_MODEL)


def _modnorm(x, mod_ref, nw_ref, k_shift, k_scale):
    h = _rms(x, nw_ref[...])
    return h * (1.0 + mod_ref[0, k_scale:k_scale + 1, :]) + mod_ref[0, k_shift:k_shift + 1, :]


def _pick(refs, first):
    if len(refs) == 1:
        return refs[0][...]
    return jnp.where(first, refs[0][...], refs[1][...])


def _inproj_kernel(*refs, n_x, n_first):
    x_refs = refs[:n_x]
    mod_ref, nw_ref, w_ref, b_ref, wgt_ref, bgta_ref, bgtb_ref, z_ref, gt_ref, h_scr = refs[n_x:]

    @pl.when(pl.program_id(1) == 0)
    def _():
        x = _pick(x_refs, pl.program_id(0) < n_first)
        hb = _modnorm(x, mod_ref, nw_ref, 0, 1).astype(BF16)
        h_scr[...] = hb
        gt_ref[...] = (lax.dot_general(wgt_ref[...], hb, NT_DIMS, preferred_element_type=F32)
                       + bgta_ref[...] + bgtb_ref[...])

    z = jnp.dot(h_scr[...], w_ref[...], preferred_element_type=F32) + b_ref[...]
    z_ref[...] = z.astype(z_ref.dtype)


def _gates_kernel(g_ref, a_ref, b_ref, cm_ref):
    g = g_ref[...]
    fwd = lax.broadcasted_iota(jnp.int32, g.shape, 0) % SUBLANES == 0
    lane = lax.broadcasted_iota(jnp.int32, g.shape, 1) % CHUNK
    lf = _log_sigmoid(pltpu.roll(g, g.shape[0] - 2, 0))
    pre, suf = _chunk_scans(lf, jnp.add, 0.0, lane)
    b = jnp.where(fwd, pre, suf)
    a = g - b
    pre, suf = _chunk_scans(a, jnp.maximum, -jnp.inf, lane)
    a_ref[...] = a
    b_ref[...] = b
    cm_ref[...] = jnp.where(fwd, pre, suf)


def _chunk_scans(x, op, fill, lane):
    width = x.shape[1]
    fwd = bwd = x
    k = 1
    while k < CHUNK:
        fwd = op(fwd, jnp.where(lane >= k, pltpu.roll(fwd, k, 1), fill))
        bwd = op(bwd, jnp.where(lane < CHUNK - k, pltpu.roll(bwd, width - k, 1), fill))
        k *= 2
    return fwd, bwd


def _mlstm_head(*refs, seq_len, has_init, emit_state, n_prev):
    q_ref, k_ref, v_ref, o_ref, a_ref, b_ref, cm_ref = refs[:7]
    pos = 7
    if has_init:
        c0_ref, n0_ref, m0_ref = refs[pos:pos + 3]
        pos += 3
    if n_prev:
        cprev_ref, nprev_ref, mprev_ref = refs[pos:pos + 3]
        pos += 3
    ya_ref = refs[pos]
    pos += 1
    if emit_state:
        cout_ref, nout_ref, mout_ref = refs[pos:pos + 3]
        pos += 3
    rows_scr, wk16_scr, dec_scr, vt_scr, vtw_scr, cb_scr, n1_scr, c_scr = refs[pos:pos + 8]

    n_chunks = seq_len // CHUNK
    unroll = min(4, n_chunks)
    row81 =lax.broadcasted_iota(jnp.int32, (SUBLANES, 1), 0)
    is_fwd1 = row81 == 0
    row_n = lax.broadcasted_iota(jnp.int32, (SUBLANES, HEAD_DIM), 0)

    chunk = lambda ref, c: ref[:, c * CHUNK:(c + 1) * CHUNK]
    a_c = [chunk(a_ref, c) for c in range(n_chunks)]
    b_c = [chunk(b_ref, c) for c in range(n_chunks)]
    cm_c = [chunk(cm_ref, c) for c in range(n_chunks)]
    row8 = lax.broadcasted_iota(jnp.int32, (SUBLANES, CHUNK), 0)
    lane8 = lax.broadcasted_iota(jnp.int32, (SUBLANES, CHUNK), 1)
    last = lane8 == jnp.where(row8 == 0, CHUNK - 1, 0)
    amax_c = [jnp.max(a, axis=1, keepdims=True) for a in a_c]
    bend_c = [jnp.sum(jnp.where(last, b, 0.0), axis=1, keepdims=True) for b in b_c]

    if has_init:
        m_state = jnp.where(is_fwd1, m0_ref[0], jnp.where(row81 == 1, m0_ref[1], 0.0))
        n_init = jnp.where(row_n == 0, n0_ref[0], n0_ref[1])
        c_scr[...] = c0_ref[...]
    else:
        m_state = jnp.zeros((SUBLANES, 1), F32)
        n_init = jnp.zeros((SUBLANES, HEAD_DIM), F32)
        c_scr[...] = jnp.zeros(c_scr.shape, F32)

    m_before, top = [], []
    for j in range(n_chunks):
        jr = n_chunks - 1 - j
        amax = jnp.where(is_fwd1, amax_c[j], amax_c[jr])
        bend = jnp.where(is_fwd1, bend_c[j], bend_c[jr])
        m_before.append(m_state)
        top.append(jnp.maximum(m_state, amax))
        m_state = bend + top[-1]

    for c in range(n_chunks):
        cr = n_chunks - 1 - c
        m_c = jnp.where(is_fwd1, m_before[c], m_before[cr])
        top_c = jnp.where(is_fwd1, top[c], top[cr])
        big_m = jnp.maximum(m_c, cm_c[c])
        w_k = jnp.exp(a_c[c] - top_c)
        rows_scr[c, 0] = a_c[c]
        rows_scr[c, 1] = big_m
        rows_scr[c, 2] = jnp.exp(m_c - big_m)
        rows_scr[c, 3] = jnp.exp(-b_c[c] - big_m)
        rows_scr[c, 4] = w_k
        wk16_scr[c] = jnp.concatenate([w_k, jnp.zeros_like(w_k)], axis=0).astype(BF16)
        dec_scr[c] = jnp.broadcast_to(jnp.exp(m_c - top_c), (SUBLANES, HEAD_DIM))

    def load_k(r0):
        return k_ref[pl.ds(r0, CHUNK), :]

    def pass_a(j, nst):
        c = n_chunks - 1 - j
        r0 = pl.multiple_of(c * CHUNK, CHUNK)
        ks = load_k(r0)
        vt = v_ref[pl.ds(r0, CHUNK), :].astype(F32).T
        w_k = rows_scr[c, 4]
        vt_scr[c] = vt.astype(BF16)
        vtw_scr[c] = (vt * w_k[0:1, :]).astype(BF16)
        c_in = c_scr[1]
        cb_scr[c] = c_in.astype(BF16)
        n1_scr[c] = nst
        dec = dec_scr[c]
        c_scr[1] = dec[1:2, :] * c_in + jnp.dot((vt * w_k[1:2, :]).astype(BF16), ks,
                                                preferred_element_type=F32)
        n_upd = jnp.dot(wk16_scr[c], ks, preferred_element_type=F32)[:SUBLANES]
        return dec * nst + n_upd

    n_rev = lax.fori_loop(0, n_chunks, pass_a, n_init, unroll=unroll)

    row_id = lax.broadcasted_iota(jnp.int32, (CHUNK, CHUNK), 0)
    col_id = lax.broadcasted_iota(jnp.int32, (CHUNK, CHUNK), 1)
    tri = (row_id <= col_id, row_id >= col_id)

    def pass_b(c, nst):
        r0 = pl.multiple_of(c * CHUNK, CHUNK)
        q = q_ref[pl.ds(r0, CHUNK), :]
        ks = load_k(r0)
        vt = vt_scr[c]
        a_r, bigm_r, winter_r, floor_r = (rows_scr[c, i] for i in range(4))
        st_all = lax.dot_general(ks, q, NT_DIMS, preferred_element_type=F32)
        n_rows = jnp.where(row_n == 0, nst, n1_scr[c])
        n_rows = jnp.concatenate([n_rows, jnp.zeros_like(n_rows)], axis=0).astype(BF16)
        qn = lax.dot_general(n_rows, q, NT_DIMS, preferred_element_type=F32)
        c0 = c_scr[0]
        sts, invs = [], []
        for d in range(2):
            row = slice(d, d + 1)
            a_st = jnp.broadcast_to(a_r[row, :], (CHUNK, CHUNK)).T
            st = st_all * jnp.exp(jnp.where(tri[d], a_st - bigm_r[row, :], -jnp.inf))
            den = jnp.sum(st, axis=0, keepdims=True) + winter_r[row, :] * qn[row, :]
            invs.append(1.0 / jnp.maximum(jnp.abs(den), floor_r[row, :]))
            sts.append(st.astype(BF16))
        intra = jnp.dot(vt, jnp.concatenate(sts, axis=1), preferred_element_type=F32)
        ht = None
        for d in range(2):
            row = slice(d, d + 1)
            cb = c0.astype(BF16) if d == 0 else cb_scr[c]
            num_t = (intra[:, d * CHUNK:(d + 1) * CHUNK]
                     + winter_r[row, :] * lax.dot_general(cb, q, NT_DIMS,
                                                          preferred_element_type=F32))
            ht = num_t * invs[d] if ht is None else ht + num_t * invs[d]
        gate = jax.nn.sigmoid(o_ref[pl.ds(r0, CHUNK), :].astype(F32))
        ya_ref[pl.ds(r0, CHUNK), :] = (gate * ht.T).astype(ya_ref.dtype)

        dec = dec_scr[c]
        c_scr[0] = dec[0:1, :] * c0 + jnp.dot(vtw_scr[c], ks, preferred_element_type=F32)
        n_upd = jnp.dot(wk16_scr[c], ks, preferred_element_type=F32)[:SUBLANES]
        return dec * nst + n_upd

    n_fwd = lax.fori_loop(0, n_chunks, pass_b, n_init, unroll=unroll)

    if emit_state:
        if n_prev:
            cout_ref[:n_prev] = cprev_ref[...]
            nout_ref[:n_prev] = nprev_ref[...]
            mout_ref[:n_prev] = mprev_ref[...]
        cout_ref[n_prev] = c_scr[...]
        nout_ref[n_prev, 0] = n_fwd[0:1, :]
        nout_ref[n_prev, 1] = n_rev[1:2, :]
        mout_ref[n_prev, 0] = m_state[0:1, :]
        mout_ref[n_prev, 1] = m_state[1:2, :]


def _mlstm_kernel(*refs, heads, seq_len, has_init, emit_state, n_prev):
    if heads == 1:
        return _mlstm_head(*refs, seq_len=seq_len, has_init=has_init, emit_state=emit_state,
                           n_prev=n_prev)
    assert not has_init
    n_state = 3 * (bool(n_prev) + emit_state)
    n_scratch = len(refs) - 8 - n_state
    for hh in range(heads):
        cols = slice(hh * HEAD_DIM, (hh + 1) * HEAD_DIM)
        views = [r.at[:, cols] for r in refs[:4]]
        views += [r.at[hh * SUBLANES:(hh + 1) * SUBLANES, :] for r in refs[4:7]]
        pos = 7
        if n_prev:
            views += [r.at[:, :, hh] for r in refs[pos:pos + 3]]
            pos += 3
        views.append(refs[pos].at[:, cols])
        pos += 1
        if emit_state:
            views += [r.at[:, :, hh] for r in refs[pos:pos + 3]]
            pos += 3
        assert len(refs) - pos == n_scratch
        _mlstm_head(*views, *refs[pos:], seq_len=seq_len, has_init=False, emit_state=emit_state,
                    n_prev=n_prev)


def _mlstm(z, scans, tok0, n_batch, seq_len, init=None, layer=0, prev=None, heads=1):
    blk0 = tok0 // seq_len
    n_chunks = seq_len // CHUNK
    has_init = init is not None
    emit_state = not has_init
    n_prev = 0 if prev is None else prev[0].shape[1]

    head_blocks = N_HEADS // heads
    hdim = None if heads == 1 else heads

    def zspec(col):
        return pl.BlockSpec((seq_len, heads * HEAD_DIM),
                            lambda b, h: (blk0 + b, col * head_blocks + h))

    scan_spec = pl.BlockSpec((heads * SUBLANES, seq_len), lambda b, h: (h, blk0 + b))
    in_specs = [zspec(COL_Q), zspec(COL_K), zspec(COL_V), zspec(COL_O),
                scan_spec, scan_spec, scan_spec]
    args = [z, z, z, z, *scans]

    def state_specs(n_layers):
        return [
            pl.BlockSpec((None, n_layers, 2, hdim, HEAD_DIM, HEAD_DIM),
                         lambda b, h: (b, 0, 0, h, 0, 0)),
            pl.BlockSpec((None, n_layers, 2, hdim, 1, HEAD_DIM), lambda b, h: (b, 0, 0, h, 0, 0)),
            pl.BlockSpec((None, n_layers, 2, hdim, 1, 1), lambda b, h: (b, 0, 0, h, 0, 0)),
        ]

    if has_init:
        sc, sn, sm = init
        nb, depth = sc.shape[:2]
        in_specs += [
            pl.BlockSpec((None, None, 2, None, HEAD_DIM, HEAD_DIM),
                         lambda b, h: (b, layer, 0, h, 0, 0)),
            pl.BlockSpec((None, None, 2, None, 1, HEAD_DIM), lambda b, h: (b, layer, 0, h, 0, 0)),
            pl.BlockSpec((None, None, 2, None, 1, 1), lambda b, h: (b, layer, 0, h, 0, 0)),
        ]
        args += [sc, sn.reshape(nb, depth, 2, N_HEADS, 1, HEAD_DIM),
                 sm.reshape(nb, depth, 2, N_HEADS, 1, 1)]
    if n_prev:
        in_specs += state_specs(n_prev)
        args += list(prev)
    out_specs = [pl.BlockSpec((seq_len, heads * HEAD_DIM), lambda b, h: (b, h))]
    out_shape = [jax.ShapeDtypeStruct((n_batch * seq_len, D_MODEL), BF16)]
    if emit_state:
        n_out = n_prev + 1
        out_specs += state_specs(n_out)
        out_shape += [
            jax.ShapeDtypeStruct((n_batch, n_out, 2, N_HEADS, HEAD_DIM, HEAD_DIM), F32),
            jax.ShapeDtypeStruct((n_batch, n_out, 2, N_HEADS, 1, HEAD_DIM), F32),
            jax.ShapeDtypeStruct((n_batch, n_out, 2, N_HEADS, 1, 1), F32),
        ]
    outs = pl.pallas_call(
        functools.partial(_mlstm_kernel, heads=heads, seq_len=seq_len, has_init=has_init,
                          emit_state=emit_state, n_prev=n_prev),
        grid=(n_batch, head_blocks),
        in_specs=in_specs,
        out_specs=out_specs,
        out_shape=out_shape,
        scratch_shapes=[
            pltpu.VMEM((n_chunks, 5, SUBLANES, CHUNK), F32),
            pltpu.VMEM((n_chunks, 2 * SUBLANES, CHUNK), BF16),
            pltpu.VMEM((n_chunks, SUBLANES, HEAD_DIM), F32),
            pltpu.VMEM((n_chunks, HEAD_DIM, CHUNK), BF16),
            pltpu.VMEM((n_chunks, HEAD_DIM, CHUNK), BF16),
            pltpu.VMEM((n_chunks, HEAD_DIM, HEAD_DIM), BF16),
            pltpu.VMEM((n_chunks, SUBLANES, HEAD_DIM), F32),
            pltpu.VMEM((2, HEAD_DIM, HEAD_DIM), F32),
        ],
        compiler_params=_params(("arbitrary", "arbitrary")),
        name="mlstm_ctx" if emit_state else "mlstm_lat",
    )(*args)
    if emit_state:
        return outs[0], tuple(outs[1:])
    return outs[0], None


TOK_TILE = 256
POOL_HALO = 64


def _mix_merge_kernel(*refs, n_x, n_first, n_ctx_tok, ctx_len, lat_len):
    x_refs = refs[:n_x]
    (ya_ctx_ref, ya_lat_ref, pp_ref, pc_ref, pn_ref, u_ref, vg_ref, m0_ref, m1_ref, m2_ref,
     mod_ref, nw_ref, band_ref, pw_ref, ps_ref, gnw_ref, ws_ref, gb_ref, wbr32_ref, wout32_ref,
     o_ref, wbr_ref, wout_ref) = refs[n_x:]

    @pl.when(pl.program_id(0) == 0)
    def _():
        for n in range(wbr_ref.shape[0]):
            wbr_ref[n] = wbr32_ref[n].astype(BF16)
        wout_ref[...] = wout32_ref[...].astype(BF16)

    tok0 = pl.program_id(0) * TOK_TILE
    first = pl.program_id(0) < n_first
    is_ctx = tok0 < n_ctx_tok
    seq_len = jnp.where(is_ctx, ctx_len, lat_len)
    base = jnp.where(is_ctx, 0, n_ctx_tok)
    seq_start = base + ((tok0 - base) // seq_len) * seq_len
    seq_end = seq_start + seq_len

    t_col = tok0 + lax.broadcasted_iota(jnp.int32, (TOK_TILE, 1), 0)
    p_prev = jnp.where(tok0 > seq_start, pp_ref[...], jnp.zeros_like(pp_ref))
    p_next = jnp.where(tok0 + TOK_TILE < seq_end, pn_ref[...], jnp.zeros_like(pn_ref))
    p_ext = jnp.concatenate([p_prev, pc_ref[...], p_next], axis=0)
    yb = []
    for g, win in enumerate(POOL_WINDOWS):
        half = win // 2
        sl = slice(g * GROUP_DIM, (g + 1) * GROUP_DIM)
        acc = jnp.dot(band_ref[g], p_ext[:, sl], preferred_element_type=F32)
        count = (jnp.minimum(t_col + half, seq_end) - jnp.maximum(t_col - half, seq_start)).astype(F32)
        y = (acc / count - pc_ref[:, sl].astype(F32)).astype(BF16)
        yb.append(jnp.dot(y, pw_ref[g], preferred_element_type=F32) * ps_ref[:, sl])
    yb = jnp.concatenate(yb, axis=1).astype(BF16)

    u = jax.nn.gelu(u_ref[...].astype(F32))
    vg = _rms(jax.nn.gelu(vg_ref[...].astype(F32)), gnw_ref[...]).astype(BF16)
    yc = []
    for ch in range(TOK_TILE // CHUNK):
        rows = slice(ch * CHUNK, (ch + 1) * CHUNK)
        parts = []
        for g in range(N_GROUPS):
            sl = slice(g * GROUP_DIM, (g + 1) * GROUP_DIM)
            s = jnp.dot(ws_ref[g], vg[rows, sl], preferred_element_type=F32) + gb_ref[:, g:g + 1]
            parts.append(u[rows, sl] * s)
        yc.append(jnp.concatenate(parts, axis=1))
    yc = jnp.concatenate(yc, axis=0).astype(BF16)

    ys = (_pick((ya_ctx_ref, ya_lat_ref), first), yb, yc)
    mixed = None
    for n, (y, m_ref) in enumerate(zip(ys, (m0_ref, m1_ref, m2_ref))):
        br = jnp.dot(y, wbr_ref[n], preferred_element_type=F32)
        term = jax.nn.sigmoid(m_ref[...].astype(F32)) * br
        mixed = term if mixed is None else mixed + term
    o = jnp.dot(mixed.astype(BF16), wout_ref[...], preferred_element_type=F32)
    o_ref[...] = _pick(x_refs, first) + mod_ref[0, 2:3, :] * _rms(o, nw_ref[...])


FFN_TILE = 512
FFN_SLAB = 256


def _ffn_kernel(xp_ref, xc_ref, xn_ref, mod_ref, nw2_ref, nw3_ref, wup_ref, cw_ref, cb_ref, wd32_ref,
                *rest, n_ctx_tok, ctx_len, lat_len):
    *out_refs, wd_ref = rest
    i = pl.program_id(0)

    @pl.when(i == 0)
    def _():
        wd_ref[...] = wd32_ref[...].astype(BF16)

    tok0 = i * FFN_TILE
    is_ctx = tok0 < n_ctx_tok
    tiles_per_lat = lat_len // FFN_TILE
    lat_tile = jnp.maximum(tok0 - n_ctx_tok, 0) // FFN_TILE % tiles_per_lat
    has_up = jnp.logical_and(jnp.logical_not(is_ctx), lat_tile != 0)
    has_dn = jnp.logical_and(jnp.logical_not(is_ctx), lat_tile != tiles_per_lat - 1)
    ext_rows = FFN_TILE + 2 * GRID_W
    r = lax.broadcasted_iota(jnp.int32, (FFN_TILE, FFN_SLAB), 0)
    col_ctx = r % ctx_len
    col_lat = r % GRID_W
    keep_left = jnp.where(jnp.where(is_ctx, col_ctx, col_lat) != 0, 1.0, 0.0)
    keep_right = jnp.where(jnp.where(is_ctx, col_ctx - (ctx_len - 1), col_lat - (GRID_W - 1)) != 0,
                           1.0, 0.0)
    r1 = lax.broadcasted_iota(jnp.int32, (ext_rows, 1), 0)
    in_grid = jnp.logical_and(jnp.logical_or(r1 >= GRID_W, has_up),
                              jnp.logical_or(r1 < GRID_W + FFN_TILE, has_dn))
    tap_on = jnp.where(lax.broadcasted_iota(jnp.int32, (9, 1), 0) // 3 == 1, 1.0,
                       jnp.where(is_ctx, 0.0, 1.0))
    cw = cw_ref[...] * tap_on

    x_ext = jnp.concatenate([xp_ref[...], xc_ref[...], xn_ref[...]], axis=0)
    h_ext = jnp.where(in_grid, _modnorm(x_ext, mod_ref, nw2_ref, 3, 4), 0.0).astype(BF16)
    h_cur = h_ext[GRID_W:GRID_W + FFN_TILE]

    ext_all = jnp.dot(h_ext, wup_ref[:, :D_FF], preferred_element_type=F32)
    gate_all = jnp.dot(h_cur, wup_ref[:, D_FF:], preferred_element_type=F32)

    u = []
    for c in range(D_FF // FFN_SLAB):
        sl = slice(c * FFN_SLAB, (c + 1) * FFN_SLAB)
        ext = ext_all[:, sl]
        q = []
        for dx in range(3):
            qs = None
            for dy in range(3):
                term = cw[dy * 3 + dx:dy * 3 + dx + 1, sl] * ext[dy * GRID_W:dy * GRID_W + FFN_TILE, :]
                qs = term if qs is None else qs + term
            q.append(qs)
        acc = (q[1] + cb_ref[:, sl] + pltpu.roll(q[0], 1, 0) * keep_left
               + pltpu.roll(q[2], FFN_TILE - 1, 0) * keep_right)
        u.append((jax.nn.gelu(acc) * gate_all[:, sl]).astype(BF16))
    f = jnp.dot(jnp.concatenate(u, axis=1), wd_ref[...], preferred_element_type=F32)

    y = xc_ref[...] + mod_ref[0, 5:6, :] * _rms(f, nw3_ref[...])
    if len(out_refs) == 1:
        out_refs[0][...] = y
    else:
        @pl.when(is_ctx)
        def _():
            out_refs[0][...] = y

        @pl.when(jnp.logical_not(is_ctx))
        def _():
            out_refs[1][...] = y


def _gate_rows(w_gate):
    idx, keep = [], []
    for p in range(N_HEADS * SUBLANES):
        h, j = divmod(p, SUBLANES)
        d, kind = j % 2, j // 2
        idx.append(d * 2 * N_HEADS + kind * N_HEADS + h if j < 4 else 0)
        keep.append(1.0 if j < 4 else 0.0)
    return w_gate[..., jnp.array(idx)] * jnp.array(keep, w_gate.dtype)


def kernel(x_prompt, x_sample, state_C, state_n, state_m, c, c_ctx, w_mod, b_mod, norm_w, w_in, b_in,
           mlstm_gate_b, pool_w, pool_scale, gmlp_norm_w, gmlp_ws, gmlp_b, w_br, w_out, w_up, conv_w,
           conv_b, w_down):
    n_ctx, ctx_len, d = x_prompt.shape
    n_lat, lat_len, _ = x_sample.shape
    depth = w_in.shape[0]
    n_ctx_tok = n_ctx * ctx_len
    n_lat_tok = n_lat * lat_len
    n_tok = n_ctx_tok + n_lat_tok
    assert d == D_MODEL and FFN_TILE % ctx_len == 0 and lat_len % FFN_TILE == 0
    assert n_ctx_tok % FFN_TILE == 0 and ctx_len % TOK_TILE == 0 and lat_len % TOK_TILE == 0
    n_gate = 2 * 2 * N_HEADS
    gate0 = 4 * D_MODEL

    x_parts = (x_prompt.reshape(n_ctx_tok, d), x_sample.reshape(n_lat_tok, d))

    def tok_specs(n_parts, tm, tile=lambda i: i):
        if n_parts == 1:
            return [pl.BlockSpec((tm, d), lambda i, *_: (tile(i), 0))]
        n_a = n_ctx_tok // tm
        return [pl.BlockSpec((tm, d), lambda i, *_: (jnp.minimum(tile(i), n_a - 1), 0)),
                pl.BlockSpec((tm, d), lambda i, *_: (jnp.maximum(tile(i) - n_a, 0), 0))]

    cvec = jnp.zeros((SUBLANES, d), F32).at[0].set(c_ctx).at[1:1 + n_lat].set(c)
    mod = _modulation(cvec, w_mod, b_mod)

    def mod_spec(tm):
        def row(i):
            t0 = i * tm
            return jnp.where(t0 < n_ctx_tok, 0, 1 + jnp.maximum(t0 - n_ctx_tok, 0) // lat_len)
        return row

    tn_in = 2048
    n_col_tiles = MAIN_WIDTH // tn_in
    k_scale = jnp.ones((MAIN_WIDTH,), F32).at[COL_K * d:(COL_K + 1) * d].set(HEAD_DIM ** -0.5)
    w_in16 = w_in.astype(BF16)
    w_main = (jnp.concatenate([w_in16[:, :, :gate0], w_in16[:, :, gate0 + n_gate:]], axis=2)
              * k_scale.astype(BF16))
    b_main = (jnp.concatenate([b_in[:, :gate0], b_in[:, gate0 + n_gate:]], axis=1)
              * k_scale).reshape(depth, 1, MAIN_WIDTH)
    row_tok = mod_spec(TOK_TILE)
    row_ffn = mod_spec(FFN_TILE)
    vec = lambda a: a.reshape(1, -1)
    states = None
    t_idx = np.arange(TOK_TILE)[:, None]
    u_idx = np.arange(TOK_TILE + 2 * POOL_HALO)[None, :] - POOL_HALO
    pool_bands = jnp.asarray(np.stack([(u_idx >= t_idx - w // 2) & (u_idx < t_idx + w // 2)
                                       for w in POOL_WINDOWS]), BF16)
    n_grow = N_HEADS * SUBLANES
    scan_lanes = 2048

    for l in range(depth):
        last_layer = l == depth - 1
        wg_rows = _gate_rows(w_in[l, :, gate0:gate0 + n_gate]).T.astype(BF16)
        bga_rows = _gate_rows(b_in[l, gate0:gate0 + n_gate]).reshape(-1, 1)
        bgb_rows = _gate_rows(mlstm_gate_b[l].reshape(n_gate)).reshape(-1, 1)
        mod_l = mod[l]
        nw = norm_w[l]
        tm_proj = 1024 if len(x_parts) == 2 else 2048
        assert n_ctx_tok % tm_proj == 0 and lat_len % tm_proj == 0
        row_in = mod_spec(tm_proj)

        z, gt = pl.pallas_call(
            functools.partial(_inproj_kernel, n_x=len(x_parts), n_first=n_ctx_tok // tm_proj),
            grid=(n_tok // tm_proj, n_col_tiles),
            in_specs=tok_specs(len(x_parts), tm_proj) + [
                pl.BlockSpec((1, N_MOD, d), lambda i, j: (row_in(i), 0, 0)),
                pl.BlockSpec((1, d), lambda i, j: (0, 0)),
                pl.BlockSpec((None, d, tn_in), lambda i, j, l=l: (l, 0, j)),
                pl.BlockSpec((None, 1, tn_in), lambda i, j, l=l: (l, 0, j)),
                pl.BlockSpec((n_grow, d), lambda i, j: (0, 0)),
                pl.BlockSpec((n_grow, 1), lambda i, j: (0, 0)),
                pl.BlockSpec((n_grow, 1), lambda i, j: (0, 0)),
            ],
            out_specs=[
                pl.BlockSpec((tm_proj, tn_in), lambda i, j: (i, j)),
                pl.BlockSpec((n_grow, tm_proj), lambda i, j: (0, i)),
            ],
            out_shape=[
                jax.ShapeDtypeStruct((n_tok, MAIN_WIDTH), BF16),
                jax.ShapeDtypeStruct((n_grow, n_tok), F32),
            ],
            scratch_shapes=[pltpu.VMEM((tm_proj, d), BF16)],
            compiler_params=_params(("arbitrary", "arbitrary")),
            name="inproj",
        )(*x_parts, mod_l, vec(nw[0]), w_main, b_main, wg_rows, bga_rows, bgb_rows)

        scan_spec = pl.BlockSpec((n_grow, scan_lanes), lambda i: (0, i))
        scans = pl.pallas_call(
            _gates_kernel,
            grid=(n_tok // scan_lanes,),
            in_specs=[scan_spec],
            out_specs=[scan_spec] * 3,
            out_shape=[jax.ShapeDtypeStruct((n_grow, n_tok), F32)] * 3,
            compiler_params=_params(("arbitrary",)),
            name="gates",
        )(gt)

        ya_ctx, states = _mlstm(z, scans, 0, n_ctx, ctx_len, prev=states, heads=N_HEADS)
        ya_lat, _ = _mlstm(z, scans, n_ctx_tok, n_lat, lat_len,
                           init=(state_C, state_n, state_m), layer=l)

        tm = TOK_TILE
        halo_per_tile = tm // POOL_HALO
        n_halo = n_tok // POOL_HALO
        zcol = lambda col: pl.BlockSpec((tm, d), lambda i: (i, col))
        full = lambda shape: pl.BlockSpec(shape, lambda i: (0,) * len(shape))
        x = pl.pallas_call(
            functools.partial(_mix_merge_kernel, n_x=len(x_parts), n_first=n_ctx_tok // tm,
                              n_ctx_tok=n_ctx_tok, ctx_len=ctx_len, lat_len=lat_len),
            grid=(n_tok // tm,),
            in_specs=tok_specs(len(x_parts), tm) + tok_specs(2, tm) + [
                pl.BlockSpec((POOL_HALO, d),
                             lambda i: (jnp.maximum(i * halo_per_tile - 1, 0), COL_P)),
                zcol(COL_P),
                pl.BlockSpec((POOL_HALO, d),
                             lambda i: (jnp.minimum((i + 1) * halo_per_tile, n_halo - 1), COL_P)),
                zcol(COL_U), zcol(COL_VG), zcol(COL_M), zcol(COL_M + 1), zcol(COL_M + 2),
                pl.BlockSpec((1, N_MOD, d), lambda i: (row_tok(i), 0, 0)),
                full((1, d)),
                full(pool_bands.shape),
                full((N_GROUPS, GROUP_DIM, GROUP_DIM)), full((1, d)), full((1, d)),
                full((N_GROUPS, CHUNK, CHUNK)), full((CHUNK, N_GROUPS)),
                pl.BlockSpec((None, 3, d, d), lambda i, l=l: (l, 0, 0, 0)),
                pl.BlockSpec((None, d, d), lambda i, l=l: (l, 0, 0)),
            ],
            out_specs=pl.BlockSpec((tm, d), lambda i: (i, 0)),
            out_shape=jax.ShapeDtypeStruct((n_tok, d), F32),
            scratch_shapes=[pltpu.VMEM((3, d, d), BF16), pltpu.VMEM((d, d), BF16)],
            compiler_params=_params(("arbitrary",)),
            name="mix_merge",
        )(*x_parts, ya_ctx, ya_lat, z, z, z, z, z, z, z, z, mod_l, vec(nw[1]),
          pool_bands, pool_w[l].astype(BF16), vec(pool_scale[l]), vec(gmlp_norm_w[l]),
          gmlp_ws[l].astype(BF16), gmlp_b[l].T, w_br, w_out)
        x_parts = (x,)

        tf = FFN_TILE
        rows_per_tile = tf // GRID_W
        n_rows = n_tok // GRID_W
        n_ctx_tiles = n_ctx_tok // tf
        if last_layer:
            out_specs = [pl.BlockSpec((tf, d), lambda i: (jnp.minimum(i, n_ctx_tiles - 1), 0)),
                         pl.BlockSpec((tf, d), lambda i: (jnp.maximum(i - n_ctx_tiles, 0), 0))]
            out_shape = [jax.ShapeDtypeStruct((n_ctx_tok, d), F32),
                         jax.ShapeDtypeStruct((n_lat_tok, d), F32)]
        else:
            out_specs = [pl.BlockSpec((tf, d), lambda i: (i, 0))]
            out_shape = [jax.ShapeDtypeStruct((n_tok, d), F32)]
        x_parts = tuple(pl.pallas_call(
            functools.partial(_ffn_kernel, n_ctx_tok=n_ctx_tok, ctx_len=ctx_len, lat_len=lat_len),
            grid=(n_tok // tf,),
            in_specs=[
                pl.BlockSpec((GRID_W, d), lambda i: (jnp.maximum(i * rows_per_tile - 1, 0), 0)),
                pl.BlockSpec((tf, d), lambda i: (i, 0)),
                pl.BlockSpec((GRID_W, d),
                             lambda i: (jnp.minimum((i + 1) * rows_per_tile, n_rows - 1), 0)),
                pl.BlockSpec((1, N_MOD, d), lambda i: (row_ffn(i), 0, 0)),
                full((1, d)), full((1, d)), full((d, 2 * D_FF)),
                full((9, D_FF)), full((1, D_FF)),
                pl.BlockSpec((None, D_FF, d), lambda i, l=l: (l, 0, 0)),
            ],
            out_specs=out_specs,
            out_shape=out_shape,
            scratch_shapes=[pltpu.VMEM((D_FF, d), BF16)],
            compiler_params=_params(("arbitrary",)),
            name="ffn",
        )(x, x, x, mod_l, vec(nw[2]), vec(nw[3]), w_up[l].astype(BF16),
          conv_w[l].reshape(9, D_FF), vec(conv_b[l]), w_down))

    new_c, new_n, new_m = states
    return (x_parts[0].reshape(n_ctx, ctx_len, d), x_parts[1].reshape(n_lat, lat_len, d), new_c,
            new_n.reshape(n_ctx, depth, 2, N_HEADS, HEAD_DIM), new_m.reshape(n_ctx, depth, 2, N_HEADS))
```

```python
import functools

import jax
import jax.numpy as jnp
import numpy as np
from jax import lax
from jax.experimental import pallas as pl
from jax.experimental.pallas import tpu as pltpu

F32 = jnp.float32
BF16 = jnp.bfloat16

D_MODEL = 1024
N_HEADS = 4
HEAD_DIM = D_MODEL // N_HEADS
CHUNK = 128
POOL_WINDOWS = (2, 4, 8, 16)
N_GROUPS = 4
GROUP_DIM = D_MODEL // N_GROUPS
D_FF = 2816
GRID_W = 64
N_MOD = 6
RMS_EPS = 1e-6
LANES = 128
SUBLANES = 8
VMEM_LIMIT = 56 * 1024 * 1024

COL_Q, COL_K, COL_V, COL_O, COL_P, COL_U, COL_VG, COL_M = 0, 1, 2, 3, 4, 5, 6, 7
MAIN_WIDTH = 10 * D_MODEL

NT_DIMS = (((1,), (1,)), ((), ()))


def _rms(x, w):
    return x * lax.rsqrt(jnp.mean(x * x, axis=-1, keepdims=True) + RMS_EPS) * w


def _log_sigmoid(x):
    return jnp.minimum(x, 0.0) - jnp.log1p(jnp.exp(-jnp.abs(x)))


def _params(sem):
    return pltpu.CompilerParams(dimension_semantics=sem, vmem_limit_bytes=VMEM_LIMIT)


def _mod_kernel(c_ref, w_ref, b_ref, o_ref):
    c = c_ref[...]
    s = (c * jax.nn.sigmoid(c)).astype(BF16)
    o_ref[...] = jnp.dot(s, w_ref[...].astype(BF16), preferred_element_type=F32) + b_ref[...]


def _modulation(cvec, w_mod, b_mod):
    depth = w_mod.shape[0]
    n = N_MOD * D_MODEL
    tn = 1536
    out = pl.pallas_call(
        _mod_kernel,
        grid=(depth, n // tn),
        in_specs=[
            pl.BlockSpec((SUBLANES, D_MODEL), lambda l, j: (0, 0)),
            pl.BlockSpec((None, D_MODEL, tn), lambda l, j: (l, 0, j)),
            pl.BlockSpec((None, 1, tn), lambda l, j: (l, 0, j)),
        ],
        out_specs=pl.BlockSpec((None, SUBLANES, tn), lambda l, j: (l, 0, j)),
        out_shape=jax.ShapeDtypeStruct((depth, SUBLANES, n), F32),
        compiler_params=_params(("arbitrary", "arbitrary")),
        name="modulation",
    )(cvec, w_mod, b_mod.reshape(depth, 1, n))
    return out.reshape(depth, SUBLANES, N_MOD, D_MODEL)


def _modnorm(x, mod_ref, nw_ref, k_shift, k_scale):
    h = _rms(x, nw_ref[...])
    return h * (1.0 + mod_ref[0, k_scale:k_scale + 1, :]) + mod_ref[0, k_shift:k_shift + 1, :]


def _pick(refs, first):
    if len(refs) == 1:
        return refs[0][...]
    return jnp.where(first, refs[0][...], refs[1][...])


def _inproj_kernel(*refs, n_x, n_first):
    x_refs = refs[:n_x]
    mod_ref, nw_ref, w_ref, b_ref, wgt_ref, bgta_ref, bgtb_ref, z_ref, gt_ref, h_scr = refs[n_x:]

    @pl.when(pl.program_id(1) == 0)
    def _():
        x = _pick(x_refs, pl.program_id(0) < n_first)
        hb = _modnorm(x, mod_ref, nw_ref, 0, 1).astype(BF16)
        h_scr[...] = hb
        gt_ref[...] = (lax.dot_general(wgt_ref[...], hb, NT_DIMS, preferred_element_type=F32)
                       + bgta_ref[...] + bgtb_ref[...])

    z = jnp.dot(h_scr[...], w_ref[...], preferred_element_type=F32) + b_ref[...]
    z_ref[...] = z.astype(z_ref.dtype)


def _gates_kernel(g_ref, a_ref, b_ref, cm_ref):
    g = g_ref[...]
    fwd = lax.broadcasted_iota(jnp.int32, g.shape, 0) % SUBLANES == 0
    lane = lax.broadcasted_iota(jnp.int32, g.shape, 1) % CHUNK
    lf = _log_sigmoid(pltpu.roll(g, g.shape[0] - 2, 0))
    pre, suf = _chunk_scans(lf, jnp.add, 0.0, lane)
    b = jnp.where(fwd, pre, suf)
    a = g - b
    pre, suf = _chunk_scans(a, jnp.maximum, -jnp.inf, lane)
    a_ref[...] = a
    b_ref[...] = b
    cm_ref[...] = jnp.where(fwd, pre, suf)


def _chunk_scans(x, op, fill, lane):
    width = x.shape[1]
    fwd = bwd = x
    k = 1
    while k < CHUNK:
        fwd = op(fwd, jnp.where(lane >= k, pltpu.roll(fwd, k, 1), fill))
        bwd = op(bwd, jnp.where(lane < CHUNK - k, pltpu.roll(bwd, width - k, 1), fill))
        k *= 2
    return fwd, bwd


def _mlstm_head(*refs, seq_len, has_init, emit_state, n_prev):
    q_ref, k_ref, v_ref, o_ref, a_ref, b_ref, cm_ref = refs[:7]
    pos = 7
    if has_init:
        c0_ref, n0_ref, m0_ref = refs[pos:pos + 3]
        pos += 3
    if n_prev:
        cprev_ref, nprev_ref, mprev_ref = refs[pos:pos + 3]
        pos += 3
    ya_ref = refs[pos]
    pos += 1
    if emit_state:
        cout_ref, nout_ref, mout_ref = refs[pos:pos + 3]
        pos += 3
    rows_scr, wk16_scr, dec_scr, vt_scr, vtw_scr, cb_scr, n1_scr, c_scr = refs[pos:pos + 8]

    n_chunks = seq_len // CHUNK
    unroll = min(4, n_chunks)
    row81 =lax.broadcasted_iota(jnp.int32, (SUBLANES, 1), 0)
    is_fwd1 = row81 == 0
    row_n = lax.broadcasted_iota(jnp.int32, (SUBLANES, HEAD_DIM), 0)

    chunk = lambda ref, c: ref[:, c * CHUNK:(c + 1) * CHUNK]
    a_c = [chunk(a_ref, c) for c in range(n_chunks)]
    b_c = [chunk(b_ref, c) for c in range(n_chunks)]
    cm_c = [chunk(cm_ref, c) for c in range(n_chunks)]
    row8 = lax.broadcasted_iota(jnp.int32, (SUBLANES, CHUNK), 0)
    lane8 = lax.broadcasted_iota(jnp.int32, (SUBLANES, CHUNK), 1)
    last = lane8 == jnp.where(row8 == 0, CHUNK - 1, 0)
    amax_c = [jnp.max(a, axis=1, keepdims=True) for a in a_c]
    bend_c = [jnp.sum(jnp.where(last, b, 0.0), axis=1, keepdims=True) for b in b_c]

    if has_init:
        m_state = jnp.where(is_fwd1, m0_ref[0], jnp.where(row81 == 1, m0_ref[1], 0.0))
        n_init = jnp.where(row_n == 0, n0_ref[0], n0_ref[1])
        c_scr[...] = c0_ref[...]
    else:
        m_state = jnp.zeros((SUBLANES, 1), F32)
        n_init = jnp.zeros((SUBLANES, HEAD_DIM), F32)
        c_scr[...] = jnp.zeros(c_scr.shape, F32)

    m_before, top = [], []
    for j in range(n_chunks):
        jr = n_chunks - 1 - j
        amax = jnp.where(is_fwd1, amax_c[j], amax_c[jr])
        bend = jnp.where(is_fwd1, bend_c[j], bend_c[jr])
        m_before.append(m_state)
        top.append(jnp.maximum(m_state, amax))
        m_state = bend + top[-1]

    for c in range(n_chunks):
        cr = n_chunks - 1 - c
        m_c = jnp.where(is_fwd1, m_before[c], m_before[cr])
        top_c = jnp.where(is_fwd1, top[c], top[cr])
        big_m = jnp.maximum(m_c, cm_c[c])
        w_k = jnp.exp(a_c[c] - top_c)
        rows_scr[c, 0] = a_c[c]
        rows_scr[c, 1] = big_m
        rows_scr[c, 2] = jnp.exp(m_c - big_m)
        rows_scr[c, 3] = jnp.exp(-b_c[c] - big_m)
        rows_scr[c, 4] = w_k
        wk16_scr[c] = jnp.concatenate([w_k, jnp.zeros_like(w_k)], axis=0).astype(BF16)
        dec_scr[c] = jnp.broadcast_to(jnp.exp(m_c - top_c), (SUBLANES, HEAD_DIM))

    def load_k(r0):
        return k_ref[pl.ds(r0, CHUNK), :]

    def pass_a(j, nst):
        c = n_chunks - 1 - j
        r0 = pl.multiple_of(c * CHUNK, CHUNK)
        ks = load_k(r0)
        vt = v_ref[pl.ds(r0, CHUNK), :].astype(F32).T
        w_k = rows_scr[c, 4]
        vt_scr[c] = vt.astype(BF16)
        vtw_scr[c] = (vt * w_k[0:1, :]).astype(BF16)
        c_in = c_scr[1]
        cb_scr[c] = c_in.astype(BF16)
        n1_scr[c] = nst
        dec = dec_scr[c]
        c_scr[1] = dec[1:2, :] * c_in + jnp.dot((vt * w_k[1:2, :]).astype(BF16), ks,
                                                preferred_element_type=F32)
        n_upd = jnp.dot(wk16_scr[c], ks, preferred_element_type=F32)[:SUBLANES]
        return dec * nst + n_upd

    n_rev = lax.fori_loop(0, n_chunks, pass_a, n_init, unroll=unroll)

    row_id = lax.broadcasted_iota(jnp.int32, (CHUNK, CHUNK), 0)
    col_id = lax.broadcasted_iota(jnp.int32, (CHUNK, CHUNK), 1)
    tri = (row_id <= col_id, row_id >= col_id)

    def pass_b(c, nst):
        r0 = pl.multiple_of(c * CHUNK, CHUNK)
        q = q_ref[pl.ds(r0, CHUNK), :]
        ks = load_k(r0)
        vt = vt_scr[c]
        a_r, bigm_r, winter_r, floor_r = (rows_scr[c, i] for i in range(4))
        st_all = lax.dot_general(ks, q, NT_DIMS, preferred_element_type=F32)
        n_rows = jnp.where(row_n == 0, nst, n1_scr[c])
        n_rows = jnp.concatenate([n_rows, jnp.zeros_like(n_rows)], axis=0).astype(BF16)
        qn = lax.dot_general(n_rows, q, NT_DIMS, preferred_element_type=F32)
        c0 = c_scr[0]
        sts, invs = [], []
        for d in range(2):
            row = slice(d, d + 1)
            a_st = jnp.broadcast_to(a_r[row, :], (CHUNK, CHUNK)).T
            st = st_all * jnp.exp(jnp.where(tri[d], a_st - bigm_r[row, :], -jnp.inf))
            den = jnp.sum(st, axis=0, keepdims=True) + winter_r[row, :] * qn[row, :]
            invs.append(1.0 / jnp.maximum(jnp.abs(den), floor_r[row, :]))
            sts.append(st.astype(BF16))
        intra = jnp.dot(vt, jnp.concatenate(sts, axis=1), preferred_element_type=F32)
        if n_chunks == unroll:
            h = (intra[:, :CHUNK] * invs[0] + intra[:, CHUNK:] * invs[1]).T
            for d in range(2):
                row = slice(d, d + 1)
                cb = c0.astype(BF16) if d == 0 else cb_scr[c]
                inter = lax.dot_general(q, cb, NT_DIMS, preferred_element_type=F32)
                w_col = jnp.broadcast_to(winter_r[row, :] * invs[d], (CHUNK, CHUNK)).T
                h = h + inter * jnp.concatenate([w_col] * (HEAD_DIM // CHUNK), axis=1)
        else:
            ht = None
            for d in range(2):
                row = slice(d, d + 1)
                cb = c0.astype(BF16) if d == 0 else cb_scr[c]
                num_t = (intra[:, d * CHUNK:(d + 1) * CHUNK]
                         + winter_r[row, :] * lax.dot_general(cb, q, NT_DIMS,
                                                              preferred_element_type=F32))
                ht = num_t * invs[d] if ht is None else ht + num_t * invs[d]
            h = ht.T
        gate = jax.nn.sigmoid(o_ref[pl.ds(r0, CHUNK), :].astype(F32))
        ya_ref[pl.ds(r0, CHUNK), :] = (gate * h).astype(ya_ref.dtype)

        dec = dec_scr[c]
        c_scr[0] = dec[0:1, :] * c0 + jnp.dot(vtw_scr[c], ks, preferred_element_type=F32)
        n_upd = jnp.dot(wk16_scr[c], ks, preferred_element_type=F32)[:SUBLANES]
        return dec * nst + n_upd

    n_fwd = lax.fori_loop(0, n_chunks, pass_b, n_init, unroll=unroll)

    if emit_state:
        if n_prev:
            cout_ref[:n_prev] = cprev_ref[...]
            nout_ref[:n_prev] = nprev_ref[...]
            mout_ref[:n_prev] = mprev_ref[...]
        cout_ref[n_prev] = c_scr[...]
        nout_ref[n_prev, 0] = n_fwd[0:1, :]
        nout_ref[n_prev, 1] = n_rev[1:2, :]
        mout_ref[n_prev, 0] = m_state[0:1, :]
        mout_ref[n_prev, 1] = m_state[1:2, :]


def _mlstm_kernel(*refs, heads, seq_len, has_init, emit_state, n_prev):
    if heads == 1:
        return _mlstm_head(*refs, seq_len=seq_len, has_init=has_init, emit_state=emit_state,
                           n_prev=n_prev)
    assert not has_init
    n_state = 3 * (bool(n_prev) + emit_state)
    n_scratch = len(refs) - 8 - n_state
    for hh in range(heads):
        cols = slice(hh * HEAD_DIM, (hh + 1) * HEAD_DIM)
        views = [r.at[:, cols] for r in refs[:4]]
        views += [r.at[hh * SUBLANES:(hh + 1) * SUBLANES, :] for r in refs[4:7]]
        pos = 7
        if n_prev:
            views += [r.at[:, :, hh] for r in refs[pos:pos + 3]]
            pos += 3
        views.append(refs[pos].at[:, cols])
        pos += 1
        if emit_state:
            views += [r.at[:, :, hh] for r in refs[pos:pos + 3]]
            pos += 3
        assert len(refs) - pos == n_scratch
        _mlstm_head(*views, *refs[pos:], seq_len=seq_len, has_init=False, emit_state=emit_state,
                    n_prev=n_prev)


def _mlstm(z, scans, tok0, n_batch, seq_len, init=None, layer=0, prev=None, heads=1):
    blk0 = tok0 // seq_len
    n_chunks = seq_len // CHUNK
    has_init = init is not None
    emit_state = not has_init
    n_prev = 0 if prev is None else prev[0].shape[1]

    head_blocks = N_HEADS // heads
    hdim = None if heads == 1 else heads

    def zspec(col):
        return pl.BlockSpec((seq_len, heads * HEAD_DIM),
                            lambda b, h: (blk0 + b, col * head_blocks + h))

    scan_spec = pl.BlockSpec((heads * SUBLANES, seq_len), lambda b, h: (h, blk0 + b))
    in_specs = [zspec(COL_Q), zspec(COL_K), zspec(COL_V), zspec(COL_O),
                scan_spec, scan_spec, scan_spec]
    args = [z, z, z, z, *scans]

    def state_specs(n_layers):
        return [
            pl.BlockSpec((None, n_layers, 2, hdim, HEAD_DIM, HEAD_DIM),
                         lambda b, h: (b, 0, 0, h, 0, 0)),
            pl.BlockSpec((None, n_layers, 2, hdim, 1, HEAD_DIM), lambda b, h: (b, 0, 0, h, 0, 0)),
            pl.BlockSpec((None, n_layers, 2, hdim, 1, 1), lambda b, h: (b, 0, 0, h, 0, 0)),
        ]

    if has_init:
        sc, sn, sm = init
        nb, depth = sc.shape[:2]
        in_specs += [
            pl.BlockSpec((None, None, 2, None, HEAD_DIM, HEAD_DIM),
                         lambda b, h: (b, layer, 0, h, 0, 0)),
            pl.BlockSpec((None, None, 2, None, 1, HEAD_DIM), lambda b, h: (b, layer, 0, h, 0, 0)),
            pl.BlockSpec((None, None, 2, None, 1, 1), lambda b, h: (b, layer, 0, h, 0, 0)),
        ]
        args += [sc, sn.reshape(nb, depth, 2, N_HEADS, 1, HEAD_DIM),
                 sm.reshape(nb, depth, 2, N_HEADS, 1, 1)]
    if n_prev:
        in_specs += state_specs(n_prev)
        args += list(prev)
    out_specs = [pl.BlockSpec((seq_len, heads * HEAD_DIM), lambda b, h: (b, h))]
    out_shape = [jax.ShapeDtypeStruct((n_batch * seq_len, D_MODEL), BF16)]
    if emit_state:
        n_out = n_prev + 1
        out_specs += state_specs(n_out)
        out_shape += [
            jax.ShapeDtypeStruct((n_batch, n_out, 2, N_HEADS, HEAD_DIM, HEAD_DIM), F32),
            jax.ShapeDtypeStruct((n_batch, n_out, 2, N_HEADS, 1, HEAD_DIM), F32),
            jax.ShapeDtypeStruct((n_batch, n_out, 2, N_HEADS, 1, 1), F32),
        ]
    outs = pl.pallas_call(
        functools.partial(_mlstm_kernel, heads=heads, seq_len=seq_len, has_init=has_init,
                          emit_state=emit_state, n_prev=n_prev),
        grid=(n_batch, head_blocks),
        in_specs=in_specs,
        out_specs=out_specs,
        out_shape=out_shape,
        scratch_shapes=[
            pltpu.VMEM((n_chunks, 5, SUBLANES, CHUNK), F32),
            pltpu.VMEM((n_chunks, 2 * SUBLANES, CHUNK), BF16),
            pltpu.VMEM((n_chunks, SUBLANES, HEAD_DIM), F32),
            pltpu.VMEM((n_chunks, HEAD_DIM, CHUNK), BF16),
            pltpu.VMEM((n_chunks, HEAD_DIM, CHUNK), BF16),
            pltpu.VMEM((n_chunks, HEAD_DIM, HEAD_DIM), BF16),
            pltpu.VMEM((n_chunks, SUBLANES, HEAD_DIM), F32),
            pltpu.VMEM((2, HEAD_DIM, HEAD_DIM), F32),
        ],
        compiler_params=_params(("arbitrary", "arbitrary")),
        name="mlstm_ctx" if emit_state else "mlstm_lat",
    )(*args)
    if emit_state:
        return outs[0], tuple(outs[1:])
    return outs[0], None


TOK_TILE = 256
POOL_HALO = 64


def _mix_merge_kernel(*refs, n_x, n_first, n_ctx_tok, ctx_len, lat_len):
    x_refs = refs[:n_x]
    (ya_ctx_ref, ya_lat_ref, pp_ref, pc_ref, pn_ref, u_ref, vg_ref, m0_ref, m1_ref, m2_ref,
     mod_ref, nw_ref, band_ref, pw_ref, ps_ref, gnw_ref, ws_ref, gb_ref, wbr32_ref, wout32_ref,
     o_ref, wbr_ref, wout_ref) = refs[n_x:]

    @pl.when(pl.program_id(0) == 0)
    def _():
        for n in range(wbr_ref.shape[0]):
            wbr_ref[n] = wbr32_ref[n].astype(BF16)
        wout_ref[...] = wout32_ref[...].astype(BF16)

    tok0 = pl.program_id(0) * TOK_TILE
    first = pl.program_id(0) < n_first
    is_ctx = tok0 < n_ctx_tok
    seq_len = jnp.where(is_ctx, ctx_len, lat_len)
    base = jnp.where(is_ctx, 0, n_ctx_tok)
    seq_start = base + ((tok0 - base) // seq_len) * seq_len
    seq_end = seq_start + seq_len

    t_col = tok0 + lax.broadcasted_iota(jnp.int32, (TOK_TILE, 1), 0)
    p_prev = jnp.where(tok0 > seq_start, pp_ref[...], jnp.zeros_like(pp_ref))
    p_next = jnp.where(tok0 + TOK_TILE < seq_end, pn_ref[...], jnp.zeros_like(pn_ref))
    p_ext = jnp.concatenate([p_prev, pc_ref[...], p_next], axis=0)
    yb = []
    for g, win in enumerate(POOL_WINDOWS):
        half = win // 2
        sl = slice(g * GROUP_DIM, (g + 1) * GROUP_DIM)
        acc = jnp.dot(band_ref[g], p_ext[:, sl], preferred_element_type=F32)
        count = (jnp.minimum(t_col + half, seq_end) - jnp.maximum(t_col - half, seq_start)).astype(F32)
        y = (acc / count - pc_ref[:, sl].astype(F32)).astype(BF16)
        yb.append(jnp.dot(y, pw_ref[g], preferred_element_type=F32) * ps_ref[:, sl])
    yb = jnp.concatenate(yb, axis=1).astype(BF16)

    u = jax.nn.gelu(u_ref[...].astype(F32))
    vg = _rms(jax.nn.gelu(vg_ref[...].astype(F32)), gnw_ref[...]).astype(BF16)
    yc = []
    for ch in range(TOK_TILE // CHUNK):
        rows = slice(ch * CHUNK, (ch + 1) * CHUNK)
        parts = []
        for g in range(N_GROUPS):
            sl = slice(g * GROUP_DIM, (g + 1) * GROUP_DIM)
            s = jnp.dot(ws_ref[g], vg[rows, sl], preferred_element_type=F32) + gb_ref[:, g:g + 1]
            parts.append(u[rows, sl] * s)
        yc.append(jnp.concatenate(parts, axis=1))
    yc = jnp.concatenate(yc, axis=0).astype(BF16)

    ys = (_pick((ya_ctx_ref, ya_lat_ref), first), yb, yc)
    mixed = None
    for n, (y, m_ref) in enumerate(zip(ys, (m0_ref, m1_ref, m2_ref))):
        br = jnp.dot(y, wbr_ref[n], preferred_element_type=F32)
        term = jax.nn.sigmoid(m_ref[...].astype(F32)) * br
        mixed = term if mixed is None else mixed + term
    o = jnp.dot(mixed.astype(BF16), wout_ref[...], preferred_element_type=F32)
    o_ref[...] = _pick(x_refs, first) + mod_ref[0, 2:3, :] * _rms(o, nw_ref[...])


FFN_TILE = 512
FFN_SLAB = 256


def _ffn_kernel(xp_ref, xc_ref, xn_ref, mod_ref, nw2_ref, nw3_ref, wup_ref, cw_ref, cb_ref, wd32_ref,
                *rest, n_ctx_tok, ctx_len, lat_len):
    *out_refs, wd_ref = rest
    i = pl.program_id(0)

    @pl.when(i == 0)
    def _():
        wd_ref[...] = wd32_ref[...].astype(BF16)

    tok0 = i * FFN_TILE
    is_ctx = tok0 < n_ctx_tok
    tiles_per_lat = lat_len // FFN_TILE
    lat_tile = jnp.maximum(tok0 - n_ctx_tok, 0) // FFN_TILE % tiles_per_lat
    has_up = jnp.logical_and(jnp.logical_not(is_ctx), lat_tile != 0)
    has_dn = jnp.logical_and(jnp.logical_not(is_ctx), lat_tile != tiles_per_lat - 1)
    ext_rows = FFN_TILE + 2 * GRID_W
    r = lax.broadcasted_iota(jnp.int32, (FFN_TILE, FFN_SLAB), 0)
    col_ctx = r % ctx_len
    col_lat = r % GRID_W
    keep_left = jnp.where(jnp.where(is_ctx, col_ctx, col_lat) != 0, 1.0, 0.0)
    keep_right = jnp.where(jnp.where(is_ctx, col_ctx - (ctx_len - 1), col_lat - (GRID_W - 1)) != 0,
                           1.0, 0.0)
    r1 = lax.broadcasted_iota(jnp.int32, (ext_rows, 1), 0)
    in_grid = jnp.logical_and(jnp.logical_or(r1 >= GRID_W, has_up),
                              jnp.logical_or(r1 < GRID_W + FFN_TILE, has_dn))
    tap_on = jnp.where(lax.broadcasted_iota(jnp.int32, (9, 1), 0) // 3 == 1, 1.0,
                       jnp.where(is_ctx, 0.0, 1.0))
    cw = cw_ref[...] * tap_on

    x_ext = jnp.concatenate([xp_ref[...], xc_ref[...], xn_ref[...]], axis=0)
    h_ext = jnp.where(in_grid, _modnorm(x_ext, mod_ref, nw2_ref, 3, 4), 0.0).astype(BF16)
    h_cur = h_ext[GRID_W:GRID_W + FFN_TILE]

    ext_all = jnp.dot(h_ext, wup_ref[:, :D_FF], preferred_element_type=F32)
    gate_all = jnp.dot(h_cur, wup_ref[:, D_FF:], preferred_element_type=F32)

    u = []
    for c in range(D_FF // FFN_SLAB):
        sl = slice(c * FFN_SLAB, (c + 1) * FFN_SLAB)
        ext = ext_all[:, sl]
        q = []
        for dx in range(3):
            qs = None
            for dy in range(3):
                term = cw[dy * 3 + dx:dy * 3 + dx + 1, sl] * ext[dy * GRID_W:dy * GRID_W + FFN_TILE, :]
                qs = term if qs is None else qs + term
            q.append(qs)
        acc = (q[1] + cb_ref[:, sl] + pltpu.roll(q[0], 1, 0) * keep_left
               + pltpu.roll(q[2], FFN_TILE - 1, 0) * keep_right)
        u.append((jax.nn.gelu(acc) * gate_all[:, sl]).astype(BF16))
    f = jnp.dot(jnp.concatenate(u, axis=1), wd_ref[...], preferred_element_type=F32)

    y = xc_ref[...] + mod_ref[0, 5:6, :] * _rms(f, nw3_ref[...])
    if len(out_refs) == 1:
        out_refs[0][...] = y
    else:
        @pl.when(is_ctx)
        def _():
            out_refs[0][...] = y

        @pl.when(jnp.logical_not(is_ctx))
        def _():
            out_refs[1][...] = y


def _gate_rows(w_gate):
    idx, keep = [], []
    for p in range(N_HEADS * SUBLANES):
        h, j = divmod(p, SUBLANES)
        d, kind = j % 2, j // 2
        idx.append(d * 2 * N_HEADS + kind * N_HEADS + h if j < 4 else 0)
        keep.append(1.0 if j < 4 else 0.0)
    return w_gate[..., jnp.array(idx)] * jnp.array(keep, w_gate.dtype)


def kernel(x_prompt, x_sample, state_C, state_n, state_m, c, c_ctx, w_mod, b_mod, norm_w, w_in, b_in,
           mlstm_gate_b, pool_w, pool_scale, gmlp_norm_w, gmlp_ws, gmlp_b, w_br, w_out, w_up, conv_w,
           conv_b, w_down):
    n_ctx, ctx_len, d = x_prompt.shape
    n_lat, lat_len, _ = x_sample.shape
    depth = w_in.shape[0]
    n_ctx_tok = n_ctx * ctx_len
    n_lat_tok = n_lat * lat_len
    n_tok = n_ctx_tok + n_lat_tok
    assert d == D_MODEL and FFN_TILE % ctx_len == 0 and lat_len % FFN_TILE == 0
    assert n_ctx_tok % FFN_TILE == 0 and ctx_len % TOK_TILE == 0 and lat_len % TOK_TILE == 0
    n_gate = 2 * 2 * N_HEADS
    gate0 = 4 * D_MODEL

    x_parts = (x_prompt.reshape(n_ctx_tok, d), x_sample.reshape(n_lat_tok, d))

    def tok_specs(n_parts, tm, tile=lambda i: i):
        if n_parts == 1:
            return [pl.BlockSpec((tm, d), lambda i, *_: (tile(i), 0))]
        n_a = n_ctx_tok // tm
        return [pl.BlockSpec((tm, d), lambda i, *_: (jnp.minimum(tile(i), n_a - 1), 0)),
                pl.BlockSpec((tm, d), lambda i, *_: (jnp.maximum(tile(i) - n_a, 0), 0))]

    cvec = jnp.zeros((SUBLANES, d), F32).at[0].set(c_ctx).at[1:1 + n_lat].set(c)
    mod = _modulation(cvec, w_mod, b_mod)

    def mod_spec(tm):
        def row(i):
            t0 = i * tm
            return jnp.where(t0 < n_ctx_tok, 0, 1 + jnp.maximum(t0 - n_ctx_tok, 0) // lat_len)
        return row

    tn_in = 2048
    n_col_tiles = MAIN_WIDTH // tn_in
    k_scale = jnp.ones((MAIN_WIDTH,), F32).at[COL_K * d:(COL_K + 1) * d].set(HEAD_DIM ** -0.5)
    w_in16 = w_in.astype(BF16)
    w_main = (jnp.concatenate([w_in16[:, :, :gate0], w_in16[:, :, gate0 + n_gate:]], axis=2)
              * k_scale.astype(BF16))
    b_main = (jnp.concatenate([b_in[:, :gate0], b_in[:, gate0 + n_gate:]], axis=1)
              * k_scale).reshape(depth, 1, MAIN_WIDTH)
    row_tok = mod_spec(TOK_TILE)
    row_ffn = mod_spec(FFN_TILE)
    vec = lambda a: a.reshape(1, -1)
    states = None
    t_idx = np.arange(TOK_TILE)[:, None]
    u_idx = np.arange(TOK_TILE + 2 * POOL_HALO)[None, :] - POOL_HALO
    pool_bands = jnp.asarray(np.stack([(u_idx >= t_idx - w // 2) & (u_idx < t_idx + w // 2)
                                       for w in POOL_WINDOWS]), BF16)
    n_grow = N_HEADS * SUBLANES
    scan_lanes = 2048

    for l in range(depth):
        last_layer = l == depth - 1
        wg_rows = _gate_rows(w_in[l, :, gate0:gate0 + n_gate]).T.astype(BF16)
        bga_rows = _gate_rows(b_in[l, gate0:gate0 + n_gate]).reshape(-1, 1)
        bgb_rows = _gate_rows(mlstm_gate_b[l].reshape(n_gate)).reshape(-1, 1)
        mod_l = mod[l]
        nw = norm_w[l]
        tm_proj = 1024 if len(x_parts) == 2 else 2048
        assert n_ctx_tok % tm_proj == 0 and lat_len % tm_proj == 0
        row_in = mod_spec(tm_proj)

        z, gt = pl.pallas_call(
            functools.partial(_inproj_kernel, n_x=len(x_parts), n_first=n_ctx_tok // tm_proj),
            grid=(n_tok // tm_proj, n_col_tiles),
            in_specs=tok_specs(len(x_parts), tm_proj) + [
                pl.BlockSpec((1, N_MOD, d), lambda i, j: (row_in(i), 0, 0)),
                pl.BlockSpec((1, d), lambda i, j: (0, 0)),
                pl.BlockSpec((None, d, tn_in), lambda i, j, l=l: (l, 0, j)),
                pl.BlockSpec((None, 1, tn_in), lambda i, j, l=l: (l, 0, j)),
                pl.BlockSpec((n_grow, d), lambda i, j: (0, 0)),
                pl.BlockSpec((n_grow, 1), lambda i, j: (0, 0)),
                pl.BlockSpec((n_grow, 1), lambda i, j: (0, 0)),
            ],
            out_specs=[
                pl.BlockSpec((tm_proj, tn_in), lambda i, j: (i, j)),
                pl.BlockSpec((n_grow, tm_proj), lambda i, j: (0, i)),
            ],
            out_shape=[
                jax.ShapeDtypeStruct((n_tok, MAIN_WIDTH), BF16),
                jax.ShapeDtypeStruct((n_grow, n_tok), F32),
            ],
            scratch_shapes=[pltpu.VMEM((tm_proj, d), BF16)],
            compiler_params=_params(("arbitrary", "arbitrary")),
            name="inproj",
        )(*x_parts, mod_l, vec(nw[0]), w_main, b_main, wg_rows, bga_rows, bgb_rows)

        scan_spec = pl.BlockSpec((n_grow, scan_lanes), lambda i: (0, i))
        scans = pl.pallas_call(
            _gates_kernel,
            grid=(n_tok // scan_lanes,),
            in_specs=[scan_spec],
            out_specs=[scan_spec] * 3,
            out_shape=[jax.ShapeDtypeStruct((n_grow, n_tok), F32)] * 3,
            compiler_params=_params(("arbitrary",)),
            name="gates",
        )(gt)

        ya_ctx, states = _mlstm(z, scans, 0, n_ctx, ctx_len, prev=states, heads=N_HEADS)
        ya_lat, _ = _mlstm(z, scans, n_ctx_tok, n_lat, lat_len,
                           init=(state_C, state_n, state_m), layer=l)

        tm = TOK_TILE
        halo_per_tile = tm // POOL_HALO
        n_halo = n_tok // POOL_HALO
        zcol = lambda col: pl.BlockSpec((tm, d), lambda i: (i, col))
        full = lambda shape: pl.BlockSpec(shape, lambda i: (0,) * len(shape))
        x = pl.pallas_call(
            functools.partial(_mix_merge_kernel, n_x=len(x_parts), n_first=n_ctx_tok // tm,
                              n_ctx_tok=n_ctx_tok, ctx_len=ctx_len, lat_len=lat_len),
            grid=(n_tok // tm,),
            in_specs=tok_specs(len(x_parts), tm) + tok_specs(2, tm) + [
                pl.BlockSpec((POOL_HALO, d),
                             lambda i: (jnp.maximum(i * halo_per_tile - 1, 0), COL_P)),
                zcol(COL_P),
                pl.BlockSpec((POOL_HALO, d),
                             lambda i: (jnp.minimum((i + 1) * halo_per_tile, n_halo - 1), COL_P)),
                zcol(COL_U), zcol(COL_VG), zcol(COL_M), zcol(COL_M + 1), zcol(COL_M + 2),
                pl.BlockSpec((1, N_MOD, d), lambda i: (row_tok(i), 0, 0)),
                full((1, d)),
                full(pool_bands.shape),
                full((N_GROUPS, GROUP_DIM, GROUP_DIM)), full((1, d)), full((1, d)),
                full((N_GROUPS, CHUNK, CHUNK)), full((CHUNK, N_GROUPS)),
                pl.BlockSpec((None, 3, d, d), lambda i, l=l: (l, 0, 0, 0)),
                pl.BlockSpec((None, d, d), lambda i, l=l: (l, 0, 0)),
            ],
            out_specs=pl.BlockSpec((tm, d), lambda i: (i, 0)),
            out_shape=jax.ShapeDtypeStruct((n_tok, d), F32),
            scratch_shapes=[pltpu.VMEM((3, d, d), BF16), pltpu.VMEM((d, d), BF16)],
            compiler_params=_params(("arbitrary",)),
            name="mix_merge",
        )(*x_parts, ya_ctx, ya_lat, z, z, z, z, z, z, z, z, mod_l, vec(nw[1]),
          pool_bands, pool_w[l].astype(BF16), vec(pool_scale[l]), vec(gmlp_norm_w[l]),
          gmlp_ws[l].astype(BF16), gmlp_b[l].T, w_br, w_out)
        x_parts = (x,)

        tf = FFN_TILE
        rows_per_tile = tf // GRID_W
        n_rows = n_tok // GRID_W
        n_ctx_tiles = n_ctx_tok // tf
        if last_layer:
            out_specs = [pl.BlockSpec((tf, d), lambda i: (jnp.minimum(i, n_ctx_tiles - 1), 0)),
                         pl.BlockSpec((tf, d), lambda i: (jnp.maximum(i - n_ctx_tiles, 0), 0))]
            out_shape = [jax.ShapeDtypeStruct((n_ctx_tok, d), F32),
                         jax.ShapeDtypeStruct((n_lat_tok, d), F32)]
        else:
            out_specs = [pl.BlockSpec((tf, d), lambda i: (i, 0))]
            out_shape = [jax.ShapeDtypeStruct((n_tok, d), F32)]
        x_parts = tuple(pl.pallas_call(
            functools.partial(_ffn_kernel, n_ctx_tok=n_ctx_tok, ctx_len=ctx_len, lat_len=lat_len),
            grid=(n_tok // tf,),
            in_specs=[
                pl.BlockSpec((GRID_W, d), lambda i: (jnp.maximum(i * rows_per_tile - 1, 0), 0)),
                pl.BlockSpec((tf, d), lambda i: (i, 0)),
                pl.BlockSpec((GRID_W, d),
                             lambda i: (jnp.minimum((i + 1) * rows_per_tile, n_rows - 1), 0)),
                pl.BlockSpec((1, N_MOD, d), lambda i: (row_ffn(i), 0, 0)),
                full((1, d)), full((1, d)), full((d, 2 * D_FF)),
                full((9, D_FF)), full((1, D_FF)),
                pl.BlockSpec((None, D_FF, d), lambda i, l=l: (l, 0, 0)),
            ],
            out_specs=out_specs,
            out_shape=out_shape,
            scratch_shapes=[pltpu.VMEM((D_FF, d), BF16)],
            compiler_params=_params(("arbitrary",)),
            name="ffn",
        )(x, x, x, mod_l, vec(nw[2]), vec(nw[3]), w_up[l].astype(BF16),
          conv_w[l].reshape(9, D_FF), vec(conv_b[l]), w_down))

    new_c, new_n, new_m = states
    return (x_parts[0].reshape(n_ctx, ctx_len, d), x_parts[1].reshape(n_lat, lat_len, d), new_c,
            new_n.reshape(n_ctx, depth, 2, N_HEADS, HEAD_DIM), new_m.reshape(n_ctx, depth, 2, N_HEADS))
```

```python
import functools

import jax
import jax.numpy as jnp
import numpy as np
from jax import lax
from jax.experimental import pallas as pl
from jax.experimental.pallas import tpu as pltpu

F32 = jnp.float32
BF16 = jnp.bfloat16

D_MODEL = 1024
N_HEADS = 4
HEAD_DIM = D_MODEL // N_HEADS
CHUNK = 128
POOL_WINDOWS = (2, 4, 8, 16)
N_GROUPS = 4
GROUP_DIM = D_MODEL // N_GROUPS
D_FF = 2816
GRID_W = 64
N_MOD = 6
RMS_EPS = 1e-6
LANES = 128
SUBLANES = 8
VMEM_LIMIT = 56 * 1024 * 1024

COL_Q, COL_K, COL_V, COL_O, COL_P, COL_U, COL_VG, COL_M = 0, 1, 2, 3, 4, 5, 6, 7
MAIN_WIDTH = 10 * D_MODEL

NT_DIMS = (((1,), (1,)), ((), ()))


def _rms(x, w):
    return x * lax.rsqrt(jnp.mean(x * x, axis=-1, keepdims=True) + RMS_EPS) * w


def _log_sigmoid(x):
    return jnp.minimum(x, 0.0) - jnp.log1p(jnp.exp(-jnp.abs(x)))


def _params(sem):
    return pltpu.CompilerParams(dimension_semantics=sem, vmem_limit_bytes=VMEM_LIMIT)


def _mod_kernel(c_ref, w_ref, b_ref, o_ref):
    c = c_ref[...]
    s = (c * jax.nn.sigmoid(c)).astype(BF16)
    o_ref[...] = jnp.dot(s, w_ref[...].astype(BF16), preferred_element_type=F32) + b_ref[...]


def _modulation(cvec, w_mod, b_mod):
    depth = w_mod.shape[0]
    n = N_MOD * D_MODEL
    tn = 1536
    out = pl.pallas_call(
        _mod_kernel,
        grid=(depth, n // tn),
        in_specs=[
            pl.BlockSpec((SUBLANES, D_MODEL), lambda l, j: (0, 0)),
            pl.BlockSpec((None, D_MODEL, tn), lambda l, j: (l, 0, j)),
            pl.BlockSpec((None, 1, tn), lambda l, j: (l, 0, j)),
        ],
        out_specs=pl.BlockSpec((None, SUBLANES, tn), lambda l, j: (l, 0, j)),
        out_shape=jax.ShapeDtypeStruct((depth, SUBLANES, n), F32),
        compiler_params=_params(("arbitrary", "arbitrary")),
        name="modulation",
    )(cvec, w_mod, b_mod.reshape(depth, 1, n))
    return out.reshape(depth, SUBLANES, N_MOD, D_MODEL)


def _modnorm(x, mod_ref, nw_ref, k_shift, k_scale):
    h = _rms(x, nw_ref[...])
    return h * (1.0 + mod_ref[0, k_scale:k_scale + 1, :]) + mod_ref[0, k_shift:k_shift + 1, :]


def _pick(refs, first):
    if len(refs) == 1:
        return refs[0][...]
    return jnp.where(first, refs[0][...], refs[1][...])


def _inproj_kernel(*refs, n_x, n_first):
    x_refs = refs[:n_x]
    mod_ref, nw_ref, w_ref, b_ref, wgt_ref, bgta_ref, bgtb_ref, z_ref, gt_ref, h_scr = refs[n_x:]

    @pl.when(pl.program_id(1) == 0)
    def _():
        x = _pick(x_refs, pl.program_id(0) < n_first)
        hb = _modnorm(x, mod_ref, nw_ref, 0, 1).astype(BF16)
        h_scr[...] = hb
        gt_ref[...] = (lax.dot_general(wgt_ref[...], hb, NT_DIMS, preferred_element_type=F32)
                       + bgta_ref[...] + bgtb_ref[...])

    z = jnp.dot(h_scr[...], w_ref[...], preferred_element_type=F32) + b_ref[...]
    z_ref[...] = z.astype(z_ref.dtype)


def _gates_kernel(g_ref, a_ref, b_ref, cm_ref):
    g = g_ref[...]
    fwd = lax.broadcasted_iota(jnp.int32, g.shape, 0) % SUBLANES == 0
    lane = lax.broadcasted_iota(jnp.int32, g.shape, 1) % CHUNK
    lf = _log_sigmoid(pltpu.roll(g, g.shape[0] - 2, 0))
    pre, suf = _chunk_scans(lf, jnp.add, 0.0, lane)
    b = jnp.where(fwd, pre, suf)
    a = g - b
    pre, suf = _chunk_scans(a, jnp.maximum, -jnp.inf, lane)
    a_ref[...] = a
    b_ref[...] = b
    cm_ref[...] = jnp.where(fwd, pre, suf)


def _chunk_scans(x, op, fill, lane):
    width = x.shape[1]
    fwd = bwd = x
    k = 1
    while k < CHUNK:
        fwd = op(fwd, jnp.where(lane >= k, pltpu.roll(fwd, k, 1), fill))
        bwd = op(bwd, jnp.where(lane < CHUNK - k, pltpu.roll(bwd, width - k, 1), fill))
        k *= 2
    return fwd, bwd


def _mlstm_head(*refs, seq_len, has_init, emit_state, n_prev):
    q_ref, k_ref, v_ref, o_ref, a_ref, b_ref, cm_ref = refs[:7]
    pos = 7
    if has_init:
        c0_ref, n0_ref, m0_ref = refs[pos:pos + 3]
        pos += 3
    if n_prev:
        cprev_ref, nprev_ref, mprev_ref = refs[pos:pos + 3]
        pos += 3
    ya_ref = refs[pos]
    pos += 1
    if emit_state:
        cout_ref, nout_ref, mout_ref = refs[pos:pos + 3]
        pos += 3
    rows_scr, wk16_scr, dec_scr, vt_scr, vtw_scr, cb_scr, n1_scr, c_scr = refs[pos:pos + 8]

    n_chunks = seq_len // CHUNK
    unroll = min(4, n_chunks)
    row81 =lax.broadcasted_iota(jnp.int32, (SUBLANES, 1), 0)
    is_fwd1 = row81 == 0
    row_n = lax.broadcasted_iota(jnp.int32, (SUBLANES, HEAD_DIM), 0)

    chunk = lambda ref, c: ref[:, c * CHUNK:(c + 1) * CHUNK]
    a_c = [chunk(a_ref, c) for c in range(n_chunks)]
    b_c = [chunk(b_ref, c) for c in range(n_chunks)]
    cm_c = [chunk(cm_ref, c) for c in range(n_chunks)]
    row8 = lax.broadcasted_iota(jnp.int32, (SUBLANES, CHUNK), 0)
    lane8 = lax.broadcasted_iota(jnp.int32, (SUBLANES, CHUNK), 1)
    last = lane8 == jnp.where(row8 == 0, CHUNK - 1, 0)
    amax_c = [jnp.max(a, axis=1, keepdims=True) for a in a_c]
    bend_c = [jnp.sum(jnp.where(last, b, 0.0), axis=1, keepdims=True) for b in b_c]

    if has_init:
        m_state = jnp.where(is_fwd1, m0_ref[0], jnp.where(row81 == 1, m0_ref[1], 0.0))
        n_init = jnp.where(row_n == 0, n0_ref[0], n0_ref[1])
        c_scr[...] = c0_ref[...]
    else:
        m_state = jnp.zeros((SUBLANES, 1), F32)
        n_init = jnp.zeros((SUBLANES, HEAD_DIM), F32)
        c_scr[...] = jnp.zeros(c_scr.shape, F32)

    m_before, top = [], []
    for j in range(n_chunks):
        jr = n_chunks - 1 - j
        amax = jnp.where(is_fwd1, amax_c[j], amax_c[jr])
        bend = jnp.where(is_fwd1, bend_c[j], bend_c[jr])
        m_before.append(m_state)
        top.append(jnp.maximum(m_state, amax))
        m_state = bend + top[-1]

    for c in range(n_chunks):
        cr = n_chunks - 1 - c
        m_c = jnp.where(is_fwd1, m_before[c], m_before[cr])
        top_c = jnp.where(is_fwd1, top[c], top[cr])
        big_m = jnp.maximum(m_c, cm_c[c])
        w_k = jnp.exp(a_c[c] - top_c)
        rows_scr[c, 0] = a_c[c]
        rows_scr[c, 1] = big_m
        rows_scr[c, 2] = jnp.exp(m_c - big_m)
        rows_scr[c, 3] = jnp.exp(-b_c[c] - big_m)
        rows_scr[c, 4] = w_k
        wk16_scr[c] = jnp.concatenate([w_k, jnp.zeros_like(w_k)], axis=0).astype(BF16)
        dec_scr[c] = jnp.broadcast_to(jnp.exp(m_c - top_c), (SUBLANES, HEAD_DIM))

    def load_k(r0):
        return k_ref[pl.ds(r0, CHUNK), :]

    def pass_a(j, nst):
        c = n_chunks - 1 - j
        r0 = pl.multiple_of(c * CHUNK, CHUNK)
        ks = load_k(r0)
        vt = v_ref[pl.ds(r0, CHUNK), :].astype(F32).T
        w_k = rows_scr[c, 4]
        vt_scr[c] = vt.astype(BF16)
        vtw_scr[c] = (vt * w_k[0:1, :]).astype(BF16)
        c_in = c_scr[1]
        cb_scr[c] = c_in.astype(BF16)
        n1_scr[c] = nst
        dec = dec_scr[c]
        c_scr[1] = dec[1:2, :] * c_in + jnp.dot((vt * w_k[1:2, :]).astype(BF16), ks,
                                                preferred_element_type=F32)
        n_upd = jnp.dot(wk16_scr[c], ks, preferred_element_type=F32)[:SUBLANES]
        return dec * nst + n_upd

    n_rev = lax.fori_loop(0, n_chunks, pass_a, n_init, unroll=min(2 * unroll, n_chunks))

    row_id = lax.broadcasted_iota(jnp.int32, (CHUNK, CHUNK), 0)
    col_id = lax.broadcasted_iota(jnp.int32, (CHUNK, CHUNK), 1)
    tri = (row_id <= col_id, row_id >= col_id)

    def pass_b(c, nst):
        r0 = pl.multiple_of(c * CHUNK, CHUNK)
        q = q_ref[pl.ds(r0, CHUNK), :]
        ks = load_k(r0)
        vt = vt_scr[c]
        a_r, bigm_r, winter_r, floor_r = (rows_scr[c, i] for i in range(4))
        st_all = lax.dot_general(ks, q, NT_DIMS, preferred_element_type=F32)
        n_rows = jnp.where(row_n == 0, nst, n1_scr[c])
        n_rows = jnp.concatenate([n_rows, jnp.zeros_like(n_rows)], axis=0).astype(BF16)
        qn = lax.dot_general(n_rows, q, NT_DIMS, preferred_element_type=F32)
        c0 = c_scr[0]
        sts, invs = [], []
        for d in range(2):
            row = slice(d, d + 1)
            a_st = jnp.broadcast_to(a_r[row, :], (CHUNK, CHUNK)).T
            st = st_all * jnp.exp(jnp.where(tri[d], a_st - bigm_r[row, :], -jnp.inf))
            den = jnp.sum(st, axis=0, keepdims=True) + winter_r[row, :] * qn[row, :]
            invs.append(1.0 / jnp.maximum(jnp.abs(den), floor_r[row, :]))
            sts.append(st.astype(BF16))
        intra = jnp.dot(vt, jnp.concatenate(sts, axis=1), preferred_element_type=F32)
        if n_chunks == unroll:
            h = (intra[:, :CHUNK] * invs[0] + intra[:, CHUNK:] * invs[1]).T
            for d in range(2):
                row = slice(d, d + 1)
                cb = c0.astype(BF16) if d == 0 else cb_scr[c]
                inter = lax.dot_general(q, cb, NT_DIMS, preferred_element_type=F32)
                w_col = jnp.broadcast_to(winter_r[row, :] * invs[d], (CHUNK, CHUNK)).T
                h = h + inter * jnp.concatenate([w_col] * (HEAD_DIM // CHUNK), axis=1)
        else:
            ht = None
            for d in range(2):
                row = slice(d, d + 1)
                cb = c0.astype(BF16) if d == 0 else cb_scr[c]
                num_t = (intra[:, d * CHUNK:(d + 1) * CHUNK]
                         + winter_r[row, :] * lax.dot_general(cb, q, NT_DIMS,
                                                              preferred_element_type=F32))
                ht = num_t * invs[d] if ht is None else ht + num_t * invs[d]
            h = ht.T
        gate = jax.nn.sigmoid(o_ref[pl.ds(r0, CHUNK), :].astype(F32))
        ya_ref[pl.ds(r0, CHUNK), :] = (gate * h).astype(ya_ref.dtype)

        dec = dec_scr[c]
        c_scr[0] = dec[0:1, :] * c0 + jnp.dot(vtw_scr[c], ks, preferred_element_type=F32)
        n_upd = jnp.dot(wk16_scr[c], ks, preferred_element_type=F32)[:SUBLANES]
        return dec * nst + n_upd

    n_fwd = lax.fori_loop(0, n_chunks, pass_b, n_init, unroll=unroll)

    if emit_state:
        if n_prev:
            cout_ref[:n_prev] = cprev_ref[...]
            nout_ref[:n_prev] = nprev_ref[...]
            mout_ref[:n_prev] = mprev_ref[...]
        cout_ref[n_prev] = c_scr[...]
        nout_ref[n_prev, 0] = n_fwd[0:1, :]
        nout_ref[n_prev, 1] = n_rev[1:2, :]
        mout_ref[n_prev, 0] = m_state[0:1, :]
        mout_ref[n_prev, 1] = m_state[1:2, :]


def _mlstm_kernel(*refs, heads, seq_len, has_init, emit_state, n_prev):
    if heads == 1:
        return _mlstm_head(*refs, seq_len=seq_len, has_init=has_init, emit_state=emit_state,
                           n_prev=n_prev)
    assert not has_init
    n_state = 3 * (bool(n_prev) + emit_state)
    n_scratch = len(refs) - 8 - n_state
    for hh in range(heads):
        cols = slice(hh * HEAD_DIM, (hh + 1) * HEAD_DIM)
        views = [r.at[:, cols] for r in refs[:4]]
        views += [r.at[hh * SUBLANES:(hh + 1) * SUBLANES, :] for r in refs[4:7]]
        pos = 7
        if n_prev:
            views += [r.at[:, :, hh] for r in refs[pos:pos + 3]]
            pos += 3
        views.append(refs[pos].at[:, cols])
        pos += 1
        if emit_state:
            views += [r.at[:, :, hh] for r in refs[pos:pos + 3]]
            pos += 3
        assert len(refs) - pos == n_scratch
        _mlstm_head(*views, *refs[pos:], seq_len=seq_len, has_init=False, emit_state=emit_state,
                    n_prev=n_prev)


def _mlstm(z, scans, tok0, n_batch, seq_len, init=None, layer=0, prev=None, heads=1):
    blk0 = tok0 // seq_len
    n_chunks = seq_len // CHUNK
    has_init = init is not None
    emit_state = not has_init
    n_prev = 0 if prev is None else prev[0].shape[1]

    head_blocks = N_HEADS // heads
    hdim = None if heads == 1 else heads

    def zspec(col):
        return pl.BlockSpec((seq_len, heads * HEAD_DIM),
                            lambda b, h: (blk0 + b, col * head_blocks + h))

    scan_spec = pl.BlockSpec((heads * SUBLANES, seq_len), lambda b, h: (h, blk0 + b))
    in_specs = [zspec(COL_Q), zspec(COL_K), zspec(COL_V), zspec(COL_O),
                scan_spec, scan_spec, scan_spec]
    args = [z, z, z, z, *scans]

    def state_specs(n_layers):
        return [
            pl.BlockSpec((None, n_layers, 2, hdim, HEAD_DIM, HEAD_DIM),
                         lambda b, h: (b, 0, 0, h, 0, 0)),
            pl.BlockSpec((None, n_layers, 2, hdim, 1, HEAD_DIM), lambda b, h: (b, 0, 0, h, 0, 0)),
            pl.BlockSpec((None, n_layers, 2, hdim, 1, 1), lambda b, h: (b, 0, 0, h, 0, 0)),
        ]

    if has_init:
        sc, sn, sm = init
        nb, depth = sc.shape[:2]
        in_specs += [
            pl.BlockSpec((None, None, 2, None, HEAD_DIM, HEAD_DIM),
                         lambda b, h: (b, layer, 0, h, 0, 0)),
            pl.BlockSpec((None, None, 2, None, 1, HEAD_DIM), lambda b, h: (b, layer, 0, h, 0, 0)),
            pl.BlockSpec((None, None, 2, None, 1, 1), lambda b, h: (b, layer, 0, h, 0, 0)),
        ]
        args += [sc, sn.reshape(nb, depth, 2, N_HEADS, 1, HEAD_DIM),
                 sm.reshape(nb, depth, 2, N_HEADS, 1, 1)]
    if n_prev:
        in_specs += state_specs(n_prev)
        args += list(prev)
    out_specs = [pl.BlockSpec((seq_len, heads * HEAD_DIM), lambda b, h: (b, h))]
    out_shape = [jax.ShapeDtypeStruct((n_batch * seq_len, D_MODEL), BF16)]
    if emit_state:
        n_out = n_prev + 1
        out_specs += state_specs(n_out)
        out_shape += [
            jax.ShapeDtypeStruct((n_batch, n_out, 2, N_HEADS, HEAD_DIM, HEAD_DIM), F32),
            jax.ShapeDtypeStruct((n_batch, n_out, 2, N_HEADS, 1, HEAD_DIM), F32),
            jax.ShapeDtypeStruct((n_batch, n_out, 2, N_HEADS, 1, 1), F32),
        ]
    outs = pl.pallas_call(
        functools.partial(_mlstm_kernel, heads=heads, seq_len=seq_len, has_init=has_init,
                          emit_state=emit_state, n_prev=n_prev),
        grid=(n_batch, head_blocks),
        in_specs=in_specs,
        out_specs=out_specs,
        out_shape=out_shape,
        scratch_shapes=[
            pltpu.VMEM((n_chunks, 5, SUBLANES, CHUNK), F32),
            pltpu.VMEM((n_chunks, 2 * SUBLANES, CHUNK), BF16),
            pltpu.VMEM((n_chunks, SUBLANES, HEAD_DIM), F32),
            pltpu.VMEM((n_chunks, HEAD_DIM, CHUNK), BF16),
            pltpu.VMEM((n_chunks, HEAD_DIM, CHUNK), BF16),
            pltpu.VMEM((n_chunks, HEAD_DIM, HEAD_DIM), BF16),
            pltpu.VMEM((n_chunks, SUBLANES, HEAD_DIM), F32),
            pltpu.VMEM((2, HEAD_DIM, HEAD_DIM), F32),
        ],
        compiler_params=_params(("arbitrary", "arbitrary")),
        name="mlstm_ctx" if emit_state else "mlstm_lat",
    )(*args)
    if emit_state:
        return outs[0], tuple(outs[1:])
    return outs[0], None


TOK_TILE = 256
POOL_HALO = 64


def _mix_merge_kernel(*refs, n_x, n_first, n_ctx_tok, ctx_len, lat_len):
    x_refs = refs[:n_x]
    (ya_ctx_ref, ya_lat_ref, pp_ref, pc_ref, pn_ref, u_ref, vg_ref, m0_ref, m1_ref, m2_ref,
     mod_ref, nw_ref, band_ref, pw_ref, ps_ref, gnw_ref, ws_ref, gb_ref, wbr32_ref, wout32_ref,
     o_ref, wbr_ref, wout_ref) = refs[n_x:]

    @pl.when(pl.program_id(0) == 0)
    def _():
        for n in range(wbr_ref.shape[0]):
            wbr_ref[n] = wbr32_ref[n].astype(BF16)
        wout_ref[...] = wout32_ref[...].astype(BF16)

    tok0 = pl.program_id(0) * TOK_TILE
    first = pl.program_id(0) < n_first
    is_ctx = tok0 < n_ctx_tok
    seq_len = jnp.where(is_ctx, ctx_len, lat_len)
    base = jnp.where(is_ctx, 0, n_ctx_tok)
    seq_start = base + ((tok0 - base) // seq_len) * seq_len
    seq_end = seq_start + seq_len

    t_col = tok0 + lax.broadcasted_iota(jnp.int32, (TOK_TILE, 1), 0)
    p_prev = jnp.where(tok0 > seq_start, pp_ref[...], jnp.zeros_like(pp_ref))
    p_next = jnp.where(tok0 + TOK_TILE < seq_end, pn_ref[...], jnp.zeros_like(pn_ref))
    p_ext = jnp.concatenate([p_prev, pc_ref[...], p_next], axis=0)
    yb = []
    for g, win in enumerate(POOL_WINDOWS):
        half = win // 2
        sl = slice(g * GROUP_DIM, (g + 1) * GROUP_DIM)
        acc = jnp.dot(band_ref[g], p_ext[:, sl], preferred_element_type=F32)
        count = (jnp.minimum(t_col + half, seq_end) - jnp.maximum(t_col - half, seq_start)).astype(F32)
        y = (acc / count - pc_ref[:, sl].astype(F32)).astype(BF16)
        yb.append(jnp.dot(y, pw_ref[g], preferred_element_type=F32) * ps_ref[:, sl])
    yb = jnp.concatenate(yb, axis=1).astype(BF16)

    u = jax.nn.gelu(u_ref[...].astype(F32))
    vg = _rms(jax.nn.gelu(vg_ref[...].astype(F32)), gnw_ref[...]).astype(BF16)
    yc = []
    for ch in range(TOK_TILE // CHUNK):
        rows = slice(ch * CHUNK, (ch + 1) * CHUNK)
        parts = []
        for g in range(N_GROUPS):
            sl = slice(g * GROUP_DIM, (g + 1) * GROUP_DIM)
            s = jnp.dot(ws_ref[g], vg[rows, sl], preferred_element_type=F32) + gb_ref[:, g:g + 1]
            parts.append(u[rows, sl] * s)
        yc.append(jnp.concatenate(parts, axis=1))
    yc = jnp.concatenate(yc, axis=0).astype(BF16)

    ys = (_pick((ya_ctx_ref, ya_lat_ref), first), yb, yc)
    mixed = None
    for n, (y, m_ref) in enumerate(zip(ys, (m0_ref, m1_ref, m2_ref))):
        br = jnp.dot(y, wbr_ref[n], preferred_element_type=F32)
        term = jax.nn.sigmoid(m_ref[...].astype(F32)) * br
        mixed = term if mixed is None else mixed + term
    o = jnp.dot(mixed.astype(BF16), wout_ref[...], preferred_element_type=F32)
    o_ref[...] = _pick(x_refs, first) + mod_ref[0, 2:3, :] * _rms(o, nw_ref[...])


FFN_TILE = 512
FFN_SLAB = 256


def _ffn_kernel(xp_ref, xc_ref, xn_ref, mod_ref, nw2_ref, nw3_ref, wup_ref, cw_ref, cb_ref, wd32_ref,
                *rest, n_ctx_tok, ctx_len, lat_len):
    *out_refs, wd_ref = rest
    i = pl.program_id(0)

    @pl.when(i == 0)
    def _():
        wd_ref[...] = wd32_ref[...].astype(BF16)

    tok0 = i * FFN_TILE
    is_ctx = tok0 < n_ctx_tok
    tiles_per_lat = lat_len // FFN_TILE
    lat_tile = jnp.maximum(tok0 - n_ctx_tok, 0) // FFN_TILE % tiles_per_lat
    has_up = jnp.logical_and(jnp.logical_not(is_ctx), lat_tile != 0)
    has_dn = jnp.logical_and(jnp.logical_not(is_ctx), lat_tile != tiles_per_lat - 1)
    ext_rows = FFN_TILE + 2 * GRID_W
    r = lax.broadcasted_iota(jnp.int32, (FFN_TILE, FFN_SLAB), 0)
    col_ctx = r % ctx_len
    col_lat = r % GRID_W
    keep_left = jnp.where(jnp.where(is_ctx, col_ctx, col_lat) != 0, 1.0, 0.0)
    keep_right = jnp.where(jnp.where(is_ctx, col_ctx - (ctx_len - 1), col_lat - (GRID_W - 1)) != 0,
                           1.0, 0.0)
    r1 = lax.broadcasted_iota(jnp.int32, (ext_rows, 1), 0)
    in_grid = jnp.logical_and(jnp.logical_or(r1 >= GRID_W, has_up),
                              jnp.logical_or(r1 < GRID_W + FFN_TILE, has_dn))
    tap_on = jnp.where(lax.broadcasted_iota(jnp.int32, (9, 1), 0) // 3 == 1, 1.0,
                       jnp.where(is_ctx, 0.0, 1.0))
    cw = cw_ref[...] * tap_on

    x_ext = jnp.concatenate([xp_ref[...], xc_ref[...], xn_ref[...]], axis=0)
    h_ext = jnp.where(in_grid, _modnorm(x_ext, mod_ref, nw2_ref, 3, 4), 0.0).astype(BF16)
    h_cur = h_ext[GRID_W:GRID_W + FFN_TILE]

    ext_all = jnp.dot(h_ext, wup_ref[:, :D_FF], preferred_element_type=F32)
    gate_all = jnp.dot(h_cur, wup_ref[:, D_FF:], preferred_element_type=F32)

    u = []
    for c in range(D_FF // FFN_SLAB):
        sl = slice(c * FFN_SLAB, (c + 1) * FFN_SLAB)
        ext = ext_all[:, sl]
        q = []
        for dx in range(3):
            qs = None
            for dy in range(3):
                term = cw[dy * 3 + dx:dy * 3 + dx + 1, sl] * ext[dy * GRID_W:dy * GRID_W + FFN_TILE, :]
                qs = term if qs is None else qs + term
            q.append(qs)
        acc = (q[1] + cb_ref[:, sl] + pltpu.roll(q[0], 1, 0) * keep_left
               + pltpu.roll(q[2], FFN_TILE - 1, 0) * keep_right)
        u.append((jax.nn.gelu(acc) * gate_all[:, sl]).astype(BF16))
    f = jnp.dot(jnp.concatenate(u, axis=1), wd_ref[...], preferred_element_type=F32)

    y = xc_ref[...] + mod_ref[0, 5:6, :] * _rms(f, nw3_ref[...])
    if len(out_refs) == 1:
        out_refs[0][...] = y
    else:
        @pl.when(is_ctx)
        def _():
            out_refs[0][...] = y

        @pl.when(jnp.logical_not(is_ctx))
        def _():
            out_refs[1][...] = y


def _gate_rows(w_gate):
    idx, keep = [], []
    for p in range(N_HEADS * SUBLANES):
        h, j = divmod(p, SUBLANES)
        d, kind = j % 2, j // 2
        idx.append(d * 2 * N_HEADS + kind * N_HEADS + h if j < 4 else 0)
        keep.append(1.0 if j < 4 else 0.0)
    return w_gate[..., jnp.array(idx)] * jnp.array(keep, w_gate.dtype)


def kernel(x_prompt, x_sample, state_C, state_n, state_m, c, c_ctx, w_mod, b_mod, norm_w, w_in, b_in,
           mlstm_gate_b, pool_w, pool_scale, gmlp_norm_w, gmlp_ws, gmlp_b, w_br, w_out, w_up, conv_w,
           conv_b, w_down):
    n_ctx, ctx_len, d = x_prompt.shape
    n_lat, lat_len, _ = x_sample.shape
    depth = w_in.shape[0]
    n_ctx_tok = n_ctx * ctx_len
    n_lat_tok = n_lat * lat_len
    n_tok = n_ctx_tok + n_lat_tok
    assert d == D_MODEL and FFN_TILE % ctx_len == 0 and lat_len % FFN_TILE == 0
    assert n_ctx_tok % FFN_TILE == 0 and ctx_len % TOK_TILE == 0 and lat_len % TOK_TILE == 0
    n_gate = 2 * 2 * N_HEADS
    gate0 = 4 * D_MODEL

    x_parts = (x_prompt.reshape(n_ctx_tok, d), x_sample.reshape(n_lat_tok, d))

    def tok_specs(n_parts, tm, tile=lambda i: i):
        if n_parts == 1:
            return [pl.BlockSpec((tm, d), lambda i, *_: (tile(i), 0))]
        n_a = n_ctx_tok // tm
        return [pl.BlockSpec((tm, d), lambda i, *_: (jnp.minimum(tile(i), n_a - 1), 0)),
                pl.BlockSpec((tm, d), lambda i, *_: (jnp.maximum(tile(i) - n_a, 0), 0))]

    cvec = jnp.zeros((SUBLANES, d), F32).at[0].set(c_ctx).at[1:1 + n_lat].set(c)
    mod = _modulation(cvec, w_mod, b_mod)

    def mod_spec(tm):
        def row(i):
            t0 = i * tm
            return jnp.where(t0 < n_ctx_tok, 0, 1 + jnp.maximum(t0 - n_ctx_tok, 0) // lat_len)
        return row

    tn_in = 2048
    n_col_tiles = MAIN_WIDTH // tn_in
    k_scale = jnp.ones((MAIN_WIDTH,), F32).at[COL_K * d:(COL_K + 1) * d].set(HEAD_DIM ** -0.5)
    w_in16 = w_in.astype(BF16)
    w_main = (jnp.concatenate([w_in16[:, :, :gate0], w_in16[:, :, gate0 + n_gate:]], axis=2)
              * k_scale.astype(BF16))
    b_main = (jnp.concatenate([b_in[:, :gate0], b_in[:, gate0 + n_gate:]], axis=1)
              * k_scale).reshape(depth, 1, MAIN_WIDTH)
    row_tok = mod_spec(TOK_TILE)
    row_ffn = mod_spec(FFN_TILE)
    vec = lambda a: a.reshape(1, -1)
    states = None
    t_idx = np.arange(TOK_TILE)[:, None]
    u_idx = np.arange(TOK_TILE + 2 * POOL_HALO)[None, :] - POOL_HALO
    pool_bands = jnp.asarray(np.stack([(u_idx >= t_idx - w // 2) & (u_idx < t_idx + w // 2)
                                       for w in POOL_WINDOWS]), BF16)
    n_grow = N_HEADS * SUBLANES
    scan_lanes = 2048

    for l in range(depth):
        last_layer = l == depth - 1
        wg_rows = _gate_rows(w_in[l, :, gate0:gate0 + n_gate]).T.astype(BF16)
        bga_rows = _gate_rows(b_in[l, gate0:gate0 + n_gate]).reshape(-1, 1)
        bgb_rows = _gate_rows(mlstm_gate_b[l].reshape(n_gate)).reshape(-1, 1)
        mod_l = mod[l]
        nw = norm_w[l]
        tm_proj = 1024 if len(x_parts) == 2 else 2048
        assert n_ctx_tok % tm_proj == 0 and lat_len % tm_proj == 0
        row_in = mod_spec(tm_proj)

        z, gt = pl.pallas_call(
            functools.partial(_inproj_kernel, n_x=len(x_parts), n_first=n_ctx_tok // tm_proj),
            grid=(n_tok // tm_proj, n_col_tiles),
            in_specs=tok_specs(len(x_parts), tm_proj) + [
                pl.BlockSpec((1, N_MOD, d), lambda i, j: (row_in(i), 0, 0)),
                pl.BlockSpec((1, d), lambda i, j: (0, 0)),
                pl.BlockSpec((None, d, tn_in), lambda i, j, l=l: (l, 0, j)),
                pl.BlockSpec((None, 1, tn_in), lambda i, j, l=l: (l, 0, j)),
                pl.BlockSpec((n_grow, d), lambda i, j: (0, 0)),
                pl.BlockSpec((n_grow, 1), lambda i, j: (0, 0)),
                pl.BlockSpec((n_grow, 1), lambda i, j: (0, 0)),
            ],
            out_specs=[
                pl.BlockSpec((tm_proj, tn_in), lambda i, j: (i, j)),
                pl.BlockSpec((n_grow, tm_proj), lambda i, j: (0, i)),
            ],
            out_shape=[
                jax.ShapeDtypeStruct((n_tok, MAIN_WIDTH), BF16),
                jax.ShapeDtypeStruct((n_grow, n_tok), F32),
            ],
            scratch_shapes=[pltpu.VMEM((tm_proj, d), BF16)],
            compiler_params=_params(("arbitrary", "arbitrary")),
            name="inproj",
        )(*x_parts, mod_l, vec(nw[0]), w_main, b_main, wg_rows, bga_rows, bgb_rows)

        scan_spec = pl.BlockSpec((n_grow, scan_lanes), lambda i: (0, i))
        scans = pl.pallas_call(
            _gates_kernel,
            grid=(n_tok // scan_lanes,),
            in_specs=[scan_spec],
            out_specs=[scan_spec] * 3,
            out_shape=[jax.ShapeDtypeStruct((n_grow, n_tok), F32)] * 3,
            compiler_params=_params(("arbitrary",)),
            name="gates",
        )(gt)

        ya_ctx, states = _mlstm(z, scans, 0, n_ctx, ctx_len, prev=states, heads=N_HEADS)
        ya_lat, _ = _mlstm(z, scans, n_ctx_tok, n_lat, lat_len,
                           init=(state_C, state_n, state_m), layer=l)

        tm = TOK_TILE
        halo_per_tile = tm // POOL_HALO
        n_halo = n_tok // POOL_HALO
        zcol = lambda col: pl.BlockSpec((tm, d), lambda i: (i, col))
        full = lambda shape: pl.BlockSpec(shape, lambda i: (0,) * len(shape))
        x = pl.pallas_call(
            functools.partial(_mix_merge_kernel, n_x=len(x_parts), n_first=n_ctx_tok // tm,
                              n_ctx_tok=n_ctx_tok, ctx_len=ctx_len, lat_len=lat_len),
            grid=(n_tok // tm,),
            in_specs=tok_specs(len(x_parts), tm) + tok_specs(2, tm) + [
                pl.BlockSpec((POOL_HALO, d),
                             lambda i: (jnp.maximum(i * halo_per_tile - 1, 0), COL_P)),
                zcol(COL_P),
                pl.BlockSpec((POOL_HALO, d),
                             lambda i: (jnp.minimum((i + 1) * halo_per_tile, n_halo - 1), COL_P)),
                zcol(COL_U), zcol(COL_VG), zcol(COL_M), zcol(COL_M + 1), zcol(COL_M + 2),
                pl.BlockSpec((1, N_MOD, d), lambda i: (row_tok(i), 0, 0)),
                full((1, d)),
                full(pool_bands.shape),
                full((N_GROUPS, GROUP_DIM, GROUP_DIM)), full((1, d)), full((1, d)),
                full((N_GROUPS, CHUNK, CHUNK)), full((CHUNK, N_GROUPS)),
                pl.BlockSpec((None, 3, d, d), lambda i, l=l: (l, 0, 0, 0)),
                pl.BlockSpec((None, d, d), lambda i, l=l: (l, 0, 0)),
            ],
            out_specs=pl.BlockSpec((tm, d), lambda i: (i, 0)),
            out_shape=jax.ShapeDtypeStruct((n_tok, d), F32),
            scratch_shapes=[pltpu.VMEM((3, d, d), BF16), pltpu.VMEM((d, d), BF16)],
            compiler_params=_params(("arbitrary",)),
            name="mix_merge",
        )(*x_parts, ya_ctx, ya_lat, z, z, z, z, z, z, z, z, mod_l, vec(nw[1]),
          pool_bands, pool_w[l].astype(BF16), vec(pool_scale[l]), vec(gmlp_norm_w[l]),
          gmlp_ws[l].astype(BF16), gmlp_b[l].T, w_br, w_out)
        x_parts = (x,)

        tf = FFN_TILE
        rows_per_tile = tf // GRID_W
        n_rows = n_tok // GRID_W
        n_ctx_tiles = n_ctx_tok // tf
        if last_layer:
            out_specs = [pl.BlockSpec((tf, d), lambda i: (jnp.minimum(i, n_ctx_tiles - 1), 0)),
                         pl.BlockSpec((tf, d), lambda i: (jnp.maximum(i - n_ctx_tiles, 0), 0))]
            out_shape = [jax.ShapeDtypeStruct((n_ctx_tok, d), F32),
                         jax.ShapeDtypeStruct((n_lat_tok, d), F32)]
        else:
            out_specs = [pl.BlockSpec((tf, d), lambda i: (i, 0))]
            out_shape = [jax.ShapeDtypeStruct((n_tok, d), F32)]
        x_parts = tuple(pl.pallas_call(
            functools.partial(_ffn_kernel, n_ctx_tok=n_ctx_tok, ctx_len=ctx_len, lat_len=lat_len),
            grid=(n_tok // tf,),
            in_specs=[
                pl.BlockSpec((GRID_W, d), lambda i: (jnp.maximum(i * rows_per_tile - 1, 0), 0)),
                pl.BlockSpec((tf, d), lambda i: (i, 0)),
                pl.BlockSpec((GRID_W, d),
                             lambda i: (jnp.minimum((i + 1) * rows_per_tile, n_rows - 1), 0)),
                pl.BlockSpec((1, N_MOD, d), lambda i: (row_ffn(i), 0, 0)),
                full((1, d)), full((1, d)), full((d, 2 * D_FF)),
                full((9, D_FF)), full((1, D_FF)),
                pl.BlockSpec((None, D_FF, d), lambda i, l=l: (l, 0, 0)),
            ],
            out_specs=out_specs,
            out_shape=out_shape,
            scratch_shapes=[pltpu.VMEM((D_FF, d), BF16)],
            compiler_params=_params(("arbitrary",)),
            name="ffn",
        )(x, x, x, mod_l, vec(nw[2]), vec(nw[3]), w_up[l].astype(BF16),
          conv_w[l].reshape(9, D_FF), vec(conv_b[l]), w_down))

    new_c, new_n, new_m = states
    return (x_parts[0].reshape(n_ctx, ctx_len, d), x_parts[1].reshape(n_lat, lat_len, d), new_c,
            new_n.reshape(n_ctx, depth, 2, N_HEADS, HEAD_DIM), new_m.reshape(n_ctx, depth, 2, N_HEADS))
```
